```python
import math
import jax, jax.numpy as jnp
from jax import lax
import numpy as np

D_MODEL = 1024
BATCH = 32
SEQ = 2048
DEPTH = 1
DEC_BATCH = 16
DEC_SEQ = 16
PAST_LEN = 4096

CHUNK = 64
HEAD_DIM = 64
H_A = 8
H_B = 8
W_A = H_A * HEAD_DIM
W_B = H_B * HEAD_DIM
BAND_CHUNKS = 8
BAND_PAST = BAND_CHUNKS * CHUNK
BAND_LEN = BAND_PAST + CHUNK
MAX_REL = 128
Q_BLOCK = 128
PLE_DIM = 256
PEER_HEADS = 8
PEER_DK = 128
N_KEYS = 128
N_EXPERTS = N_KEYS * N_KEYS
PEER_TOPK = 16
PEER_TOKEN_BLOCK = 256
RMS_EPS = 1e-6
FORGET_BIAS_INIT = 3.0
IN_W = 3 * W_A + H_A + 3 * W_B + 2 * D_MODEL
IN_SPLITS = [int(s) for s in np.cumsum([W_A, W_A, W_A, H_A, W_B, W_B, W_B, D_MODEL])]

kernel_name = "fox_chunkband_peer_streaming_encoder"


def rms_norm(x, g):
    xf = x.astype(jnp.float32)
    y = xf * lax.rsqrt(jnp.mean(xf * xf, axis=-1, keepdims=True) + RMS_EPS)
    return (y * g.astype(jnp.float32)).astype(x.dtype)


def mixer_projections(h, g_mix, w_in, b_f, qn_a, kn_a, qn_b, kn_b):
    B, T, _ = h.shape
    n1 = rms_norm(h, g_mix)
    z = n1 @ w_in
    qa, ka, va, fl, qb, kb, vb, ga, gb = jnp.split(z, IN_SPLITS, axis=-1)
    qa = rms_norm(qa.reshape(B, T, H_A, HEAD_DIM), qn_a)
    ka = rms_norm(ka.reshape(B, T, H_A, HEAD_DIM), kn_a)
    va = va.reshape(B, T, H_A, HEAD_DIM)
    qb = rms_norm(qb.reshape(B, T, H_B, HEAD_DIM), qn_b)
    kb = rms_norm(kb.reshape(B, T, H_B, HEAD_DIM), kn_b)
    vb = vb.reshape(B, T, H_B, HEAD_DIM)
    log_f = jax.nn.log_sigmoid(fl.astype(jnp.float32) + b_f.astype(jnp.float32))
    return qa, ka, va, log_f, qb, kb, vb, jax.nn.sigmoid(ga), jax.nn.sigmoid(gb)


def forgetting_attention_prompt(q, k, v, log_f):
    B, T, H, Dh = q.shape
    c = jnp.cumsum(log_f, axis=1)
    cT = c.transpose(0, 2, 1)
    nb = T // Q_BLOCK
    qb = q.reshape(B, nb, Q_BLOCK, H, Dh).transpose(1, 0, 2, 3, 4)
    cb = cT.reshape(B, H, nb, Q_BLOCK).transpose(2, 0, 1, 3)
    kpos = jnp.arange(T)
    scale = HEAD_DIM ** -0.5

    def block(args):
        i, qi, ci = args
        s = jnp.einsum('bqhd,bkhd->bhqk', qi, k).astype(jnp.float32) * scale
        s = s + ci[..., None] - cT[:, :, None, :]
        qpos = i * Q_BLOCK + jnp.arange(Q_BLOCK)
        s = jnp.where(kpos[None, :] <= qpos[:, None], s, -jnp.inf)
        p = jax.nn.softmax(s, axis=-1).astype(v.dtype)
        return jnp.einsum('bhqk,bkhd->bqhd', p, v)

    out = lax.map(block, (jnp.arange(nb), qb, cb))
    return out.transpose(1, 0, 2, 3, 4).reshape(B, T, H * Dh)


def forgetting_attention_sample(q, k_new, v_new, logf_new, k_cache, v_cache, logf_cache):
    B, n, H, Dh = q.shape
    P = k_cache.shape[1]
    k = jnp.concatenate([k_cache.astype(k_new.dtype), k_new], axis=1)
    v = jnp.concatenate([v_cache.astype(v_new.dtype), v_new], axis=1)
    c = jnp.cumsum(jnp.concatenate([logf_cache.astype(jnp.float32), logf_new], axis=1), axis=1)
    cT = c.transpose(0, 2, 1)
    s = jnp.einsum('bqhd,bkhd->bhqk', q, k).astype(jnp.float32) * (HEAD_DIM ** -0.5)
    s = s + cT[:, :, P:, None] - cT[:, :, None, :]
    mask = jnp.arange(P + n)[None, :] <= (P + jnp.arange(n))[:, None]
    s = jnp.where(mask, s, -jnp.inf)
    p = jax.nn.softmax(s, axis=-1).astype(v.dtype)
    return jnp.einsum('bhqk,bkhd->bqhd', p, v).reshape(B, n, H * Dh)


def rel_bias_lookup(rel_bias, rel):
    idx = jnp.clip(rel, -MAX_REL, MAX_REL) + MAX_REL
    return rel_bias[:, idx].astype(jnp.float32)


def chunk_band_prompt(q, k, v, rel_bias):
    B, T, H, Dh = q.shape
    nc = T // CHUNK
    kp = jnp.pad(k, ((0, 0), (BAND_PAST, 0), (0, 0), (0, 0)))
    vp = jnp.pad(v, ((0, 0), (BAND_PAST, 0), (0, 0), (0, 0)))
    qc = q.reshape(B, nc, CHUNK, H, Dh).transpose(1, 0, 2, 3, 4)
    i = jnp.arange(CHUNK)
    j = jnp.arange(BAND_LEN)
    bias = rel_bias_lookup(rel_bias, (j[None, :] - BAND_PAST) - i[:, None])
    scale = HEAD_DIM ** -0.5

    def chunk(args):
        c, qi = args
        kb = lax.dynamic_slice_in_dim(kp, c * CHUNK, BAND_LEN, axis=1)
        vb = lax.dynamic_slice_in_dim(vp, c * CHUNK, BAND_LEN, axis=1)
        s = jnp.einsum('bqhd,bkhd->bhqk', qi, kb).astype(jnp.float32) * scale + bias
        s = jnp.where(j >= BAND_PAST - c * CHUNK, s, -jnp.inf)
        p = jax.nn.softmax(s, axis=-1).astype(vb.dtype)
        return jnp.einsum('bhqk,bkhd->bqhd', p, vb)

    out = lax.map(chunk, (jnp.arange(nc), qc))
    return out.transpose(1, 0, 2, 3, 4).reshape(B, T, H * Dh)


def chunk_band_sample(q, k_new, v_new, k_cache, v_cache, rel_bias):
    B, n, H, Dh = q.shape
    L = k_cache.shape[1]
    k = jnp.concatenate([k_cache.astype(k_new.dtype), k_new], axis=1)
    v = jnp.concatenate([v_cache.astype(v_new.dtype), v_new], axis=1)
    i = jnp.arange(n)
    j = jnp.arange(L + n)
    bias = rel_bias_lookup(rel_bias, (j[None, :] - L) - i[:, None])
    s = jnp.einsum('bqhd,bkhd->bhqk', q, k).astype(jnp.float32) * (HEAD_DIM ** -0.5) + bias
    p = jax.nn.softmax(s, axis=-1).astype(v.dtype)
    return jnp.einsum('bhqk,bkhd->bqhd', p, v).reshape(B, n, H * Dh)


def peer_ffn(x, w_q, sub_keys, u, v):
    T, D = x.shape
    pad = (-T) % PEER_TOKEN_BLOCK
    xb = jnp.pad(x, ((0, pad), (0, 0))).reshape(-1, PEER_TOKEN_BLOCK, D)

    def block(xi):
        t = xi.shape[0]
        q = (xi @ w_q).reshape(t, PEER_HEADS, 2, PEER_DK // 2)
        s = jnp.einsum('thpc,hpnc->thpn', q, sub_keys).astype(jnp.float32)
        top_s, top_i = lax.top_k(s, PEER_TOPK)
        cand_s = (top_s[:, :, 0, :, None] + top_s[:, :, 1, None, :]).reshape(t, PEER_HEADS, PEER_TOPK * PEER_TOPK)
        cand_i = (top_i[:, :, 0, :, None] * N_KEYS + top_i[:, :, 1, None, :]).reshape(t, PEER_HEADS, PEER_TOPK * PEER_TOPK)
        best_s, pos = lax.top_k(cand_s, PEER_TOPK)
        idx = jnp.take_along_axis(cand_i, pos, axis=-1)
        g = jax.nn.softmax(best_s, axis=-1)
        hid = jax.nn.gelu(jnp.einsum('thkd,td->thk', u[idx], xi).astype(jnp.float32))
        wgt = (g * hid).astype(xi.dtype)
        return jnp.einsum('thk,thkd->td', wgt, v[idx])

    return lax.map(block, xb).reshape(-1, D)[:T]


def merge_and_channel(h, ya, yb, ga, gb, p_l, w_up_a, w_up_b, w_out, g_ffn, peer_wq, peer_subkeys,
                      peer_u, peer_v, g_ple, w_ple_gate, w_ple_proj):
    merged = ga * (ya @ w_up_a) + gb * (yb @ w_up_b)
    h = h + merged @ w_out
    B, T, D = h.shape
    n2 = rms_norm(h, g_ffn)
    h = h + peer_ffn(n2.reshape(B * T, D), peer_wq, peer_subkeys, peer_u, peer_v).reshape(B, T, D)
    gate = jax.nn.sigmoid(rms_norm(h, g_ple) @ w_ple_gate)
    return h + gate * (p_l @ w_ple_proj)


def setup_inputs(seed: int = 0) -> dict:
    key = jax.random.key(seed)
    ks = jax.random.split(key, 28)
    nrm = lambda k, shape, s=1.0: jax.random.normal(k, shape, jnp.float32) * s
    lb = min(BAND_PAST, PAST_LEN)
    return {
        "x_prompt": nrm(ks[0], (BATCH, SEQ, D_MODEL)),
        "x_sample": nrm(ks[1], (DEC_BATCH, DEC_SEQ, D_MODEL)),
        "cache_a_k": nrm(ks[2], (DEPTH, DEC_BATCH, PAST_LEN, H_A, HEAD_DIM)),
        "cache_a_v": nrm(ks[3], (DEPTH, DEC_BATCH, PAST_LEN, H_A, HEAD_DIM)),
        "cache_a_logf": jax.nn.log_sigmoid(nrm(ks[4], (DEPTH, DEC_BATCH, PAST_LEN, H_A)) + FORGET_BIAS_INIT),
        "cache_b_k": nrm(ks[5], (DEPTH, DEC_BATCH, lb, H_B, HEAD_DIM)),
        "cache_b_v": nrm(ks[6], (DEPTH, DEC_BATCH, lb, H_B, HEAD_DIM)),
        "p_prompt": nrm(ks[7], (DEPTH, BATCH, SEQ, PLE_DIM)),
        "p_sample": nrm(ks[8], (DEPTH, DEC_BATCH, DEC_SEQ, PLE_DIM)),
        "g_mix": 1.0 + nrm(ks[9], (DEPTH, D_MODEL), 0.02),
        "w_in": nrm(ks[10], (DEPTH, D_MODEL, IN_W), D_MODEL ** -0.5),
        "b_f": FORGET_BIAS_INIT + nrm(ks[11], (DEPTH, H_A), 0.1),
        "qn_a": 1.0 + nrm(ks[12], (DEPTH, HEAD_DIM), 0.02),
        "kn_a": 1.0 + nrm(ks[13], (DEPTH, HEAD_DIM), 0.02),
        "qn_b": 1.0 + nrm(ks[14], (DEPTH, HEAD_DIM), 0.02),
        "kn_b": 1.0 + nrm(ks[15], (DEPTH, HEAD_DIM), 0.02),
        "rel_bias_b": nrm(ks[16], (DEPTH, H_B, 2 * MAX_REL + 1), 0.1),
        "w_up_a": nrm(ks[17], (DEPTH, W_A, D_MODEL), W_A ** -0.5),
        "w_up_b": nrm(ks[18], (DEPTH, W_B, D_MODEL), W_B ** -0.5),
        "w_out": nrm(ks[19], (DEPTH, D_MODEL, D_MODEL), D_MODEL ** -0.5),
        "g_ffn": 1.0 + nrm(ks[20], (DEPTH, D_MODEL), 0.02),
        "peer_wq": nrm(ks[21], (DEPTH, D_MODEL, PEER_HEADS * PEER_DK), D_MODEL ** -0.5),
        "peer_subkeys": nrm(ks[22], (DEPTH, PEER_HEADS, 2, N_KEYS, PEER_DK // 2), (PEER_DK // 2) ** -0.5),
        "peer_u": nrm(ks[23], (DEPTH, N_EXPERTS, D_MODEL), D_MODEL ** -0.5),
        "peer_v": nrm(ks[24], (DEPTH, N_EXPERTS, D_MODEL), 0.25),
        "g_ple": 1.0 + nrm(ks[25], (DEPTH, D_MODEL), 0.02),
        "w_ple_gate": nrm(ks[26], (DEPTH, D_MODEL, D_MODEL), D_MODEL ** -0.5),
        "w_ple_proj": nrm(ks[27], (DEPTH, PLE_DIM, D_MODEL), PLE_DIM ** -0.5),
    }


def reference(x_prompt, x_sample, cache_a_k, cache_a_v, cache_a_logf, cache_b_k, cache_b_v, p_prompt, p_sample,
              g_mix, w_in, b_f, qn_a, kn_a, qn_b, kn_b, rel_bias_b, w_up_a, w_up_b, w_out, g_ffn,
              peer_wq, peer_subkeys, peer_u, peer_v, g_ple, w_ple_gate, w_ple_proj):
    hp, hs = x_prompt, x_sample
    akp, avp, afp, bkp, bvp = [], [], [], [], []
    aks, avs, afs, bks, bvs = [], [], [], [], []
    for l in range(DEPTH):
        proj = (g_mix[l], w_in[l], b_f[l], qn_a[l], kn_a[l], qn_b[l], kn_b[l])
        chan = (w_up_a[l], w_up_b[l], w_out[l], g_ffn[l], peer_wq[l], peer_subkeys[l], peer_u[l], peer_v[l],
                g_ple[l], w_ple_gate[l], w_ple_proj[l])
        qa, ka, va, lf, qb, kb, vb, ga, gb = mixer_projections(hp, *proj)
        ya = forgetting_attention_prompt(qa, ka, va, lf)
        yb = chunk_band_prompt(qb, kb, vb, rel_bias_b[l])
        hp = merge_and_channel(hp, ya, yb, ga, gb, p_prompt[l], *chan)
        akp.append(ka); avp.append(va); afp.append(lf)
        bkp.append(kb[:, -BAND_PAST:]); bvp.append(vb[:, -BAND_PAST:])
        qa, ka, va, lf, qb, kb, vb, ga, gb = mixer_projections(hs, *proj)
        ya = forgetting_attention_sample(qa, ka, va, lf, cache_a_k[l], cache_a_v[l], cache_a_logf[l])
        yb = chunk_band_sample(qb, kb, vb, cache_b_k[l], cache_b_v[l], rel_bias_b[l])
        hs = merge_and_channel(hs, ya, yb, ga, gb, p_sample[l], *chan)
        aks.append(ka); avs.append(va); afs.append(lf); bks.append(kb); bvs.append(vb)
    return (hp, hs,
            jnp.stack(akp), jnp.stack(avp), jnp.stack(afp), jnp.stack(bkp), jnp.stack(bvp),
            jnp.stack(aks), jnp.stack(avs), jnp.stack(afs), jnp.stack(bks), jnp.stack(bvs))
```

```python
import functools
import math

import jax
import jax.numpy as jnp
from jax import lax
from jax.experimental import pallas as pl
from jax.experimental.pallas import tpu as pltpu

F32 = jnp.float32
BF16 = jnp.bfloat16

D_MODEL = 1024
HEAD_DIM = 64
N_HEADS = 8
W_MIX = N_HEADS * HEAD_DIM
PAIR = 2 * HEAD_DIM
N_PAIRS = N_HEADS // 2
CHUNK = 64
BAND_PAST = 8 * CHUNK
MAX_REL = 128
PLE_DIM = 256
PEER_HEADS = 8
N_KEYS = 128
N_EXPERTS = N_KEYS * N_KEYS
TOPK = 16
RMS_EPS = 1e-6
ATT_SCALE = HEAD_DIM ** -0.5
NEG = -1e30
LANES = 128
VMEM_LIMIT = 56 * 1024 * 1024

N_CAND = 16 + 7 * 8 + 8


def _params(sem, vmem=VMEM_LIMIT):
    return pltpu.CompilerParams(dimension_semantics=sem, vmem_limit_bytes=vmem)


def _rms(x, g):
    return x * lax.rsqrt(jnp.mean(x * x, axis=-1, keepdims=True) + RMS_EPS) * g


def _dot(a, b):
    return jnp.dot(a, b, preferred_element_type=F32)


def _dot_t(a, b):
    return lax.dot_general(a, b, (((1,), (1,)), ((), ())), preferred_element_type=F32)


def _const_spec(shape):
    nd = len(shape)
    return pl.BlockSpec(shape, lambda *_: (0,) * nd)


def _proj_body(x_ref, g_ref, wa_ref, wf_ref, wb_ref, wg_ref, bf_ref, qna_ref, kna_ref, qnb_ref, knb_ref, msum_ref,
               qa_ref, ka_ref, va_ref, lf_ref, qb_ref, kb_ref, vb_ref, ga_ref, gb_ref):
    n1 = _rms(x_ref[...], g_ref[...]).astype(BF16)

    def head_norm(z, gain):
        ms = _dot((z * z).astype(BF16), msum_ref[...])
        return z * lax.rsqrt(ms + RMS_EPS) * gain

    za = _dot(n1, wa_ref[...])
    qa_ref[...] = (head_norm(za[:, :W_MIX], qna_ref[...]) * ATT_SCALE).astype(BF16)
    ka_ref[...] = head_norm(za[:, W_MIX:2 * W_MIX], kna_ref[...])
    va_ref[...] = za[:, 2 * W_MIX:]
    zb = _dot(n1, wb_ref[...])
    qb_ref[...] = (head_norm(zb[:, :W_MIX], qnb_ref[...]) * ATT_SCALE).astype(BF16)
    kb_ref[...] = head_norm(zb[:, W_MIX:2 * W_MIX], knb_ref[...])
    vb_ref[...] = zb[:, 2 * W_MIX:]
    fl = _dot(n1, wf_ref[...]) + bf_ref[...]
    ls = jnp.minimum(fl, 0.0) - jnp.log1p(jnp.exp(-jnp.abs(fl)))
    lf_ref[...] = ls[:, :N_HEADS]
    zg = _dot(n1, wg_ref[...])
    sg = 1.0 / (1.0 + jnp.exp(-zg))
    ga_ref[...] = sg[:, :D_MODEL].astype(BF16)
    gb_ref[...] = sg[:, D_MODEL:].astype(BF16)


def _proj(x, w, tb):
    n = x.shape[0]
    row = lambda width: pl.BlockSpec((tb, width), lambda i: (i, 0))
    ins = [x, w["g_mix"], w["w_a"], w["w_f"], w["w_b"], w["w_g"], w["b_f"], w["qn_a"], w["kn_a"], w["qn_b"],
           w["kn_b"], w["msum"]]
    in_specs = [row(D_MODEL)] + [_const_spec(a.shape) for a in ins[1:]]
    widths = [(W_MIX, BF16), (W_MIX, F32), (W_MIX, F32), (N_HEADS, F32), (W_MIX, BF16), (W_MIX, F32), (W_MIX, F32),
              (D_MODEL, BF16), (D_MODEL, BF16)]
    return pl.pallas_call(
        _proj_body,
        grid=(n // tb,),
        in_specs=in_specs,
        out_specs=[row(wd) for wd, _ in widths],
        out_shape=[jax.ShapeDtypeStruct((n, wd), dt) for wd, dt in widths],
        compiler_params=_params(("parallel",)),
        name="proj",
    )(*ins)


def _cumsum_lanes(x):
    n = x.shape[-1]
    lane = lax.broadcasted_iota(jnp.int32, x.shape, x.ndim - 1)
    s = 1
    while s < n:
        x = x + jnp.where(lane >= s, pltpu.roll(x, s, axis=x.ndim - 1), 0.0)
        s *= 2
    return x


def _head_of_pair(x, hh):
    lane = lax.broadcasted_iota(jnp.int32, (1, PAIR), 1)
    keep = (lane < HEAD_DIM) if hh == 0 else (lane >= HEAD_DIM)
    return jnp.where(keep, x, jnp.zeros_like(x))


def _merge_pair(o0, o1):
    lane = lax.broadcasted_iota(jnp.int32, (1, PAIR), 1)
    return jnp.where(lane < HEAD_DIM, o0, o1)


def _fox_body(q_ref, k_ref, v_ref, lft_ref, o_ref, kb_s, vb_s, c_s, *, tq, nq):
    hp = pl.program_id(1)
    qi = pl.program_id(2)

    @pl.when(qi == 0)
    def _():
        kb_s[...] = k_ref[0].astype(BF16)
        vb_s[...] = v_ref[0].astype(BF16)
        c = _cumsum_lanes(lft_ref[0])
        for j in range(nq):
            c_s[j] = c[:, j * tq:(j + 1) * tq]

    q = q_ref[0]
    row = lax.broadcasted_iota(jnp.int32, (tq, tq), 0)
    col = lax.broadcasted_iota(jnp.int32, (tq, tq), 1)
    outs = []
    for hh in range(2):
        qh = _head_of_pair(q, hh)
        h = 2 * hp + hh

        def step(kb, carry, masked):
            m, l, acc = carry
            off = pl.multiple_of(kb * tq, tq)
            s = _dot_t(qh, kb_s[pl.ds(off, tq), :])
            s = s - c_s[kb, pl.ds(h, 1), :]
            if masked:
                s = jnp.where(col <= row, s, -jnp.inf)
            m_new = jnp.maximum(m, jnp.max(s, axis=-1, keepdims=True))
            alpha = jnp.exp(m - m_new)
            p = jnp.exp(s - m_new)
            l = alpha * l + jnp.sum(p, axis=-1, keepdims=True)
            acc = alpha * acc + _dot(p.astype(BF16), vb_s[pl.ds(off, tq), :])
            return m_new, l, acc

        init = (jnp.full((tq, 1), -jnp.inf, F32), jnp.zeros((tq, 1), F32), jnp.zeros((tq, PAIR), F32))
        carry = lax.fori_loop(0, qi, lambda kb, c: step(kb, c, False), init)
        m, l, acc = step(qi, carry, True)
        outs.append(acc / l)
    o_ref[0] = _merge_pair(outs[0], outs[1]).astype(o_ref.dtype)


def _fox_prompt(q, k, v, lft, tq):
    b, t, _ = q.shape
    nq = t // tq
    return pl.pallas_call(
        functools.partial(_fox_body, tq=tq, nq=nq),
        grid=(b, N_PAIRS, nq),
        in_specs=[
            pl.BlockSpec((1, tq, PAIR), lambda i, p, j: (i, j, p)),
            pl.BlockSpec((1, t, PAIR), lambda i, p, j: (i, 0, p)),
            pl.BlockSpec((1, t, PAIR), lambda i, p, j: (i, 0, p)),
            pl.BlockSpec((1, N_HEADS, t), lambda i, p, j: (i, 0, 0)),
        ],
        out_specs=pl.BlockSpec((1, tq, PAIR), lambda i, p, j: (i, j, p)),
        out_shape=jax.ShapeDtypeStruct((b, t, W_MIX), BF16),
        scratch_shapes=[pltpu.VMEM((t, PAIR), BF16), pltpu.VMEM((t, PAIR), BF16),
                        pltpu.VMEM((nq, N_HEADS, tq), F32)],
        compiler_params=_params(("parallel", "parallel", "arbitrary")),
        name="fox_prompt",
    )(q, k, v, lft)


def _band_body(q_ref, k_ref, v_ref, bias_ref, o_ref, kp_s, vp_s, *, tq):
    qi = pl.program_id(2)
    win = tq + BAND_PAST

    @pl.when(qi == 0)
    def _():
        zeros = jnp.zeros((BAND_PAST, PAIR), BF16)
        kp_s[:BAND_PAST, :] = zeros
        vp_s[:BAND_PAST, :] = zeros
        kp_s[BAND_PAST:, :] = k_ref[0].astype(BF16)
        vp_s[BAND_PAST:, :] = v_ref[0].astype(BF16)

    off = pl.multiple_of(qi * tq, tq)
    kw = kp_s[pl.ds(off, win), :]
    vw = vp_s[pl.ds(off, win), :]
    q = q_ref[0]
    exists = lax.broadcasted_iota(jnp.int32, (1, win), 1) >= BAND_PAST - qi * tq
    outs = []
    for hh in range(2):
        s = _dot_t(_head_of_pair(q, hh), kw) + bias_ref[hh]
        s = jnp.where(exists, s, NEG)
        m = jnp.max(s, axis=-1, keepdims=True)
        p = jnp.exp(s - m)
        l = jnp.sum(p, axis=-1, keepdims=True)
        outs.append(_dot(p.astype(BF16), vw) / l)
    o_ref[0] = _merge_pair(outs[0], outs[1]).astype(o_ref.dtype)


def _band_prompt(q, k, v, bias, tq):
    b, t, _ = q.shape
    win = tq + BAND_PAST
    return pl.pallas_call(
        functools.partial(_band_body, tq=tq),
        grid=(b, N_PAIRS, t // tq),
        in_specs=[
            pl.BlockSpec((1, tq, PAIR), lambda i, p, j: (i, j, p)),
            pl.BlockSpec((1, t, PAIR), lambda i, p, j: (i, 0, p)),
            pl.BlockSpec((1, t, PAIR), lambda i, p, j: (i, 0, p)),
            pl.BlockSpec((2, tq, win), lambda i, p, j: (p, 0, 0)),
        ],
        out_specs=pl.BlockSpec((1, tq, PAIR), lambda i, p, j: (i, j, p)),
        out_shape=jax.ShapeDtypeStruct((b, t, W_MIX), BF16),
        scratch_shapes=[pltpu.VMEM((t + BAND_PAST, PAIR), BF16), pltpu.VMEM((t + BAND_PAST, PAIR), BF16)],
        compiler_params=_params(("parallel", "parallel", "arbitrary")),
        name="band_prompt",
    )(q, k, v, bias)


def _band_bias_prompt(rel_bias, tq):
    win = tq + BAND_PAST
    ii = jnp.arange(tq)[:, None]
    jj = jnp.arange(win)[None, :]
    rel = jj - BAND_PAST - ii
    table = rel_bias[:, jnp.clip(rel, -MAX_REL, MAX_REL) + MAX_REL].astype(F32)
    lo = (ii // CHUNK) * CHUNK
    in_band = (jj >= lo) & (jj < lo + BAND_PAST + CHUNK)
    return jnp.where(in_band[None], table, NEG)


def _fox_sample_body(q_ref, kc_ref, vc_ref, kn_ref, vn_ref, lft_ref, o_ref, *, past, n_new):
    hp = pl.program_id(1)
    c = _cumsum_lanes(lft_ref[0])
    kc = kc_ref[0].astype(BF16)
    vc = vc_ref[0].astype(BF16)
    kn = kn_ref[0].astype(BF16)
    vn = vn_ref[0].astype(BF16)
    q = q_ref[0]
    nq = q.shape[0]
    row = lax.broadcasted_iota(jnp.int32, (nq, LANES), 0)
    col = lax.broadcasted_iota(jnp.int32, (nq, LANES), 1)
    outs = []
    for hh in range(2):
        qh = _head_of_pair(q, hh)
        sel = lax.broadcasted_iota(jnp.int32, (N_HEADS, 1), 0) == 2 * hp + hh
        crow = jnp.sum(jnp.where(sel, c, 0.0), axis=0, keepdims=True)
        sc = _dot_t(qh, kc) - crow[:, :past]
        sn = _dot_t(qh, kn) - crow[:, past:]
        sn = jnp.where((col <= row) & (col < n_new), sn, -jnp.inf)
        m = jnp.maximum(jnp.max(sc, axis=-1, keepdims=True), jnp.max(sn, axis=-1, keepdims=True))
        pc = jnp.exp(sc - m)
        pn = jnp.exp(sn - m)
        l = jnp.sum(pc, axis=-1, keepdims=True) + jnp.sum(pn, axis=-1, keepdims=True)
        outs.append((_dot(pc.astype(BF16), vc) + _dot(pn.astype(BF16), vn)) / l)
    o_ref[0] = _merge_pair(outs[0], outs[1]).astype(o_ref.dtype)


def _fox_sample(q, kc, vc, kn, vn, lft, n_new):
    b, nq, _ = q.shape
    past = kc.shape[1]
    pair_spec = lambda rows: pl.BlockSpec((1, rows, PAIR), lambda i, p: (i, 0, p))
    return pl.pallas_call(
        functools.partial(_fox_sample_body, past=past, n_new=n_new),
        grid=(b, N_PAIRS),
        in_specs=[pair_spec(nq), pair_spec(past), pair_spec(past), pair_spec(LANES), pair_spec(LANES),
                  pl.BlockSpec((1, N_HEADS, past + LANES), lambda i, p: (i, 0, 0))],
        out_specs=pair_spec(nq),
        out_shape=jax.ShapeDtypeStruct((b, nq, W_MIX), BF16),
        compiler_params=_params(("parallel", "parallel")),
        name="fox_sample",
    )(q, kc, vc, kn, vn, lft)


def _band_sample_body(q_ref, kc_ref, vc_ref, kn_ref, vn_ref, bias_ref, o_ref, *, past):
    kc = kc_ref[0].astype(BF16)
    vc = vc_ref[0].astype(BF16)
    kn = kn_ref[0].astype(BF16)
    vn = vn_ref[0].astype(BF16)
    q = q_ref[0]
    outs = []
    for hh in range(2):
        qh = _head_of_pair(q, hh)
        bias = bias_ref[hh]
        sc = _dot_t(qh, kc) + bias[:, :past]
        sn = _dot_t(qh, kn) + bias[:, past:]
        m = jnp.maximum(jnp.max(sc, axis=-1, keepdims=True), jnp.max(sn, axis=-1, keepdims=True))
        pc = jnp.exp(sc - m)
        pn = jnp.exp(sn - m)
        l = jnp.sum(pc, axis=-1, keepdims=True) + jnp.sum(pn, axis=-1, keepdims=True)
        outs.append((_dot(pc.astype(BF16), vc) + _dot(pn.astype(BF16), vn)) / l)
    o_ref[0] = _merge_pair(outs[0], outs[1]).astype(o_ref.dtype)


def _band_sample(q, kc, vc, kn, vn, bias):
    b, nq, _ = q.shape
    past = kc.shape[1]
    pair_spec = lambda rows: pl.BlockSpec((1, rows, PAIR), lambda i, p: (i, 0, p))
    return pl.pallas_call(
        functools.partial(_band_sample_body, past=past),
        grid=(b, N_PAIRS),
        in_specs=[pair_spec(nq), pair_spec(past), pair_spec(past), pair_spec(LANES), pair_spec(LANES),
                  pl.BlockSpec((2, nq, past + LANES), lambda i, p: (p, 0, 0))],
        out_specs=pair_spec(nq),
        out_shape=jax.ShapeDtypeStruct((b, nq, W_MIX), BF16),
        compiler_params=_params(("parallel", "parallel")),
        name="band_sample",
    )(q, kc, vc, kn, vn, bias)


def _band_bias_sample(rel_bias, n_new, past):
    ii = jnp.arange(n_new)[:, None]
    jj = jnp.arange(past + LANES)[None, :]
    table = rel_bias[:, jnp.clip(jj - past - ii, -MAX_REL, MAX_REL) + MAX_REL].astype(F32)
    return jnp.where((jj < past + n_new)[None], table, NEG)


def _merge_body(x_ref, ya_ref, yb_ref, ga_ref, gb_ref, wua_ref, wub_ref, wo_ref, gffn_ref, h_ref, n2_ref):
    merged = (ga_ref[...].astype(F32) * _dot(ya_ref[...], wua_ref[...])
              + gb_ref[...].astype(F32) * _dot(yb_ref[...], wub_ref[...]))
    h = x_ref[...] + _dot(merged.astype(BF16), wo_ref[...])
    h_ref[...] = h
    n2_ref[...] = _rms(h, gffn_ref[...]).astype(BF16)


def _merge(x, ya, yb, ga, gb, w, tb):
    n = x.shape[0]
    row = lambda width: pl.BlockSpec((tb, width), lambda i: (i, 0))
    consts = [w["w_up_a"], w["w_up_b"], w["w_out"], w["g_ffn"]]
    return pl.pallas_call(
        _merge_body,
        grid=(n // tb,),
        in_specs=[row(D_MODEL), row(W_MIX), row(W_MIX), row(D_MODEL), row(D_MODEL)]
        + [_const_spec(a.shape) for a in consts],
        out_specs=[row(D_MODEL), row(D_MODEL)],
        out_shape=[jax.ShapeDtypeStruct((n, D_MODEL), F32), jax.ShapeDtypeStruct((n, D_MODEL), BF16)],
        compiler_params=_params(("parallel",)),
        name="merge",
    )(x, ya, yb, ga, gb, *consts)


def _top16(s, vals_ref):
    rank = jnp.full(s.shape, float(TOPK), F32)
    work = s
    for r in range(TOPK):
        m = jnp.max(work, axis=0, keepdims=True)
        hit = work == m
        rank = jnp.where(hit, float(r), rank)
        work = jnp.where(hit, -jnp.inf, work)
        vals_ref[r:r + 1, :] = m
    return rank


def _retrieve_body(n2_ref, wq_ref, sk_ref, r1_ref, b_ref, a_ref, c_ref, va_s, vb_s):
    q = _dot(n2_ref[...], wq_ref[...]).astype(BF16)
    for h in range(PEER_HEADS):
        qh = q[:, h * PAIR:(h + 1) * PAIR]
        s0 = _dot_t(sk_ref[h, 0], qh)
        s1 = _dot_t(sk_ref[h, 1], qh)
        rank0 = _top16(s0, va_s)
        rank1 = _top16(s1, vb_s)
        va = va_s[...]
        vb = vb_s[...]
        cand = jnp.concatenate([va[0:1] + vb] + [va[k:k + 1] + vb[0:8] for k in range(1, 8)] + [va[8:16] + vb[0:1]],
                               axis=0)
        work = cand
        picked = jnp.zeros(cand.shape, jnp.bool_)
        for _ in range(TOPK):
            m = jnp.max(work, axis=0, keepdims=True)
            hit = work == m
            picked = picked | hit
            work = jnp.where(hit, -jnp.inf, work)
        top = va[0:1] + vb[0:1]
        z = jnp.sum(jnp.where(picked, jnp.exp(cand - top), 0.0), axis=0, keepdims=True)
        pickf = picked.astype(F32)
        counts = [jnp.sum(pickf[0:16], axis=0, keepdims=True)]
        counts += [jnp.sum(pickf[8 + 8 * k:16 + 8 * k], axis=0, keepdims=True) for k in range(1, 8)]
        counts += [pickf[72 + k:73 + k] for k in range(8)]
        c = jnp.zeros(s0.shape, F32)
        for k in range(TOPK):
            c = jnp.where(rank0 == float(k), counts[k], c)
        a = jnp.where(rank0 < float(TOPK), jnp.exp(s0 - va[0:1]) / z, 0.0)
        b = jnp.where(rank1 < float(TOPK), jnp.exp(s1 - vb[0:1]), 0.0)
        r1_ref[h] = rank1.astype(BF16)
        b_ref[h] = b.astype(BF16)
        a_ref[h] = a
        c_ref[h] = c


def _retrieve(n2, w, tb):
    n = n2.shape[0]
    maps = pl.BlockSpec((PEER_HEADS, N_KEYS, tb), lambda i: (0, 0, i))
    return pl.pallas_call(
        _retrieve_body,
        grid=(n // tb,),
        in_specs=[pl.BlockSpec((tb, D_MODEL), lambda i: (i, 0)), _const_spec(w["peer_wq"].shape),
                  _const_spec(w["peer_sk"].shape)],
        out_specs=[maps, maps, maps, maps],
        out_shape=[jax.ShapeDtypeStruct((PEER_HEADS, N_KEYS, n), BF16), jax.ShapeDtypeStruct((PEER_HEADS, N_KEYS, n), BF16),
                   jax.ShapeDtypeStruct((PEER_HEADS, N_KEYS, n), F32), jax.ShapeDtypeStruct((PEER_HEADS, N_KEYS, n), F32)],
        scratch_shapes=[pltpu.VMEM((TOPK, tb), F32), pltpu.VMEM((TOPK, tb), F32)],
        compiler_params=_params(("parallel",)),
        name="peer_retrieve",
    )(n2, w["peer_wq"], w["peer_sk"])


def _gelu_tanh(x):
    return 0.5 * x * (1.0 + jnp.tanh(math.sqrt(2.0 / math.pi) * (x + 0.044715 * (x * x * x))))


def _dense_body(n2_ref, u_ref, vt_ref, r1_ref, b_ref, a_ref, c_ref, o_ref, acc_s, r1_s, b_s, act_s, w_s, *, ni, tb):
    g = pl.program_id(1)

    @pl.when(g == 0)
    def _():
        acc_s[...] = jnp.zeros(acc_s.shape, F32)
        r1_s[...] = r1_ref[...].astype(F32)
        b_s[...] = b_ref[...].astype(F32)

    act_s[...] = _gelu_tanh(_dot_t(u_ref[...], n2_ref[...]))

    def row_block(ii, carry):
        off = pl.multiple_of(ii * N_KEYS, N_KEYS)
        for tt in range(tb // LANES):
            lanes = slice(tt * LANES, (tt + 1) * LANES)
            gate = jnp.zeros((N_KEYS, LANES), F32)
            for h in range(PEER_HEADS):
                cnt = c_ref[0, ii, h:h + 1, lanes]
                wgt = a_ref[0, ii, h:h + 1, lanes]
                gate = gate + jnp.where(r1_s[h, :, lanes] < cnt, b_s[h, :, lanes] * wgt, 0.0)
            w_s[pl.ds(off, N_KEYS), lanes] = (gate * act_s[pl.ds(off, N_KEYS), lanes]).astype(BF16)
        return carry

    lax.fori_loop(0, ni, row_block, 0)
    acc_s[...] += _dot(vt_ref[...], w_s[...])

    @pl.when(g == pl.num_programs(1) - 1)
    def _():
        o_ref[...] = acc_s[...].T


def _dense(n2, u, vt, r1, b, a, c, tb, ni):
    n = n2.shape[0]
    ec = ni * N_KEYS
    ng = N_KEYS // ni
    regroup = lambda m: m.reshape(PEER_HEADS, ng, ni, n).transpose(1, 2, 0, 3)
    maps = pl.BlockSpec((PEER_HEADS, N_KEYS, tb), lambda i, g: (0, 0, i))
    rows = pl.BlockSpec((1, ni, PEER_HEADS, tb), lambda i, g: (g, 0, 0, i))
    return pl.pallas_call(
        functools.partial(_dense_body, ni=ni, tb=tb),
        grid=(n // tb, ng),
        in_specs=[pl.BlockSpec((tb, D_MODEL), lambda i, g: (i, 0)),
                  pl.BlockSpec((ec, D_MODEL), lambda i, g: (g, 0)),
                  pl.BlockSpec((D_MODEL, ec), lambda i, g: (0, g)),
                  maps, maps, rows, rows],
        out_specs=pl.BlockSpec((tb, D_MODEL), lambda i, g: (i, 0)),
        out_shape=jax.ShapeDtypeStruct((n, D_MODEL), F32),
        scratch_shapes=[pltpu.VMEM((D_MODEL, tb), F32), pltpu.VMEM((PEER_HEADS, N_KEYS, tb), F32),
                        pltpu.VMEM((PEER_HEADS, N_KEYS, tb), F32), pltpu.VMEM((ec, tb), F32),
                        pltpu.VMEM((ec, tb), BF16)],
        compiler_params=_params(("parallel", "arbitrary")),
        name="peer_dense",
    )(n2, u, vt, r1, b, regroup(a), regroup(c))


def _final_body(h_ref, peer_ref, p_ref, gple_ref, wg_ref, wp_ref, o_ref):
    h = h_ref[...] + peer_ref[...]
    zg = _dot(_rms(h, gple_ref[...]).astype(BF16), wg_ref[...])
    gate = 1.0 / (1.0 + jnp.exp(-zg))
    o_ref[...] = h + gate * _dot(p_ref[...].astype(BF16), wp_ref[...])


def _final(h, peer, p, w, tb):
    n = h.shape[0]
    row = lambda width: pl.BlockSpec((tb, width), lambda i: (i, 0))
    consts = [w["g_ple"], w["w_ple_gate"], w["w_ple_proj"]]
    return pl.pallas_call(
        _final_body,
        grid=(n // tb,),
        in_specs=[row(D_MODEL), row(D_MODEL), row(PLE_DIM)] + [_const_spec(a.shape) for a in consts],
        out_specs=row(D_MODEL),
        out_shape=jax.ShapeDtypeStruct((n, D_MODEL), F32),
        compiler_params=_params(("parallel",)),
        name="final",
    )(h, peer, p, *consts)


def _layer_weights(l, g_mix, w_in, b_f, qn_a, kn_a, qn_b, kn_b, w_up_a, w_up_b, w_out, g_ffn, peer_wq, peer_subkeys,
                   peer_u, peer_v, g_ple, w_ple_gate, w_ple_proj):
    o_f = 3 * W_MIX
    o_b = o_f + N_HEADS
    o_g = o_b + 3 * W_MIX
    wi = w_in[l]
    tile_heads = lambda g: jnp.tile(g[l].astype(F32), N_HEADS)[None, :]
    head_of = jnp.arange(W_MIX) // HEAD_DIM
    sk = peer_subkeys[l].astype(BF16)
    zeros = jnp.zeros_like(sk[:, 0])
    sk_pad = jnp.stack([jnp.concatenate([sk[:, 0], zeros], axis=-1), jnp.concatenate([zeros, sk[:, 1]], axis=-1)],
                       axis=1)
    return {
        "g_mix": g_mix[l][None, :],
        "w_a": wi[:, :o_f].astype(BF16),
        "w_f": jnp.pad(wi[:, o_f:o_b], ((0, 0), (0, LANES - N_HEADS))).astype(BF16),
        "w_b": wi[:, o_b:o_g].astype(BF16),
        "w_g": wi[:, o_g:].astype(BF16),
        "b_f": jnp.pad(b_f[l], (0, LANES - N_HEADS))[None, :],
        "qn_a": tile_heads(qn_a), "kn_a": tile_heads(kn_a), "qn_b": tile_heads(qn_b), "kn_b": tile_heads(kn_b),
        "msum": jnp.where(head_of[:, None] == head_of[None, :], 1.0 / HEAD_DIM, 0.0).astype(BF16),
        "w_up_a": w_up_a[l].astype(BF16), "w_up_b": w_up_b[l].astype(BF16), "w_out": w_out[l].astype(BF16),
        "g_ffn": g_ffn[l][None, :],
        "peer_wq": peer_wq[l].astype(BF16),
        "peer_sk": sk_pad,
        "peer_u": peer_u[l].astype(BF16),
        "peer_vt": peer_v[l].astype(BF16).T,
        "g_ple": g_ple[l][None, :],
        "w_ple_gate": w_ple_gate[l].astype(BF16),
        "w_ple_proj": w_ple_proj[l].astype(BF16),
    }


def _channel(x, ya, yb, ga, gb, p, w, tb, tb_dense, ni):
    h1, n2 = _merge(x, ya, yb, ga, gb, w, tb)
    r1, b, a, c = _retrieve(n2, w, tb)
    peer = _dense(n2, w["peer_u"], w["peer_vt"], r1, b, a, c, tb_dense, ni)
    return _final(h1, peer, p, w, tb)


def _pad_rows(x, rows):
    return jnp.pad(x, ((0, 0), (0, rows - x.shape[1]), (0, 0)))


def kernel(x_prompt, x_sample, cache_a_k, cache_a_v, cache_a_logf, cache_b_k, cache_b_v, p_prompt, p_sample, g_mix, w_in, b_f, qn_a, kn_a, qn_b, kn_b, rel_bias_b, w_up_a, w_up_b, w_out, g_ffn, peer_wq, peer_subkeys, peer_u, peer_v, g_ple, w_ple_gate, w_ple_proj):
    depth = w_in.shape[0]
    bp, tp, _ = x_prompt.shape
    bs, ts, _ = x_sample.shape
    past = cache_a_k.shape[2]
    band_rows = min(BAND_PAST, tp)
    tq = 256
    hp = x_prompt.reshape(bp * tp, D_MODEL)
    hs = x_sample.reshape(bs * ts, D_MODEL)
    outs = [[] for _ in range(10)]
    for l in range(depth):
        w = _layer_weights(l, g_mix, w_in, b_f, qn_a, kn_a, qn_b, kn_b, w_up_a, w_up_b, w_out, g_ffn, peer_wq,
                           peer_subkeys, peer_u, peer_v, g_ple, w_ple_gate, w_ple_proj)
        qa, ka, va, lf, qb, kb, vb, ga, gb = _proj(hp, w, 256)
        as_seq = lambda z: z.reshape(bp, tp, z.shape[-1])
        ya = _fox_prompt(as_seq(qa), as_seq(ka), as_seq(va), as_seq(lf).transpose(0, 2, 1), tq)
        yb = _band_prompt(as_seq(qb), as_seq(kb), as_seq(vb), _band_bias_prompt(rel_bias_b[l], tq), tq)
        hp = _channel(hp, ya.reshape(-1, W_MIX), yb.reshape(-1, W_MIX), ga, gb, p_prompt[l].reshape(-1, PLE_DIM), w,
                      256, 512, 4)
        heads = lambda z, b_, t_: z.reshape(b_, t_, N_HEADS, HEAD_DIM)
        outs[0].append(heads(ka, bp, tp)); outs[1].append(heads(va, bp, tp)); outs[2].append(as_seq(lf))
        outs[3].append(heads(kb, bp, tp)[:, -band_rows:]); outs[4].append(heads(vb, bp, tp)[:, -band_rows:])
        qa, ka, va, lf, qb, kb, vb, ga, gb = _proj(hs, w, bs * ts)
        as_seq = lambda z: z.reshape(bs, ts, z.shape[-1])
        lft = jnp.concatenate([cache_a_logf[l].astype(F32), as_seq(lf),
                               jnp.zeros((bs, LANES - ts, N_HEADS), F32)], axis=1).transpose(0, 2, 1)
        flat_cache = lambda z: z[l].reshape(bs, z.shape[2], W_MIX)
        ya = _fox_sample(as_seq(qa), flat_cache(cache_a_k), flat_cache(cache_a_v), _pad_rows(as_seq(ka), LANES),
                         _pad_rows(as_seq(va), LANES), lft, ts)
        yb = _band_sample(as_seq(qb), flat_cache(cache_b_k), flat_cache(cache_b_v), _pad_rows(as_seq(kb), LANES),
                          _pad_rows(as_seq(vb), LANES), _band_bias_sample(rel_bias_b[l], ts, cache_b_k.shape[2]))
        hs = _channel(hs, ya.reshape(-1, W_MIX), yb.reshape(-1, W_MIX), ga, gb, p_sample[l].reshape(-1, PLE_DIM), w,
                      bs * ts, bs * ts, 4)
        outs[5].append(heads(ka, bs, ts)); outs[6].append(heads(va, bs, ts)); outs[7].append(as_seq(lf))
        outs[8].append(heads(kb, bs, ts)); outs[9].append(heads(vb, bs, ts))
    return (hp.reshape(bp, tp, D_MODEL), hs.reshape(bs, ts, D_MODEL)) + tuple(jnp.stack(o) for o in outs)
```

```python
import functools
import math

import jax
import jax.numpy as jnp
from jax import lax
from jax.experimental import pallas as pl
from jax.experimental.pallas import tpu as pltpu

F32 = jnp.float32
BF16 = jnp.bfloat16

D_MODEL = 1024
HEAD_DIM = 64
N_HEADS = 8
W_MIX = N_HEADS * HEAD_DIM
PAIR = 2 * HEAD_DIM
N_PAIRS = N_HEADS // 2
CHUNK = 64
BAND_PAST = 8 * CHUNK
MAX_REL = 128
PLE_DIM = 256
PEER_HEADS = 8
N_KEYS = 128
N_EXPERTS = N_KEYS * N_KEYS
TOPK = 16
RMS_EPS = 1e-6
ATT_SCALE = HEAD_DIM ** -0.5
NEG = -1e30
LANES = 128
SUBLANES = 8
VMEM_LIMIT = 56 * 1024 * 1024

N_CAND = 16 + 7 * 8 + 8


def _params(sem, vmem=VMEM_LIMIT):
    return pltpu.CompilerParams(dimension_semantics=sem, vmem_limit_bytes=vmem)


def _rms(x, g):
    return x * lax.rsqrt(jnp.mean(x * x, axis=-1, keepdims=True) + RMS_EPS) * g


def _dot(a, b):
    return jnp.dot(a, b, preferred_element_type=F32)


def _dot_t(a, b):
    return lax.dot_general(a, b, (((1,), (1,)), ((), ())), preferred_element_type=F32)


def _const_spec(shape):
    nd = len(shape)
    return pl.BlockSpec(shape, lambda *_: (0,) * nd)


def _proj_body(x_ref, g_ref, wa_ref, wf_ref, wb_ref, wg_ref, bf_ref, qna_ref, kna_ref, qnb_ref, knb_ref, msum_ref,
               qa_ref, ka_ref, va_ref, lf_ref, qb_ref, kb_ref, vb_ref, ga_ref, gb_ref):
    n1 = _rms(x_ref[...], g_ref[...]).astype(BF16)

    def head_norm(z, gain):
        ms = _dot((z * z).astype(BF16), msum_ref[...])
        return z * lax.rsqrt(ms + RMS_EPS) * gain

    za = _dot(n1, wa_ref[...])
    qa_ref[...] = (head_norm(za[:, :W_MIX], qna_ref[...]) * ATT_SCALE).astype(BF16)
    ka_ref[...] = head_norm(za[:, W_MIX:2 * W_MIX], kna_ref[...])
    va_ref[...] = za[:, 2 * W_MIX:]
    zb = _dot(n1, wb_ref[...])
    qb_ref[...] = (head_norm(zb[:, :W_MIX], qnb_ref[...]) * ATT_SCALE).astype(BF16)
    kb_ref[...] = head_norm(zb[:, W_MIX:2 * W_MIX], knb_ref[...])
    vb_ref[...] = zb[:, 2 * W_MIX:]
    fl = _dot(n1, wf_ref[...]) + bf_ref[...]
    ls = jnp.minimum(fl, 0.0) - jnp.log1p(jnp.exp(-jnp.abs(fl)))
    lf_ref[...] = ls[:, :N_HEADS]
    zg = _dot(n1, wg_ref[...])
    sg = 1.0 / (1.0 + jnp.exp(-zg))
    ga_ref[...] = sg[:, :D_MODEL].astype(BF16)
    gb_ref[...] = sg[:, D_MODEL:].astype(BF16)


def _proj(x, w, tb):
    n = x.shape[0]
    row = lambda width: pl.BlockSpec((tb, width), lambda i: (i, 0))
    ins = [x, w["g_mix"], w["w_a"], w["w_f"], w["w_b"], w["w_g"], w["b_f"], w["qn_a"], w["kn_a"], w["qn_b"],
           w["kn_b"], w["msum"]]
    in_specs = [row(D_MODEL)] + [_const_spec(a.shape) for a in ins[1:]]
    widths = [(W_MIX, BF16), (W_MIX, F32), (W_MIX, F32), (N_HEADS, F32), (W_MIX, BF16), (W_MIX, F32), (W_MIX, F32),
              (D_MODEL, BF16), (D_MODEL, BF16)]
    return pl.pallas_call(
        _proj_body,
        grid=(n // tb,),
        in_specs=in_specs,
        out_specs=[row(wd) for wd, _ in widths],
        out_shape=[jax.ShapeDtypeStruct((n, wd), dt) for wd, dt in widths],
        compiler_params=_params(("parallel",)),
        name="proj",
    )(*ins)


def _cumsum_lanes(x):
    n = x.shape[-1]
    lane = lax.broadcasted_iota(jnp.int32, x.shape, x.ndim - 1)
    s = 1
    while s < n:
        x = x + jnp.where(lane >= s, pltpu.roll(x, s, axis=x.ndim - 1), 0.0)
        s *= 2
    return x


def _head_of_pair(x, hh):
    lane = lax.broadcasted_iota(jnp.int32, (1, PAIR), 1)
    keep = (lane < HEAD_DIM) if hh == 0 else (lane >= HEAD_DIM)
    return jnp.where(keep, x, jnp.zeros_like(x))


def _merge_pair(o0, o1):
    lane = lax.broadcasted_iota(jnp.int32, (1, PAIR), 1)
    return jnp.where(lane < HEAD_DIM, o0, o1)


def _fox_body(q_ref, k_ref, v_ref, lft_ref, o_ref, kb_s, vb_s, c_s, *, tq, nq):
    hp = pl.program_id(1)
    qi = pl.program_id(2)

    @pl.when(qi == 0)
    def _():
        kb_s[...] = k_ref[0].astype(BF16)
        vb_s[...] = v_ref[0].astype(BF16)
        c_s[...] = _cumsum_lanes(lft_ref[0])

    q = q_ref[0]
    q2 = jnp.concatenate([_head_of_pair(q, 0), _head_of_pair(q, 1)], axis=0)
    row = lax.broadcasted_iota(jnp.int32, (tq, tq), 0)
    col = lax.broadcasted_iota(jnp.int32, (tq, tq), 1)

    def tile(n_blocks):
        past = (n_blocks - 1) * tq
        keys = n_blocks * tq
        s2 = _dot_t(q2, kb_s[:keys, :])
        probs, sums = [], []
        for hh in range(2):
            s = s2[hh * tq:(hh + 1) * tq] - c_s[pl.ds(2 * hp + hh, 1), :keys]
            s_diag = jnp.where(col <= row, s[:, past:], -jnp.inf)
            m = jnp.max(s_diag, axis=-1, keepdims=True)
            if past:
                m = jnp.maximum(m, jnp.max(s[:, :past], axis=-1, keepdims=True))
            p_diag = jnp.exp(s_diag - m)
            l = jnp.sum(p_diag, axis=-1, keepdims=True)
            if past:
                p_past = jnp.exp(s[:, :past] - m)
                l = l + jnp.sum(p_past, axis=-1, keepdims=True)
                p_diag = jnp.concatenate([p_past, p_diag], axis=1)
            probs.append(p_diag.astype(BF16))
            sums.append(l)
        o2 = _dot(jnp.concatenate(probs, axis=0), vb_s[:keys, :])
        o_ref[0] = _merge_pair(o2[:tq] / sums[0], o2[tq:] / sums[1]).astype(o_ref.dtype)

    for n_blocks in range(1, nq + 1):
        pl.when(qi == n_blocks - 1)(functools.partial(tile, n_blocks))


def _fox_prompt(q, k, v, lft, tq):
    b, t, _ = q.shape
    nq = t // tq
    return pl.pallas_call(
        functools.partial(_fox_body, tq=tq, nq=nq),
        grid=(b, N_PAIRS, nq),
        in_specs=[
            pl.BlockSpec((1, tq, PAIR), lambda i, p, j: (i, j, p)),
            pl.BlockSpec((1, t, PAIR), lambda i, p, j: (i, 0, p)),
            pl.BlockSpec((1, t, PAIR), lambda i, p, j: (i, 0, p)),
            pl.BlockSpec((1, N_HEADS, t), lambda i, p, j: (i, 0, 0)),
        ],
        out_specs=pl.BlockSpec((1, tq, PAIR), lambda i, p, j: (i, j, p)),
        out_shape=jax.ShapeDtypeStruct((b, t, W_MIX), BF16),
        scratch_shapes=[pltpu.VMEM((t, PAIR), BF16), pltpu.VMEM((t, PAIR), BF16), pltpu.VMEM((N_HEADS, t), F32)],
        compiler_params=_params(("parallel", "parallel", "arbitrary")),
        name="fox_prompt",
    )(q, k, v, lft)


def _band_body(q_ref, k_ref, v_ref, bias_ref, o_ref, kp_s, vp_s, *, tq):
    qi = pl.program_id(1)
    win = tq + BAND_PAST

    @pl.when(qi == 0)
    def _():
        zeros = jnp.zeros((BAND_PAST, W_MIX), BF16)
        kp_s[:BAND_PAST, :] = zeros
        vp_s[:BAND_PAST, :] = zeros
        kp_s[BAND_PAST:, :] = k_ref[0].astype(BF16)
        vp_s[BAND_PAST:, :] = v_ref[0].astype(BF16)

    off = pl.multiple_of(qi * tq, tq)
    exists = lax.broadcasted_iota(jnp.int32, (1, win), 1) >= BAND_PAST - qi * tq
    for pair in range(N_PAIRS):
        lanes = slice(pair * PAIR, (pair + 1) * PAIR)
        kw = kp_s[pl.ds(off, win), lanes]
        vw = vp_s[pl.ds(off, win), lanes]
        q = q_ref[0, :, lanes]
        outs = []
        for hh in range(2):
            s = _dot_t(_head_of_pair(q, hh), kw) + bias_ref[2 * pair + hh]
            s = jnp.where(exists, s, NEG)
            m = jnp.max(s, axis=-1, keepdims=True)
            p = jnp.exp(s - m)
            l = jnp.sum(p, axis=-1, keepdims=True)
            outs.append(_dot(p.astype(BF16), vw) / l)
        o_ref[0, :, lanes] = _merge_pair(outs[0], outs[1]).astype(o_ref.dtype)


def _band_prompt(q, k, v, bias, tq):
    b, t, _ = q.shape
    win = tq + BAND_PAST
    return pl.pallas_call(
        functools.partial(_band_body, tq=tq),
        grid=(b, t // tq),
        in_specs=[
            pl.BlockSpec((1, tq, W_MIX), lambda i, j: (i, j, 0)),
            pl.BlockSpec((1, t, W_MIX), lambda i, j: (i, 0, 0)),
            pl.BlockSpec((1, t, W_MIX), lambda i, j: (i, 0, 0)),
            _const_spec((N_HEADS, tq, win)),
        ],
        out_specs=pl.BlockSpec((1, tq, W_MIX), lambda i, j: (i, j, 0)),
        out_shape=jax.ShapeDtypeStruct((b, t, W_MIX), BF16),
        scratch_shapes=[pltpu.VMEM((t + BAND_PAST, W_MIX), BF16), pltpu.VMEM((t + BAND_PAST, W_MIX), BF16)],
        compiler_params=_params(("parallel", "arbitrary")),
        name="band_prompt",
    )(q, k, v, bias)


def _toeplitz(w, n, m):
    heads, span = w.shape
    hankel = jnp.tile(w, (1, n + 1))[:, :n * (span + 1)].reshape(heads, n, span + 1)[:, :, :m]
    return hankel[:, ::-1, :]


def _band_bias_prompt(rel_bias, tq):
    win = tq + BAND_PAST
    rel = jnp.arange(tq + win - 1) - (tq - 1) - BAND_PAST
    table = _toeplitz(rel_bias[:, jnp.clip(rel, -MAX_REL, MAX_REL) + MAX_REL].astype(F32), tq, win)
    ii = jnp.arange(tq)[:, None]
    jj = jnp.arange(win)[None, :]
    lo = (ii // CHUNK) * CHUNK
    in_band = (jj >= lo) & (jj < lo + BAND_PAST + CHUNK)
    return jnp.where(in_band[None], table, NEG)


def _fox_sample_body(q_ref, kc_ref, vc_ref, kn_ref, vn_ref, lft_ref, o_ref, *, past, n_new):
    hp = pl.program_id(1)
    c = _cumsum_lanes(lft_ref[0])
    kc = kc_ref[0].astype(BF16)
    vc = vc_ref[0].astype(BF16)
    kn = kn_ref[0].astype(BF16)
    vn = vn_ref[0].astype(BF16)
    q = q_ref[0]
    nq = q.shape[0]
    row = lax.broadcasted_iota(jnp.int32, (nq, LANES), 0)
    col = lax.broadcasted_iota(jnp.int32, (nq, LANES), 1)
    outs = []
    for hh in range(2):
        qh = _head_of_pair(q, hh)
        sel = lax.broadcasted_iota(jnp.int32, (N_HEADS, 1), 0) == 2 * hp + hh
        crow = jnp.sum(jnp.where(sel, c, 0.0), axis=0, keepdims=True)
        sc = _dot_t(qh, kc) - crow[:, :past]
        sn = _dot_t(qh, kn) - crow[:, past:]
        sn = jnp.where((col <= row) & (col < n_new), sn, -jnp.inf)
        m = jnp.maximum(jnp.max(sc, axis=-1, keepdims=True), jnp.max(sn, axis=-1, keepdims=True))
        pc = jnp.exp(sc - m)
        pn = jnp.exp(sn - m)
        l = jnp.sum(pc, axis=-1, keepdims=True) + jnp.sum(pn, axis=-1, keepdims=True)
        outs.append((_dot(pc.astype(BF16), vc) + _dot(pn.astype(BF16), vn)) / l)
    o_ref[0] = _merge_pair(outs[0], outs[1]).astype(o_ref.dtype)


def _fox_sample(q, kc, vc, kn, vn, lft, n_new):
    b, nq, _ = q.shape
    past = kc.shape[1]
    pair_spec = lambda rows: pl.BlockSpec((1, rows, PAIR), lambda i, p: (i, 0, p))
    return pl.pallas_call(
        functools.partial(_fox_sample_body, past=past, n_new=n_new),
        grid=(b, N_PAIRS),
        in_specs=[pair_spec(nq), pair_spec(past), pair_spec(past), pair_spec(LANES), pair_spec(LANES),
                  pl.BlockSpec((1, N_HEADS, past + LANES), lambda i, p: (i, 0, 0))],
        out_specs=pair_spec(nq),
        out_shape=jax.ShapeDtypeStruct((b, nq, W_MIX), BF16),
        compiler_params=_params(("parallel", "parallel")),
        name="fox_sample",
    )(q, kc, vc, kn, vn, lft)


def _band_sample_body(q_ref, kc_ref, vc_ref, kn_ref, vn_ref, bias_ref, o_ref, *, past):
    kc = kc_ref[0].astype(BF16)
    vc = vc_ref[0].astype(BF16)
    kn = kn_ref[0].astype(BF16)
    vn = vn_ref[0].astype(BF16)
    q = q_ref[0]
    outs = []
    for hh in range(2):
        qh = _head_of_pair(q, hh)
        bias = bias_ref[hh]
        sc = _dot_t(qh, kc) + bias[:, :past]
        sn = _dot_t(qh, kn) + bias[:, past:]
        m = jnp.maximum(jnp.max(sc, axis=-1, keepdims=True), jnp.max(sn, axis=-1, keepdims=True))
        pc = jnp.exp(sc - m)
        pn = jnp.exp(sn - m)
        l = jnp.sum(pc, axis=-1, keepdims=True) + jnp.sum(pn, axis=-1, keepdims=True)
        outs.append((_dot(pc.astype(BF16), vc) + _dot(pn.astype(BF16), vn)) / l)
    o_ref[0] = _merge_pair(outs[0], outs[1]).astype(o_ref.dtype)


def _band_sample(q, kc, vc, kn, vn, bias):
    b, nq, _ = q.shape
    past = kc.shape[1]
    pair_spec = lambda rows: pl.BlockSpec((1, rows, PAIR), lambda i, p: (i, 0, p))
    return pl.pallas_call(
        functools.partial(_band_sample_body, past=past),
        grid=(b, N_PAIRS),
        in_specs=[pair_spec(nq), pair_spec(past), pair_spec(past), pair_spec(LANES), pair_spec(LANES),
                  pl.BlockSpec((2, nq, past + LANES), lambda i, p: (p, 0, 0))],
        out_specs=pair_spec(nq),
        out_shape=jax.ShapeDtypeStruct((b, nq, W_MIX), BF16),
        compiler_params=_params(("parallel", "parallel")),
        name="band_sample",
    )(q, kc, vc, kn, vn, bias)


def _band_bias_sample(rel_bias, n_new, past):
    ii = jnp.arange(n_new)[:, None]
    jj = jnp.arange(past + LANES)[None, :]
    table = rel_bias[:, jnp.clip(jj - past - ii, -MAX_REL, MAX_REL) + MAX_REL].astype(F32)
    return jnp.where((jj < past + n_new)[None], table, NEG)


def _merge_body(x_ref, ya_ref, yb_ref, ga_ref, gb_ref, wua_ref, wub_ref, wo_ref, gffn_ref, h_ref, n2_ref):
    merged = (ga_ref[...].astype(F32) * _dot(ya_ref[...], wua_ref[...])
              + gb_ref[...].astype(F32) * _dot(yb_ref[...], wub_ref[...]))
    h = x_ref[...] + _dot(merged.astype(BF16), wo_ref[...])
    h_ref[...] = h
    n2_ref[...] = _rms(h, gffn_ref[...]).astype(BF16)


def _merge(x, ya, yb, ga, gb, w, tb):
    n = x.shape[0]
    row = lambda width: pl.BlockSpec((tb, width), lambda i: (i, 0))
    consts = [w["w_up_a"], w["w_up_b"], w["w_out"], w["g_ffn"]]
    return pl.pallas_call(
        _merge_body,
        grid=(n // tb,),
        in_specs=[row(D_MODEL), row(W_MIX), row(W_MIX), row(D_MODEL), row(D_MODEL)]
        + [_const_spec(a.shape) for a in consts],
        out_specs=[row(D_MODEL), row(D_MODEL)],
        out_shape=[jax.ShapeDtypeStruct((n, D_MODEL), F32), jax.ShapeDtypeStruct((n, D_MODEL), BF16)],
        compiler_params=_params(("parallel",)),
        name="merge",
    )(x, ya, yb, ga, gb, *consts)


def _top16(s, vals_ref):
    rank = jnp.full(s.shape, float(TOPK), F32)
    work = s
    for r in range(TOPK):
        m = jnp.max(work, axis=0, keepdims=True)
        hit = work == m
        rank = jnp.where(hit, float(r), rank)
        work = jnp.where(hit, -jnp.inf, work)
        vals_ref[r:r + 1, :] = m
    return rank


def _pair_bf16(x):
    bits = lax.bitcast_convert_type(x.astype(BF16).astype(F32), jnp.uint32)
    return bits | (bits >> 16)


def _retrieve_body(n2_ref, wq_ref, sk_ref, r1_ref, b_ref, a_ref, c_ref, va_s, vb_s):
    q = _dot(n2_ref[...], wq_ref[...]).astype(BF16)
    for h in range(PEER_HEADS):
        qh = q[:, h * PAIR:(h + 1) * PAIR]
        s0 = _dot_t(sk_ref[h, 0], qh)
        s1 = _dot_t(sk_ref[h, 1], qh)
        rank0 = _top16(s0, va_s)
        rank1 = _top16(s1, vb_s)
        va = va_s[...]
        vb = vb_s[...]
        cand = jnp.concatenate([va[0:1] + vb] + [va[k:k + 1] + vb[0:8] for k in range(1, 8)] + [va[8:16] + vb[0:1]],
                               axis=0)
        work = cand
        picked = jnp.zeros(cand.shape, jnp.bool_)
        for _ in range(TOPK):
            m = jnp.max(work, axis=0, keepdims=True)
            hit = work == m
            picked = picked | hit
            work = jnp.where(hit, -jnp.inf, work)
        top = va[0:1] + vb[0:1]
        z = jnp.sum(jnp.where(picked, jnp.exp(cand - top), 0.0), axis=0, keepdims=True)
        pickf = picked.astype(F32)
        counts = [jnp.sum(pickf[0:16], axis=0, keepdims=True)]
        counts += [jnp.sum(pickf[8 + 8 * k:16 + 8 * k], axis=0, keepdims=True) for k in range(1, 8)]
        counts += [pickf[72 + k:73 + k] for k in range(8)]
        c = jnp.zeros(s0.shape, F32)
        for k in range(TOPK):
            c = jnp.where(rank0 == float(k), counts[k], c)
        a = jnp.where(rank0 < float(TOPK), jnp.exp(s0 - va[0:1]) / z, 0.0)
        b = jnp.where(rank1 < float(TOPK), jnp.exp(s1 - vb[0:1]), 0.0)
        r1_ref[h] = rank1.astype(BF16)
        b_ref[h] = b.astype(BF16)
        a_ref[h] = _pair_bf16(a)
        c_ref[h] = _pair_bf16(c)


def _retrieve(n2, w, tb):
    n = n2.shape[0]
    maps = pl.BlockSpec((PEER_HEADS, N_KEYS, tb), lambda i: (0, 0, i))
    return pl.pallas_call(
        _retrieve_body,
        grid=(n // tb,),
        in_specs=[pl.BlockSpec((tb, D_MODEL), lambda i: (i, 0)), _const_spec(w["peer_wq"].shape),
                  _const_spec(w["peer_sk"].shape)],
        out_specs=[maps, maps, maps, maps],
        out_shape=[jax.ShapeDtypeStruct((PEER_HEADS, N_KEYS, n), BF16), jax.ShapeDtypeStruct((PEER_HEADS, N_KEYS, n), BF16),
                   jax.ShapeDtypeStruct((PEER_HEADS, N_KEYS, n), jnp.uint32),
                   jax.ShapeDtypeStruct((PEER_HEADS, N_KEYS, n), jnp.uint32)],
        scratch_shapes=[pltpu.VMEM((TOPK, tb), F32), pltpu.VMEM((TOPK, tb), F32)],
        compiler_params=_params(("parallel",)),
        name="peer_retrieve",
    )(n2, w["peer_wq"], w["peer_sk"])


def _gelu_tanh(x):
    return 0.5 * x * (1.0 + jnp.tanh(math.sqrt(2.0 / math.pi) * (x + 0.044715 * (x * x * x))))


def _row_tile(ref, h, g, ii, lanes):
    row = jnp.broadcast_to(ref[h, g, ii:ii + 1, lanes], (SUBLANES, LANES))
    packed = pltpu.bitcast(row, BF16)
    return jnp.concatenate([packed] * (N_KEYS // packed.shape[0]), axis=0)


def _dense_body(n2_ref, u_ref, vt_ref, r1_ref, b_ref, a_ref, c_ref, o_ref, acc_s, gate_s, *, tb):
    g = pl.program_id(1)

    @pl.when(g == 0)
    def _():
        acc_s[...] = jnp.zeros(acc_s.shape, F32)

    for tt in range(tb // LANES):
        lanes = slice(tt * LANES, (tt + 1) * LANES)
        for ii in range(SUBLANES):
            rows = slice(ii * N_KEYS, (ii + 1) * N_KEYS)
            gate = None
            for h in range(PEER_HEADS):
                keys = slice(h * N_KEYS, (h + 1) * N_KEYS)
                wgt = b_ref[keys, lanes] * _row_tile(a_ref, h, g, ii, lanes)
                term = jnp.where(r1_ref[keys, lanes] < _row_tile(c_ref, h, g, ii, lanes), wgt, jnp.zeros_like(wgt))
                gate = term if gate is None else gate + term
            gate_s[rows, lanes] = gate

    hid = _dot_t(u_ref[...], n2_ref[...])
    weighted = gate_s[...] * _gelu_tanh(hid).astype(BF16)
    acc_s[...] += _dot(vt_ref[...], weighted)

    @pl.when(g == pl.num_programs(1) - 1)
    def _():
        o_ref[...] = acc_s[...].T


def _dense(n2, u, vt, r1, b, a, c, tb):
    n = n2.shape[0]
    ec = SUBLANES * N_KEYS
    ng = N_EXPERTS // ec
    regroup = lambda m: m.reshape(PEER_HEADS, ng, SUBLANES, n)
    flat = lambda m: m.reshape(PEER_HEADS * N_KEYS, n)
    maps = pl.BlockSpec((PEER_HEADS * N_KEYS, tb), lambda i, g: (0, i))
    rows = pl.BlockSpec((PEER_HEADS, ng, SUBLANES, tb), lambda i, g: (0, 0, 0, i))
    return pl.pallas_call(
        functools.partial(_dense_body, tb=tb),
        grid=(n // tb, ng),
        in_specs=[pl.BlockSpec((tb, D_MODEL), lambda i, g: (i, 0)),
                  pl.BlockSpec((ec, D_MODEL), lambda i, g: (g, 0)),
                  pl.BlockSpec((D_MODEL, ec), lambda i, g: (0, g)),
                  maps, maps, rows, rows],
        out_specs=pl.BlockSpec((tb, D_MODEL), lambda i, g: (i, 0)),
        out_shape=jax.ShapeDtypeStruct((n, D_MODEL), F32),
        scratch_shapes=[pltpu.VMEM((D_MODEL, tb), F32), pltpu.VMEM((ec, tb), BF16)],
        compiler_params=_params(("parallel", "arbitrary")),
        name="peer_dense",
    )(n2, u, vt, flat(r1), flat(b), regroup(a), regroup(c))


def _final_body(h_ref, peer_ref, p_ref, gple_ref, wg_ref, wp_ref, o_ref):
    h = h_ref[...] + peer_ref[...]
    zg = _dot(_rms(h, gple_ref[...]).astype(BF16), wg_ref[...])
    gate = 1.0 / (1.0 + jnp.exp(-zg))
    o_ref[...] = h + gate * _dot(p_ref[...].astype(BF16), wp_ref[...])


def _final(h, peer, p, w, tb):
    n = h.shape[0]
    row = lambda width: pl.BlockSpec((tb, width), lambda i: (i, 0))
    consts = [w["g_ple"], w["w_ple_gate"], w["w_ple_proj"]]
    return pl.pallas_call(
        _final_body,
        grid=(n // tb,),
        in_specs=[row(D_MODEL), row(D_MODEL), row(PLE_DIM)] + [_const_spec(a.shape) for a in consts],
        out_specs=row(D_MODEL),
        out_shape=jax.ShapeDtypeStruct((n, D_MODEL), F32),
        compiler_params=_params(("parallel",)),
        name="final",
    )(h, peer, p, *consts)


def _layer_weights(l, g_mix, w_in, b_f, qn_a, kn_a, qn_b, kn_b, w_up_a, w_up_b, w_out, g_ffn, peer_wq, peer_subkeys,
                   peer_u, peer_v, g_ple, w_ple_gate, w_ple_proj):
    o_f = 3 * W_MIX
    o_b = o_f + N_HEADS
    o_g = o_b + 3 * W_MIX
    wi = w_in[l]
    tile_heads = lambda g: jnp.tile(g[l].astype(F32), N_HEADS)[None, :]
    head_of = jnp.arange(W_MIX) // HEAD_DIM
    sk = peer_subkeys[l].astype(BF16)
    zeros = jnp.zeros_like(sk[:, 0])
    sk_pad = jnp.stack([jnp.concatenate([sk[:, 0], zeros], axis=-1), jnp.concatenate([zeros, sk[:, 1]], axis=-1)],
                       axis=1)
    return {
        "g_mix": g_mix[l][None, :],
        "w_a": wi[:, :o_f].astype(BF16),
        "w_f": jnp.pad(wi[:, o_f:o_b], ((0, 0), (0, LANES - N_HEADS))).astype(BF16),
        "w_b": wi[:, o_b:o_g].astype(BF16),
        "w_g": wi[:, o_g:].astype(BF16),
        "b_f": jnp.pad(b_f[l], (0, LANES - N_HEADS))[None, :],
        "qn_a": tile_heads(qn_a), "kn_a": tile_heads(kn_a), "qn_b": tile_heads(qn_b), "kn_b": tile_heads(kn_b),
        "msum": jnp.where(head_of[:, None] == head_of[None, :], 1.0 / HEAD_DIM, 0.0).astype(BF16),
        "w_up_a": w_up_a[l].astype(BF16), "w_up_b": w_up_b[l].astype(BF16), "w_out": w_out[l].astype(BF16),
        "g_ffn": g_ffn[l][None, :],
        "peer_wq": peer_wq[l].astype(BF16),
        "peer_sk": sk_pad,
        "peer_u": peer_u[l].astype(BF16),
        "peer_vt": peer_v[l].astype(BF16).T,
        "g_ple": g_ple[l][None, :],
        "w_ple_gate": w_ple_gate[l].astype(BF16),
        "w_ple_proj": w_ple_proj[l].astype(BF16),
    }


def _channel(x, ya, yb, ga, gb, p, w, tb, tb_dense):
    h1, n2 = _merge(x, ya, yb, ga, gb, w, tb)
    r1, b, a, c = _retrieve(n2, w, tb)
    peer = _dense(n2, w["peer_u"], w["peer_vt"], r1, b, a, c, tb_dense)
    return _final(h1, peer, p, w, tb)


def _pad_rows(x, rows):
    return jnp.pad(x, ((0, 0), (0, rows - x.shape[1]), (0, 0)))


def kernel(x_prompt, x_sample, cache_a_k, cache_a_v, cache_a_logf, cache_b_k, cache_b_v, p_prompt, p_sample, g_mix, w_in, b_f, qn_a, kn_a, qn_b, kn_b, rel_bias_b, w_up_a, w_up_b, w_out, g_ffn, peer_wq, peer_subkeys, peer_u, peer_v, g_ple, w_ple_gate, w_ple_proj):
    depth = w_in.shape[0]
    bp, tp, _ = x_prompt.shape
    bs, ts, _ = x_sample.shape
    past = cache_a_k.shape[2]
    band_rows = min(BAND_PAST, tp)
    tq = 256
    hp = x_prompt.reshape(bp * tp, D_MODEL)
    hs = x_sample.reshape(bs * ts, D_MODEL)
    outs = [[] for _ in range(10)]
    for l in range(depth):
        w = _layer_weights(l, g_mix, w_in, b_f, qn_a, kn_a, qn_b, kn_b, w_up_a, w_up_b, w_out, g_ffn, peer_wq,
                           peer_subkeys, peer_u, peer_v, g_ple, w_ple_gate, w_ple_proj)
        qa, ka, va, lf, qb, kb, vb, ga, gb = _proj(hp, w, 256)
        as_seq = lambda z: z.reshape(bp, tp, z.shape[-1])
        ya = _fox_prompt(as_seq(qa), as_seq(ka), as_seq(va), as_seq(lf).transpose(0, 2, 1), tq)
        yb = _band_prompt(as_seq(qb), as_seq(kb), as_seq(vb), _band_bias_prompt(rel_bias_b[l], tq), tq)
        hp = _channel(hp, ya.reshape(-1, W_MIX), yb.reshape(-1, W_MIX), ga, gb, p_prompt[l].reshape(-1, PLE_DIM), w,
                      256, 512)
        heads = lambda z, b_, t_: z.reshape(b_, t_, N_HEADS, HEAD_DIM)
        outs[0].append(heads(ka, bp, tp)); outs[1].append(heads(va, bp, tp)); outs[2].append(as_seq(lf))
        outs[3].append(heads(kb, bp, tp)[:, -band_rows:]); outs[4].append(heads(vb, bp, tp)[:, -band_rows:])
        qa, ka, va, lf, qb, kb, vb, ga, gb = _proj(hs, w, bs * ts)
        as_seq = lambda z: z.reshape(bs, ts, z.shape[-1])
        lft = jnp.concatenate([cache_a_logf[l].astype(F32), as_seq(lf),
                               jnp.zeros((bs, LANES - ts, N_HEADS), F32)], axis=1).transpose(0, 2, 1)
        flat_cache = lambda z: z[l].reshape(bs, z.shape[2], W_MIX)
        ya = _fox_sample(as_seq(qa), flat_cache(cache_a_k), flat_cache(cache_a_v), _pad_rows(as_seq(ka), LANES),
                         _pad_rows(as_seq(va), LANES), lft, ts)
        yb = _band_sample(as_seq(qb), flat_cache(cache_b_k), flat_cache(cache_b_v), _pad_rows(as_seq(kb), LANES),
                          _pad_rows(as_seq(vb), LANES), _band_bias_sample(rel_bias_b[l], ts, cache_b_k.shape[2]))
        hs = _channel(hs, ya.reshape(-1, W_MIX), yb.reshape(-1, W_MIX), ga, gb, p_sample[l].reshape(-1, PLE_DIM), w,
                      bs * ts, bs * ts)
        outs[5].append(heads(ka, bs, ts)); outs[6].append(heads(va, bs, ts)); outs[7].append(as_seq(lf))
        outs[8].append(heads(kb, bs, ts)); outs[9].append(heads(vb, bs, ts))
    return (hp.reshape(bp, tp, D_MODEL), hs.reshape(bs, ts, D_MODEL)) + tuple(jnp.stack(o) for o in outs)
```

```python
import functools
import math

import jax
import jax.numpy as jnp
from jax import lax
from jax.experimental import pallas as pl
from jax.experimental.pallas import tpu as pltpu

F32 = jnp.float32
BF16 = jnp.bfloat16

D_MODEL = 1024
HEAD_DIM = 64
N_HEADS = 8
W_MIX = N_HEADS * HEAD_DIM
PAIR = 2 * HEAD_DIM
N_PAIRS = N_HEADS // 2
CHUNK = 64
BAND_PAST = 8 * CHUNK
MAX_REL = 128
PLE_DIM = 256
PEER_HEADS = 8
N_KEYS = 128
N_EXPERTS = N_KEYS * N_KEYS
TOPK = 16
RMS_EPS = 1e-6
ATT_SCALE = HEAD_DIM ** -0.5
NEG = -1e30
RANK_SENTINEL = 2.0 ** 100
LANES = 128
SUBLANES = 8
VMEM_LIMIT = 56 * 1024 * 1024

N_CAND = 16 + 7 * 8 + 8


def _params(sem, vmem=VMEM_LIMIT):
    return pltpu.CompilerParams(dimension_semantics=sem, vmem_limit_bytes=vmem)


def _rms(x, g):
    return x * lax.rsqrt(jnp.mean(x * x, axis=-1, keepdims=True) + RMS_EPS) * g


def _dot(a, b):
    return jnp.dot(a, b, preferred_element_type=F32)


def _dot_t(a, b):
    return lax.dot_general(a, b, (((1,), (1,)), ((), ())), preferred_element_type=F32)


def _const_spec(shape):
    nd = len(shape)
    return pl.BlockSpec(shape, lambda *_: (0,) * nd)


def _proj_body(x_ref, g_ref, wa_ref, wf_ref, wb_ref, wg_ref, bf_ref, qna_ref, kna_ref, qnb_ref, knb_ref, msum_ref,
               qa_ref, ka_ref, va_ref, lf_ref, qb_ref, kb_ref, vb_ref, ga_ref, gb_ref):
    n1 = _rms(x_ref[...], g_ref[...]).astype(BF16)

    def head_norm(z, gain):
        ms = _dot((z * z).astype(BF16), msum_ref[...])
        return z * lax.rsqrt(ms + RMS_EPS) * gain

    za = _dot(n1, wa_ref[...])
    qa_ref[...] = (head_norm(za[:, :W_MIX], qna_ref[...]) * ATT_SCALE).astype(BF16)
    ka_ref[...] = head_norm(za[:, W_MIX:2 * W_MIX], kna_ref[...])
    va_ref[...] = za[:, 2 * W_MIX:]
    zb = _dot(n1, wb_ref[...])
    qb_ref[...] = (head_norm(zb[:, :W_MIX], qnb_ref[...]) * ATT_SCALE).astype(BF16)
    kb_ref[...] = head_norm(zb[:, W_MIX:2 * W_MIX], knb_ref[...])
    vb_ref[...] = zb[:, 2 * W_MIX:]
    fl = _dot(n1, wf_ref[...]) + bf_ref[...]
    ls = jnp.minimum(fl, 0.0) - jnp.log1p(jnp.exp(-jnp.abs(fl)))
    lf_ref[...] = ls[:, :N_HEADS]
    zg = _dot(n1, wg_ref[...])
    sg = 1.0 / (1.0 + jnp.exp(-zg))
    ga_ref[...] = sg[:, :D_MODEL].astype(BF16)
    gb_ref[...] = sg[:, D_MODEL:].astype(BF16)


def _proj(x, w, tb):
    n = x.shape[0]
    row = lambda width: pl.BlockSpec((tb, width), lambda i: (i, 0))
    ins = [x, w["g_mix"], w["w_a"], w["w_f"], w["w_b"], w["w_g"], w["b_f"], w["qn_a"], w["kn_a"], w["qn_b"],
           w["kn_b"], w["msum"]]
    in_specs = [row(D_MODEL)] + [_const_spec(a.shape) for a in ins[1:]]
    widths = [(W_MIX, BF16), (W_MIX, F32), (W_MIX, F32), (N_HEADS, F32), (W_MIX, BF16), (W_MIX, F32), (W_MIX, F32),
              (D_MODEL, BF16), (D_MODEL, BF16)]
    return pl.pallas_call(
        _proj_body,
        grid=(n // tb,),
        in_specs=in_specs,
        out_specs=[row(wd) for wd, _ in widths],
        out_shape=[jax.ShapeDtypeStruct((n, wd), dt) for wd, dt in widths],
        compiler_params=_params(("parallel",)),
        name="proj",
    )(*ins)


def _cumsum_lanes(x):
    n = x.shape[-1]
    lane = lax.broadcasted_iota(jnp.int32, x.shape, x.ndim - 1)
    s = 1
    while s < n:
        x = x + jnp.where(lane >= s, pltpu.roll(x, s, axis=x.ndim - 1), 0.0)
        s *= 2
    return x


def _head_of_pair(x, hh):
    lane = lax.broadcasted_iota(jnp.int32, (1, PAIR), 1)
    keep = (lane < HEAD_DIM) if hh == 0 else (lane >= HEAD_DIM)
    return jnp.where(keep, x, jnp.zeros_like(x))


def _merge_pair(o0, o1):
    lane = lax.broadcasted_iota(jnp.int32, (1, PAIR), 1)
    return jnp.where(lane < HEAD_DIM, o0, o1)


def _fox_body(q_ref, k_ref, v_ref, lft_ref, o_ref, kb_s, vb_s, c_s, *, tq, nq):
    hp = pl.program_id(1)
    qi = pl.program_id(2)

    @pl.when(qi == 0)
    def _():
        kb_s[...] = k_ref[0].astype(BF16)
        vb_s[...] = v_ref[0].astype(BF16)
        c_s[...] = _cumsum_lanes(lft_ref[0])

    q = q_ref[0]
    q2 = jnp.concatenate([_head_of_pair(q, 0), _head_of_pair(q, 1)], axis=0)
    row = lax.broadcasted_iota(jnp.int32, (tq, tq), 0)
    col = lax.broadcasted_iota(jnp.int32, (tq, tq), 1)

    def tile(n_blocks):
        past = (n_blocks - 1) * tq
        keys = n_blocks * tq
        s2 = _dot_t(q2, kb_s[:keys, :])
        probs, sums = [], []
        for hh in range(2):
            s = s2[hh * tq:(hh + 1) * tq] - c_s[pl.ds(2 * hp + hh, 1), :keys]
            s_diag = jnp.where(col <= row, s[:, past:], -jnp.inf)
            m = jnp.max(s_diag, axis=-1, keepdims=True)
            if past:
                m = jnp.maximum(m, jnp.max(s[:, :past], axis=-1, keepdims=True))
            p_diag = jnp.exp(s_diag - m)
            l = jnp.sum(p_diag, axis=-1, keepdims=True)
            if past:
                p_past = jnp.exp(s[:, :past] - m)
                l = l + jnp.sum(p_past, axis=-1, keepdims=True)
                p_diag = jnp.concatenate([p_past, p_diag], axis=1)
            probs.append(p_diag.astype(BF16))
            sums.append(l)
        o2 = _dot(jnp.concatenate(probs, axis=0), vb_s[:keys, :])
        o_ref[0] = _merge_pair(o2[:tq] / sums[0], o2[tq:] / sums[1]).astype(o_ref.dtype)

    for n_blocks in range(1, nq + 1):
        pl.when(qi == n_blocks - 1)(functools.partial(tile, n_blocks))


def _fox_prompt(q, k, v, lft, tq):
    b, t, _ = q.shape
    nq = t // tq
    return pl.pallas_call(
        functools.partial(_fox_body, tq=tq, nq=nq),
        grid=(b, N_PAIRS, nq),
        in_specs=[
            pl.BlockSpec((1, tq, PAIR), lambda i, p, j: (i, j, p)),
            pl.BlockSpec((1, t, PAIR), lambda i, p, j: (i, 0, p)),
            pl.BlockSpec((1, t, PAIR), lambda i, p, j: (i, 0, p)),
            pl.BlockSpec((1, N_HEADS, t), lambda i, p, j: (i, 0, 0)),
        ],
        out_specs=pl.BlockSpec((1, tq, PAIR), lambda i, p, j: (i, j, p)),
        out_shape=jax.ShapeDtypeStruct((b, t, W_MIX), BF16),
        scratch_shapes=[pltpu.VMEM((t, PAIR), BF16), pltpu.VMEM((t, PAIR), BF16), pltpu.VMEM((N_HEADS, t), F32)],
        compiler_params=_params(("parallel", "parallel", "arbitrary")),
        name="fox_prompt",
    )(q, k, v, lft)


def _band_body(q_ref, k_ref, v_ref, bias_ref, o_ref, kp_s, vp_s, *, tq):
    qi = pl.program_id(1)
    win = tq + BAND_PAST

    @pl.when(qi == 0)
    def _():
        zeros = jnp.zeros((BAND_PAST, W_MIX), BF16)
        kp_s[:BAND_PAST, :] = zeros
        vp_s[:BAND_PAST, :] = zeros
        kp_s[BAND_PAST:, :] = k_ref[0].astype(BF16)
        vp_s[BAND_PAST:, :] = v_ref[0].astype(BF16)

    off = pl.multiple_of(qi * tq, tq)
    exists = lax.broadcasted_iota(jnp.int32, (1, win), 1) >= BAND_PAST - qi * tq
    for pair in range(N_PAIRS):
        lanes = slice(pair * PAIR, (pair + 1) * PAIR)
        kw = kp_s[pl.ds(off, win), lanes]
        vw = vp_s[pl.ds(off, win), lanes]
        q = q_ref[0, :, lanes]
        outs = []
        for hh in range(2):
            s = _dot_t(_head_of_pair(q, hh), kw) + bias_ref[2 * pair + hh]
            s = jnp.where(exists, s, NEG)
            m = jnp.max(s, axis=-1, keepdims=True)
            p = jnp.exp(s - m)
            l = jnp.sum(p, axis=-1, keepdims=True)
            outs.append(_dot(p.astype(BF16), vw) / l)
        o_ref[0, :, lanes] = _merge_pair(outs[0], outs[1]).astype(o_ref.dtype)


def _band_prompt(q, k, v, bias, tq):
    b, t, _ = q.shape
    win = tq + BAND_PAST
    return pl.pallas_call(
        functools.partial(_band_body, tq=tq),
        grid=(b, t // tq),
        in_specs=[
            pl.BlockSpec((1, tq, W_MIX), lambda i, j: (i, j, 0)),
            pl.BlockSpec((1, t, W_MIX), lambda i, j: (i, 0, 0)),
            pl.BlockSpec((1, t, W_MIX), lambda i, j: (i, 0, 0)),
            _const_spec((N_HEADS, tq, win)),
        ],
        out_specs=pl.BlockSpec((1, tq, W_MIX), lambda i, j: (i, j, 0)),
        out_shape=jax.ShapeDtypeStruct((b, t, W_MIX), BF16),
        scratch_shapes=[pltpu.VMEM((t + BAND_PAST, W_MIX), BF16), pltpu.VMEM((t + BAND_PAST, W_MIX), BF16)],
        compiler_params=_params(("parallel", "arbitrary")),
        name="band_prompt",
    )(q, k, v, bias)


def _toeplitz(w, n, m):
    heads, span = w.shape
    hankel = jnp.tile(w, (1, n + 1))[:, :n * (span + 1)].reshape(heads, n, span + 1)[:, :, :m]
    return hankel[:, ::-1, :]


def _band_bias_prompt(rel_bias, tq):
    win = tq + BAND_PAST
    rel = jnp.arange(tq + win - 1) - (tq - 1) - BAND_PAST
    table = _toeplitz(rel_bias[:, jnp.clip(rel, -MAX_REL, MAX_REL) + MAX_REL].astype(F32), tq, win)
    ii = jnp.arange(tq)[:, None]
    jj = jnp.arange(win)[None, :]
    lo = (ii // CHUNK) * CHUNK
    in_band = (jj >= lo) & (jj < lo + BAND_PAST + CHUNK)
    return jnp.where(in_band[None], table, NEG)


def _fox_sample_body(q_ref, kc_ref, vc_ref, kn_ref, vn_ref, lft_ref, o_ref, *, past, n_new):
    hp = pl.program_id(1)
    c = _cumsum_lanes(lft_ref[0])
    kc = kc_ref[0].astype(BF16)
    vc = vc_ref[0].astype(BF16)
    kn = kn_ref[0].astype(BF16)
    vn = vn_ref[0].astype(BF16)
    q = q_ref[0]
    nq = q.shape[0]
    row = lax.broadcasted_iota(jnp.int32, (nq, LANES), 0)
    col = lax.broadcasted_iota(jnp.int32, (nq, LANES), 1)
    outs = []
    for hh in range(2):
        qh = _head_of_pair(q, hh)
        sel = lax.broadcasted_iota(jnp.int32, (N_HEADS, 1), 0) == 2 * hp + hh
        crow = jnp.sum(jnp.where(sel, c, 0.0), axis=0, keepdims=True)
        sc = _dot_t(qh, kc) - crow[:, :past]
        sn = _dot_t(qh, kn) - crow[:, past:]
        sn = jnp.where((col <= row) & (col < n_new), sn, -jnp.inf)
        m = jnp.maximum(jnp.max(sc, axis=-1, keepdims=True), jnp.max(sn, axis=-1, keepdims=True))
        pc = jnp.exp(sc - m)
        pn = jnp.exp(sn - m)
        l = jnp.sum(pc, axis=-1, keepdims=True) + jnp.sum(pn, axis=-1, keepdims=True)
        outs.append((_dot(pc.astype(BF16), vc) + _dot(pn.astype(BF16), vn)) / l)
    o_ref[0] = _merge_pair(outs[0], outs[1]).astype(o_ref.dtype)


def _fox_sample(q, kc, vc, kn, vn, lft, n_new):
    b, nq, _ = q.shape
    past = kc.shape[1]
    pair_spec = lambda rows: pl.BlockSpec((1, rows, PAIR), lambda i, p: (i, 0, p))
    return pl.pallas_call(
        functools.partial(_fox_sample_body, past=past, n_new=n_new),
        grid=(b, N_PAIRS),
        in_specs=[pair_spec(nq), pair_spec(past), pair_spec(past), pair_spec(LANES), pair_spec(LANES),
                  pl.BlockSpec((1, N_HEADS, past + LANES), lambda i, p: (i, 0, 0))],
        out_specs=pair_spec(nq),
        out_shape=jax.ShapeDtypeStruct((b, nq, W_MIX), BF16),
        compiler_params=_params(("parallel", "parallel")),
        name="fox_sample",
    )(q, kc, vc, kn, vn, lft)


def _band_sample_body(q_ref, kc_ref, vc_ref, kn_ref, vn_ref, bias_ref, o_ref, *, past):
    kc = kc_ref[0].astype(BF16)
    vc = vc_ref[0].astype(BF16)
    kn = kn_ref[0].astype(BF16)
    vn = vn_ref[0].astype(BF16)
    q = q_ref[0]
    outs = []
    for hh in range(2):
        qh = _head_of_pair(q, hh)
        bias = bias_ref[hh]
        sc = _dot_t(qh, kc) + bias[:, :past]
        sn = _dot_t(qh, kn) + bias[:, past:]
        m = jnp.maximum(jnp.max(sc, axis=-1, keepdims=True), jnp.max(sn, axis=-1, keepdims=True))
        pc = jnp.exp(sc - m)
        pn = jnp.exp(sn - m)
        l = jnp.sum(pc, axis=-1, keepdims=True) + jnp.sum(pn, axis=-1, keepdims=True)
        outs.append((_dot(pc.astype(BF16), vc) + _dot(pn.astype(BF16), vn)) / l)
    o_ref[0] = _merge_pair(outs[0], outs[1]).astype(o_ref.dtype)


def _band_sample(q, kc, vc, kn, vn, bias):
    b, nq, _ = q.shape
    past = kc.shape[1]
    pair_spec = lambda rows: pl.BlockSpec((1, rows, PAIR), lambda i, p: (i, 0, p))
    return pl.pallas_call(
        functools.partial(_band_sample_body, past=past),
        grid=(b, N_PAIRS),
        in_specs=[pair_spec(nq), pair_spec(past), pair_spec(past), pair_spec(LANES), pair_spec(LANES),
                  pl.BlockSpec((2, nq, past + LANES), lambda i, p: (p, 0, 0))],
        out_specs=pair_spec(nq),
        out_shape=jax.ShapeDtypeStruct((b, nq, W_MIX), BF16),
        compiler_params=_params(("parallel", "parallel")),
        name="band_sample",
    )(q, kc, vc, kn, vn, bias)


def _band_bias_sample(rel_bias, n_new, past):
    ii = jnp.arange(n_new)[:, None]
    jj = jnp.arange(past + LANES)[None, :]
    table = rel_bias[:, jnp.clip(jj - past - ii, -MAX_REL, MAX_REL) + MAX_REL].astype(F32)
    return jnp.where((jj < past + n_new)[None], table, NEG)


def _merge_body(x_ref, ya_ref, yb_ref, ga_ref, gb_ref, wua_ref, wub_ref, wo_ref, gffn_ref, h_ref, n2_ref):
    merged = (ga_ref[...].astype(F32) * _dot(ya_ref[...], wua_ref[...])
              + gb_ref[...].astype(F32) * _dot(yb_ref[...], wub_ref[...]))
    h = x_ref[...] + _dot(merged.astype(BF16), wo_ref[...])
    h_ref[...] = h
    n2_ref[...] = _rms(h, gffn_ref[...]).astype(BF16)


def _merge(x, ya, yb, ga, gb, w, tb):
    n = x.shape[0]
    row = lambda width: pl.BlockSpec((tb, width), lambda i: (i, 0))
    consts = [w["w_up_a"], w["w_up_b"], w["w_out"], w["g_ffn"]]
    return pl.pallas_call(
        _merge_body,
        grid=(n // tb,),
        in_specs=[row(D_MODEL), row(W_MIX), row(W_MIX), row(D_MODEL), row(D_MODEL)]
        + [_const_spec(a.shape) for a in consts],
        out_specs=[row(D_MODEL), row(D_MODEL)],
        out_shape=[jax.ShapeDtypeStruct((n, D_MODEL), F32), jax.ShapeDtypeStruct((n, D_MODEL), BF16)],
        compiler_params=_params(("parallel",)),
        name="merge",
    )(x, ya, yb, ga, gb, *consts)


def _leave(work, exact):
    hit = work == jnp.max(work, axis=0, keepdims=True)
    if exact:
        idx = lax.broadcasted_iota(jnp.int32, work.shape, 0)
        hit = idx == jnp.min(jnp.where(hit, idx, work.shape[0]), axis=0, keepdims=True)
    return hit


def _top16(s, vals_ref, exact):
    work = s
    for r in range(TOPK):
        vals_ref[r:r + 1, :] = jnp.max(work, axis=0, keepdims=True)
        work = jnp.where(_leave(work, exact), -RANK_SENTINEL * (1.0 + r / 32.0), work)
    return jnp.where(work <= -RANK_SENTINEL, work * (-32.0 / RANK_SENTINEL) - 32.0, float(TOPK))


def _count(mask):
    return jnp.sum(mask.astype(F32), axis=0, keepdims=True)


def _pair_bf16(x):
    bits = lax.bitcast_convert_type(x.astype(BF16).astype(F32), jnp.uint32)
    return bits | (bits >> 16)


def _retrieve_tile(s_s, tt, r1_ref, b_ref, a_ref, c_ref, va_s, vb_s, exact):
    off = jnp.zeros((1, LANES), F32)
    for h in range(PEER_HEADS):
        s0 = s_s[2 * h, tt]
        s1 = s_s[2 * h + 1, tt]
        rank0 = _top16(s0, va_s, exact)
        rank1 = _top16(s1, vb_s, exact)
        va = va_s[...]
        vb = vb_s[...]
        cand = jnp.concatenate([va[0:1] + vb] + [va[k:k + 1] + vb[0:8] for k in range(1, 8)] + [va[8:16] + vb[0:1]],
                               axis=0)
        work = cand
        picked = jnp.zeros(cand.shape, jnp.bool_)
        for _ in range(TOPK):
            hit = _leave(work, exact)
            picked = picked | hit
            work = jnp.where(hit, -jnp.inf, work)
        top = va[0:1] + vb[0:1]
        z = jnp.sum(jnp.where(picked, jnp.exp(cand - top), 0.0), axis=0, keepdims=True)
        pickf = picked.astype(F32)
        counts = [jnp.sum(pickf[0:16], axis=0, keepdims=True)]
        counts += [jnp.sum(pickf[8 + 8 * k:16 + 8 * k], axis=0, keepdims=True) for k in range(1, 8)]
        counts += [pickf[72 + k:73 + k] for k in range(8)]
        c = jnp.zeros(s0.shape, F32)
        for k in range(TOPK):
            c = jnp.where(rank0 == float(k), counts[k], c)
        a = jnp.where(rank0 < float(TOPK), jnp.exp(s0 - va[0:1]) * (1.0 / z), 0.0)
        b = jnp.where(rank1 < float(TOPK), jnp.exp(s1 - vb[0:1]), 0.0)
        r1_ref[tt, h] = rank1.astype(BF16)
        b_ref[tt, h] = b.astype(BF16)
        a_rows = _pair_bf16(a)
        c_rows = _pair_bf16(c)
        for grp in range(N_KEYS // SUBLANES):
            rows = slice(grp * SUBLANES, (grp + 1) * SUBLANES)
            a_ref[tt, h, grp] = a_rows[rows]
            c_ref[tt, h, grp] = c_rows[rows]
        if not exact:
            for n_left in (_count(rank0 < float(TOPK)), _count(rank1 < float(TOPK)), _count(picked)):
                off = off + jnp.abs(n_left - float(TOPK))
    return off


def _retrieve_body(n2_ref, wq_ref, sk_ref, r1_ref, b_ref, a_ref, c_ref, s_s, va_s, vb_s, *, tb):
    q = _dot(n2_ref[...], wq_ref[...]).astype(BF16)
    for h in range(PEER_HEADS):
        qh = q[:, h * PAIR:(h + 1) * PAIR]
        for half in range(2):
            s = _dot_t(sk_ref[h, half], qh)
            for tt in range(tb // LANES):
                s_s[2 * h + half, tt] = s[:, tt * LANES:(tt + 1) * LANES]

    def tile(tt, carry):
        maps = (r1_ref, b_ref, a_ref, c_ref, va_s, vb_s)
        off = _retrieve_tile(s_s, tt, *maps, exact=False)

        @pl.when(jnp.max(off) > 0.0)
        def _():
            _retrieve_tile(s_s, tt, *maps, exact=True)

        return carry

    lax.fori_loop(0, tb // LANES, tile, 0)


def _retrieve(n2, w, tb):
    n = n2.shape[0]
    nt = tb // LANES
    maps = pl.BlockSpec((nt, PEER_HEADS, N_KEYS, LANES), lambda i: (i, 0, 0, 0))
    rows = pl.BlockSpec((nt, PEER_HEADS, N_KEYS // SUBLANES, SUBLANES, LANES), lambda i: (i, 0, 0, 0, 0))
    map_shape = (n // LANES, PEER_HEADS, N_KEYS, LANES)
    row_shape = (n // LANES, PEER_HEADS, N_KEYS // SUBLANES, SUBLANES, LANES)
    return pl.pallas_call(
        functools.partial(_retrieve_body, tb=tb),
        grid=(n // tb,),
        in_specs=[pl.BlockSpec((tb, D_MODEL), lambda i: (i, 0)), _const_spec(w["peer_wq"].shape),
                  _const_spec(w["peer_sk"].shape)],
        out_specs=[maps, maps, rows, rows],
        out_shape=[jax.ShapeDtypeStruct(map_shape, BF16), jax.ShapeDtypeStruct(map_shape, BF16),
                   jax.ShapeDtypeStruct(row_shape, jnp.uint32), jax.ShapeDtypeStruct(row_shape, jnp.uint32)],
        scratch_shapes=[pltpu.VMEM((2 * PEER_HEADS, nt, N_KEYS, LANES), F32), pltpu.VMEM((TOPK, LANES), F32),
                        pltpu.VMEM((TOPK, LANES), F32)],
        compiler_params=_params(("parallel",)),
        name="peer_retrieve",
    )(n2, w["peer_wq"], w["peer_sk"])


def _gelu_tanh(x):
    return 0.5 * x * (1.0 + jnp.tanh(math.sqrt(2.0 / math.pi) * (x + 0.044715 * (x * x * x))))


def _row_tile(ref, tt, h, g, ii):
    row = jnp.broadcast_to(ref[tt, h, g, ii:ii + 1, :], (SUBLANES, LANES))
    packed = pltpu.bitcast(row, BF16)
    return jnp.concatenate([packed] * (N_KEYS // packed.shape[0]), axis=0)


def _dense_body(n2_ref, u_ref, vt_ref, r1_ref, b_ref, a_ref, c_ref, o_ref, acc_s, gate_s, r1_s, b_s, *, tb):
    g = pl.program_id(1)

    @pl.when(g == 0)
    def _():
        acc_s[...] = jnp.zeros(acc_s.shape, F32)
        r1_s[...] = r1_ref[...]
        b_s[...] = b_ref[...]

    for tt in range(tb // LANES):
        lanes = slice(tt * LANES, (tt + 1) * LANES)
        for ii in range(SUBLANES):
            rows = slice(ii * N_KEYS, (ii + 1) * N_KEYS)
            gate = None
            for h in range(PEER_HEADS):
                wgt = b_s[tt, h] * _row_tile(a_ref, tt, h, g, ii)
                term = jnp.where(r1_s[tt, h] < _row_tile(c_ref, tt, h, g, ii), wgt, jnp.zeros_like(wgt))
                gate = term if gate is None else gate + term
            gate_s[rows, lanes] = gate

    hid = _dot_t(u_ref[...], n2_ref[...])
    weighted = gate_s[...] * _gelu_tanh(hid).astype(BF16)
    acc_s[...] += _dot(vt_ref[...], weighted)

    @pl.when(g == pl.num_programs(1) - 1)
    def _():
        o_ref[...] = acc_s[...].T


def _dense(n2, u, vt, r1, b, a, c, tb):
    n = n2.shape[0]
    ec = SUBLANES * N_KEYS
    ng = N_EXPERTS // ec
    nt = tb // LANES
    maps = pl.BlockSpec((nt, PEER_HEADS, N_KEYS, LANES), lambda i, g: (i, 0, 0, 0))
    rows = pl.BlockSpec((nt, PEER_HEADS, ng, SUBLANES, LANES), lambda i, g: (i, 0, 0, 0, 0))
    return pl.pallas_call(
        functools.partial(_dense_body, tb=tb),
        grid=(n // tb, ng),
        in_specs=[pl.BlockSpec((tb, D_MODEL), lambda i, g: (i, 0)),
                  pl.BlockSpec((ec, D_MODEL), lambda i, g: (g, 0)),
                  pl.BlockSpec((D_MODEL, ec), lambda i, g: (0, g)),
                  maps, maps, rows, rows],
        out_specs=pl.BlockSpec((tb, D_MODEL), lambda i, g: (i, 0)),
        out_shape=jax.ShapeDtypeStruct((n, D_MODEL), F32),
        scratch_shapes=[pltpu.VMEM((D_MODEL, tb), F32), pltpu.VMEM((ec, tb), BF16),
                        pltpu.VMEM((nt, PEER_HEADS, N_KEYS, LANES), BF16),
                        pltpu.VMEM((nt, PEER_HEADS, N_KEYS, LANES), BF16)],
        compiler_params=_params(("parallel", "arbitrary")),
        name="peer_dense",
    )(n2, u, vt, r1, b, a, c)


def _final_body(h_ref, peer_ref, p_ref, gple_ref, wg_ref, wp_ref, o_ref):
    h = h_ref[...] + peer_ref[...]
    zg = _dot(_rms(h, gple_ref[...]).astype(BF16), wg_ref[...])
    gate = 1.0 / (1.0 + jnp.exp(-zg))
    o_ref[...] = h + gate * _dot(p_ref[...].astype(BF16), wp_ref[...])


def _final(h, peer, p, w, tb):
    n = h.shape[0]
    row = lambda width: pl.BlockSpec((tb, width), lambda i: (i, 0))
    consts = [w["g_ple"], w["w_ple_gate"], w["w_ple_proj"]]
    return pl.pallas_call(
        _final_body,
        grid=(n // tb,),
        in_specs=[row(D_MODEL), row(D_MODEL), row(PLE_DIM)] + [_const_spec(a.shape) for a in consts],
        out_specs=row(D_MODEL),
        out_shape=jax.ShapeDtypeStruct((n, D_MODEL), F32),
        compiler_params=_params(("parallel",)),
        name="final",
    )(h, peer, p, *consts)


def _layer_weights(l, g_mix, w_in, b_f, qn_a, kn_a, qn_b, kn_b, w_up_a, w_up_b, w_out, g_ffn, peer_wq, peer_subkeys,
                   peer_u, peer_v, g_ple, w_ple_gate, w_ple_proj):
    o_f = 3 * W_MIX
    o_b = o_f + N_HEADS
    o_g = o_b + 3 * W_MIX
    wi = w_in[l]
    tile_heads = lambda g: jnp.tile(g[l].astype(F32), N_HEADS)[None, :]
    head_of = jnp.arange(W_MIX) // HEAD_DIM
    sk = peer_subkeys[l].astype(BF16)
    zeros = jnp.zeros_like(sk[:, 0])
    sk_pad = jnp.stack([jnp.concatenate([sk[:, 0], zeros], axis=-1), jnp.concatenate([zeros, sk[:, 1]], axis=-1)],
                       axis=1)
    return {
        "g_mix": g_mix[l][None, :],
        "w_a": wi[:, :o_f].astype(BF16),
        "w_f": jnp.pad(wi[:, o_f:o_b], ((0, 0), (0, LANES - N_HEADS))).astype(BF16),
        "w_b": wi[:, o_b:o_g].astype(BF16),
        "w_g": wi[:, o_g:].astype(BF16),
        "b_f": jnp.pad(b_f[l], (0, LANES - N_HEADS))[None, :],
        "qn_a": tile_heads(qn_a), "kn_a": tile_heads(kn_a), "qn_b": tile_heads(qn_b), "kn_b": tile_heads(kn_b),
        "msum": jnp.where(head_of[:, None] == head_of[None, :], 1.0 / HEAD_DIM, 0.0).astype(BF16),
        "w_up_a": w_up_a[l].astype(BF16), "w_up_b": w_up_b[l].astype(BF16), "w_out": w_out[l].astype(BF16),
        "g_ffn": g_ffn[l][None, :],
        "peer_wq": peer_wq[l].astype(BF16),
        "peer_sk": sk_pad,
        "peer_u": peer_u[l].astype(BF16),
        "peer_vt": peer_v[l].astype(BF16).T,
        "g_ple": g_ple[l][None, :],
        "w_ple_gate": w_ple_gate[l].astype(BF16),
        "w_ple_proj": w_ple_proj[l].astype(BF16),
    }


def _channel(x, ya, yb, ga, gb, p, w, tb, tb_dense):
    h1, n2 = _merge(x, ya, yb, ga, gb, w, tb)
    r1, b, a, c = _retrieve(n2, w, tb_dense)
    peer = _dense(n2, w["peer_u"], w["peer_vt"], r1, b, a, c, tb_dense)
    return _final(h1, peer, p, w, tb)


def _pad_rows(x, rows):
    return jnp.pad(x, ((0, 0), (0, rows - x.shape[1]), (0, 0)))


def kernel(x_prompt, x_sample, cache_a_k, cache_a_v, cache_a_logf, cache_b_k, cache_b_v, p_prompt, p_sample, g_mix, w_in, b_f, qn_a, kn_a, qn_b, kn_b, rel_bias_b, w_up_a, w_up_b, w_out, g_ffn, peer_wq, peer_subkeys, peer_u, peer_v, g_ple, w_ple_gate, w_ple_proj):
    depth = w_in.shape[0]
    bp, tp, _ = x_prompt.shape
    bs, ts, _ = x_sample.shape
    past = cache_a_k.shape[2]
    band_rows = min(BAND_PAST, tp)
    tq = 256
    hp = x_prompt.reshape(bp * tp, D_MODEL)
    hs = x_sample.reshape(bs * ts, D_MODEL)
    outs = [[] for _ in range(10)]
    for l in range(depth):
        w = _layer_weights(l, g_mix, w_in, b_f, qn_a, kn_a, qn_b, kn_b, w_up_a, w_up_b, w_out, g_ffn, peer_wq,
                           peer_subkeys, peer_u, peer_v, g_ple, w_ple_gate, w_ple_proj)
        qa, ka, va, lf, qb, kb, vb, ga, gb = _proj(hp, w, 256)
        as_seq = lambda z: z.reshape(bp, tp, z.shape[-1])
        ya = _fox_prompt(as_seq(qa), as_seq(ka), as_seq(va), as_seq(lf).transpose(0, 2, 1), tq)
        yb = _band_prompt(as_seq(qb), as_seq(kb), as_seq(vb), _band_bias_prompt(rel_bias_b[l], tq), tq)
        hp = _channel(hp, ya.reshape(-1, W_MIX), yb.reshape(-1, W_MIX), ga, gb, p_prompt[l].reshape(-1, PLE_DIM), w,
                      256, 512)
        heads = lambda z, b_, t_: z.reshape(b_, t_, N_HEADS, HEAD_DIM)
        outs[0].append(heads(ka, bp, tp)); outs[1].append(heads(va, bp, tp)); outs[2].append(as_seq(lf))
        outs[3].append(heads(kb, bp, tp)[:, -band_rows:]); outs[4].append(heads(vb, bp, tp)[:, -band_rows:])
        qa, ka, va, lf, qb, kb, vb, ga, gb = _proj(hs, w, bs * ts)
        as_seq = lambda z: z.reshape(bs, ts, z.shape[-1])
        lft = jnp.concatenate([cache_a_logf[l].astype(F32), as_seq(lf),
                               jnp.zeros((bs, LANES - ts, N_HEADS), F32)], axis=1).transpose(0, 2, 1)
        flat_cache = lambda z: z[l].reshape(bs, z.shape[2], W_MIX)
        ya = _fox_sample(as_seq(qa), flat_cache(cache_a_k), flat_cache(cache_a_v), _pad_rows(as_seq(ka), LANES),
                         _pad_rows(as_seq(va), LANES), lft, ts)
        yb = _band_sample(as_seq(qb), flat_cache(cache_b_k), flat_cache(cache_b_v), _pad_rows(as_seq(kb), LANES),
                          _pad_rows(as_seq(vb), LANES), _band_bias_sample(rel_bias_b[l], ts, cache_b_k.shape[2]))
        hs = _channel(hs, ya.reshape(-1, W_MIX), yb.reshape(-1, W_MIX), ga, gb, p_sample[l].reshape(-1, PLE_DIM), w,
                      bs * ts, bs * ts)
        outs[5].append(heads(ka, bs, ts)); outs[6].append(heads(va, bs, ts)); outs[7].append(as_seq(lf))
        outs[8].append(heads(kb, bs, ts)); outs[9].append(heads(vb, bs, ts))
    return (hp.reshape(bp, tp, D_MODEL), hs.reshape(bs, ts, D_MODEL)) + tuple(jnp.stack(o) for o in outs)
```

```python
import functools
import math

import jax
import jax.numpy as jnp
from jax import lax
from jax.experimental import pallas as pl
from jax.experimental.pallas import tpu as pltpu

F32 = jnp.float32
BF16 = jnp.bfloat16

D_MODEL = 1024
HEAD_DIM = 64
N_HEADS = 8
W_MIX = N_HEADS * HEAD_DIM
PAIR = 2 * HEAD_DIM
N_PAIRS = N_HEADS // 2
CHUNK = 64
BAND_PAST = 8 * CHUNK
MAX_REL = 128
PLE_DIM = 256
PEER_HEADS = 8
N_KEYS = 128
N_EXPERTS = N_KEYS * N_KEYS
TOPK = 16
RMS_EPS = 1e-6
ATT_SCALE = HEAD_DIM ** -0.5
NEG = -1e30
RANK_SENTINEL = 2.0 ** 100
LANES = 128
SUBLANES = 8
EXPERT_CHUNK = SUBLANES * N_KEYS
VMEM_LIMIT = 56 * 1024 * 1024

N_CAND = 16 + 7 * 8 + 8


def _params(sem, vmem=VMEM_LIMIT):
    return pltpu.CompilerParams(dimension_semantics=sem, vmem_limit_bytes=vmem)


def _rms(x, g):
    return x * lax.rsqrt(jnp.mean(x * x, axis=-1, keepdims=True) + RMS_EPS) * g


def _dot(a, b):
    return jnp.dot(a, b, preferred_element_type=F32)


def _dot_t(a, b):
    return lax.dot_general(a, b, (((1,), (1,)), ((), ())), preferred_element_type=F32)


def _const_spec(shape):
    nd = len(shape)
    return pl.BlockSpec(shape, lambda *_: (0,) * nd)


def _proj_body(x_ref, g_ref, wa_ref, wf_ref, wb_ref, wg_ref, bf_ref, qna_ref, kna_ref, qnb_ref, knb_ref, msum_ref,
               qa_ref, ka_ref, va_ref, lf_ref, qb_ref, kb_ref, vb_ref, ga_ref, gb_ref):
    n1 = _rms(x_ref[...], g_ref[...]).astype(BF16)

    def head_norm(z, gain):
        ms = _dot((z * z).astype(BF16), msum_ref[...])
        return z * lax.rsqrt(ms + RMS_EPS) * gain

    za = _dot(n1, wa_ref[...])
    qa_ref[...] = (head_norm(za[:, :W_MIX], qna_ref[...]) * ATT_SCALE).astype(BF16)
    ka_ref[...] = head_norm(za[:, W_MIX:2 * W_MIX], kna_ref[...])
    va_ref[...] = za[:, 2 * W_MIX:]
    zb = _dot(n1, wb_ref[...])
    qb_ref[...] = (head_norm(zb[:, :W_MIX], qnb_ref[...]) * ATT_SCALE).astype(BF16)
    kb_ref[...] = head_norm(zb[:, W_MIX:2 * W_MIX], knb_ref[...])
    vb_ref[...] = zb[:, 2 * W_MIX:]
    fl = _dot(n1, wf_ref[...]) + bf_ref[...]
    ls = jnp.minimum(fl, 0.0) - jnp.log1p(jnp.exp(-jnp.abs(fl)))
    lf_ref[...] = ls[:, :N_HEADS]
    zg = _dot(n1, wg_ref[...])
    sg = 1.0 / (1.0 + jnp.exp(-zg))
    ga_ref[...] = sg[:, :D_MODEL].astype(BF16)
    gb_ref[...] = sg[:, D_MODEL:].astype(BF16)


def _proj(x, w, tb):
    n = x.shape[0]
    row = lambda width: pl.BlockSpec((tb, width), lambda i: (i, 0))
    ins = [x, w["g_mix"], w["w_a"], w["w_f"], w["w_b"], w["w_g"], w["b_f"], w["qn_a"], w["kn_a"], w["qn_b"],
           w["kn_b"], w["msum"]]
    in_specs = [row(D_MODEL)] + [_const_spec(a.shape) for a in ins[1:]]
    widths = [(W_MIX, BF16), (W_MIX, F32), (W_MIX, F32), (N_HEADS, F32), (W_MIX, BF16), (W_MIX, F32), (W_MIX, F32),
              (D_MODEL, BF16), (D_MODEL, BF16)]
    return pl.pallas_call(
        _proj_body,
        grid=(n // tb,),
        in_specs=in_specs,
        out_specs=[row(wd) for wd, _ in widths],
        out_shape=[jax.ShapeDtypeStruct((n, wd), dt) for wd, dt in widths],
        compiler_params=_params(("parallel",)),
        name="proj",
    )(*ins)


def _cumsum_lanes(x):
    n = x.shape[-1]
    lane = lax.broadcasted_iota(jnp.int32, x.shape, x.ndim - 1)
    s = 1
    while s < n:
        x = x + jnp.where(lane >= s, pltpu.roll(x, s, axis=x.ndim - 1), 0.0)
        s *= 2
    return x


def _head_of_pair(x, hh):
    lane = lax.broadcasted_iota(jnp.int32, (1, PAIR), 1)
    keep = (lane < HEAD_DIM) if hh == 0 else (lane >= HEAD_DIM)
    return jnp.where(keep, x, jnp.zeros_like(x))


def _merge_pair(o0, o1):
    lane = lax.broadcasted_iota(jnp.int32, (1, PAIR), 1)
    return jnp.where(lane < HEAD_DIM, o0, o1)


def _fox_body(q_ref, k_ref, v_ref, lft_ref, o_ref, kb_s, vb_s, c_s, *, tq, nq):
    hp = pl.program_id(1)
    qi = pl.program_id(2)

    @pl.when(qi == 0)
    def _():
        kb_s[...] = k_ref[0].astype(BF16)
        vb_s[...] = v_ref[0].astype(BF16)
        c_s[...] = _cumsum_lanes(lft_ref[0])

    q = q_ref[0]
    q2 = jnp.concatenate([_head_of_pair(q, 0), _head_of_pair(q, 1)], axis=0)
    row = lax.broadcasted_iota(jnp.int32, (tq, tq), 0)
    col = lax.broadcasted_iota(jnp.int32, (tq, tq), 1)

    def tile(n_blocks):
        past = (n_blocks - 1) * tq
        keys = n_blocks * tq
        s2 = _dot_t(q2, kb_s[:keys, :])
        probs, sums = [], []
        for hh in range(2):
            s = s2[hh * tq:(hh + 1) * tq] - c_s[pl.ds(2 * hp + hh, 1), :keys]
            s_diag = jnp.where(col <= row, s[:, past:], -jnp.inf)
            m = jnp.max(s_diag, axis=-1, keepdims=True)
            if past:
                m = jnp.maximum(m, jnp.max(s[:, :past], axis=-1, keepdims=True))
            p_diag = jnp.exp(s_diag - m)
            l = jnp.sum(p_diag, axis=-1, keepdims=True)
            if past:
                p_past = jnp.exp(s[:, :past] - m)
                l = l + jnp.sum(p_past, axis=-1, keepdims=True)
                p_diag = jnp.concatenate([p_past, p_diag], axis=1)
            probs.append(p_diag.astype(BF16))
            sums.append(l)
        o2 = _dot(jnp.concatenate(probs, axis=0), vb_s[:keys, :])
        o_ref[0] = _merge_pair(o2[:tq] / sums[0], o2[tq:] / sums[1]).astype(o_ref.dtype)

    for n_blocks in range(1, nq + 1):
        pl.when(qi == n_blocks - 1)(functools.partial(tile, n_blocks))


def _fox_prompt(q, k, v, lft, tq):
    b, t, _ = q.shape
    nq = t // tq
    return pl.pallas_call(
        functools.partial(_fox_body, tq=tq, nq=nq),
        grid=(b, N_PAIRS, nq),
        in_specs=[
            pl.BlockSpec((1, tq, PAIR), lambda i, p, j: (i, j, p)),
            pl.BlockSpec((1, t, PAIR), lambda i, p, j: (i, 0, p)),
            pl.BlockSpec((1, t, PAIR), lambda i, p, j: (i, 0, p)),
            pl.BlockSpec((1, N_HEADS, t), lambda i, p, j: (i, 0, 0)),
        ],
        out_specs=pl.BlockSpec((1, tq, PAIR), lambda i, p, j: (i, j, p)),
        out_shape=jax.ShapeDtypeStruct((b, t, W_MIX), BF16),
        scratch_shapes=[pltpu.VMEM((t, PAIR), BF16), pltpu.VMEM((t, PAIR), BF16), pltpu.VMEM((N_HEADS, t), F32)],
        compiler_params=_params(("parallel", "parallel", "arbitrary")),
        name="fox_prompt",
    )(q, k, v, lft)


def _band_body(q_ref, k_ref, v_ref, bias_ref, o_ref, kp_s, vp_s, *, tq):
    qi = pl.program_id(1)
    win = tq + BAND_PAST

    @pl.when(qi == 0)
    def _():
        zeros = jnp.zeros((BAND_PAST, W_MIX), BF16)
        kp_s[:BAND_PAST, :] = zeros
        vp_s[:BAND_PAST, :] = zeros
        kp_s[BAND_PAST:, :] = k_ref[0].astype(BF16)
        vp_s[BAND_PAST:, :] = v_ref[0].astype(BF16)

    off = pl.multiple_of(qi * tq, tq)
    exists = lax.broadcasted_iota(jnp.int32, (1, win), 1) >= BAND_PAST - qi * tq
    for pair in range(N_PAIRS):
        lanes = slice(pair * PAIR, (pair + 1) * PAIR)
        kw = kp_s[pl.ds(off, win), lanes]
        vw = vp_s[pl.ds(off, win), lanes]
        q = q_ref[0, :, lanes]
        outs = []
        for hh in range(2):
            s = _dot_t(_head_of_pair(q, hh), kw) + bias_ref[2 * pair + hh]
            s = jnp.where(exists, s, NEG)
            m = jnp.max(s, axis=-1, keepdims=True)
            p = jnp.exp(s - m)
            l = jnp.sum(p, axis=-1, keepdims=True)
            outs.append(_dot(p.astype(BF16), vw) / l)
        o_ref[0, :, lanes] = _merge_pair(outs[0], outs[1]).astype(o_ref.dtype)


def _band_prompt(q, k, v, bias, tq):
    b, t, _ = q.shape
    win = tq + BAND_PAST
    return pl.pallas_call(
        functools.partial(_band_body, tq=tq),
        grid=(b, t // tq),
        in_specs=[
            pl.BlockSpec((1, tq, W_MIX), lambda i, j: (i, j, 0)),
            pl.BlockSpec((1, t, W_MIX), lambda i, j: (i, 0, 0)),
            pl.BlockSpec((1, t, W_MIX), lambda i, j: (i, 0, 0)),
            _const_spec((N_HEADS, tq, win)),
        ],
        out_specs=pl.BlockSpec((1, tq, W_MIX), lambda i, j: (i, j, 0)),
        out_shape=jax.ShapeDtypeStruct((b, t, W_MIX), BF16),
        scratch_shapes=[pltpu.VMEM((t + BAND_PAST, W_MIX), BF16), pltpu.VMEM((t + BAND_PAST, W_MIX), BF16)],
        compiler_params=_params(("parallel", "arbitrary")),
        name="band_prompt",
    )(q, k, v, bias)


def _toeplitz(w, n, m):
    heads, span = w.shape
    hankel = jnp.tile(w, (1, n + 1))[:, :n * (span + 1)].reshape(heads, n, span + 1)[:, :, :m]
    return hankel[:, ::-1, :]


def _band_bias_prompt(rel_bias, tq):
    win = tq + BAND_PAST
    rel = jnp.arange(tq + win - 1) - (tq - 1) - BAND_PAST
    table = _toeplitz(rel_bias[:, jnp.clip(rel, -MAX_REL, MAX_REL) + MAX_REL].astype(F32), tq, win)
    ii = jnp.arange(tq)[:, None]
    jj = jnp.arange(win)[None, :]
    lo = (ii // CHUNK) * CHUNK
    in_band = (jj >= lo) & (jj < lo + BAND_PAST + CHUNK)
    return jnp.where(in_band[None], table, NEG)


def _fox_sample_body(q_ref, kc_ref, vc_ref, kn_ref, vn_ref, lft_ref, o_ref, *, past, n_new):
    hp = pl.program_id(1)
    c = _cumsum_lanes(lft_ref[0])
    kc = kc_ref[0].astype(BF16)
    vc = vc_ref[0].astype(BF16)
    kn = kn_ref[0].astype(BF16)
    vn = vn_ref[0].astype(BF16)
    q = q_ref[0]
    nq = q.shape[0]
    row = lax.broadcasted_iota(jnp.int32, (nq, LANES), 0)
    col = lax.broadcasted_iota(jnp.int32, (nq, LANES), 1)
    outs = []
    for hh in range(2):
        qh = _head_of_pair(q, hh)
        sel = lax.broadcasted_iota(jnp.int32, (N_HEADS, 1), 0) == 2 * hp + hh
        crow = jnp.sum(jnp.where(sel, c, 0.0), axis=0, keepdims=True)
        sc = _dot_t(qh, kc) - crow[:, :past]
        sn = _dot_t(qh, kn) - crow[:, past:]
        sn = jnp.where((col <= row) & (col < n_new), sn, -jnp.inf)
        m = jnp.maximum(jnp.max(sc, axis=-1, keepdims=True), jnp.max(sn, axis=-1, keepdims=True))
        pc = jnp.exp(sc - m)
        pn = jnp.exp(sn - m)
        l = jnp.sum(pc, axis=-1, keepdims=True) + jnp.sum(pn, axis=-1, keepdims=True)
        outs.append((_dot(pc.astype(BF16), vc) + _dot(pn.astype(BF16), vn)) / l)
    o_ref[0] = _merge_pair(outs[0], outs[1]).astype(o_ref.dtype)


def _fox_sample(q, kc, vc, kn, vn, lft, n_new):
    b, nq, _ = q.shape
    past = kc.shape[1]
    pair_spec = lambda rows: pl.BlockSpec((1, rows, PAIR), lambda i, p: (i, 0, p))
    return pl.pallas_call(
        functools.partial(_fox_sample_body, past=past, n_new=n_new),
        grid=(b, N_PAIRS),
        in_specs=[pair_spec(nq), pair_spec(past), pair_spec(past), pair_spec(LANES), pair_spec(LANES),
                  pl.BlockSpec((1, N_HEADS, past + LANES), lambda i, p: (i, 0, 0))],
        out_specs=pair_spec(nq),
        out_shape=jax.ShapeDtypeStruct((b, nq, W_MIX), BF16),
        compiler_params=_params(("parallel", "parallel")),
        name="fox_sample",
    )(q, kc, vc, kn, vn, lft)


def _band_sample_body(q_ref, kc_ref, vc_ref, kn_ref, vn_ref, bias_ref, o_ref, *, past):
    kc = kc_ref[0].astype(BF16)
    vc = vc_ref[0].astype(BF16)
    kn = kn_ref[0].astype(BF16)
    vn = vn_ref[0].astype(BF16)
    q = q_ref[0]
    outs = []
    for hh in range(2):
        qh = _head_of_pair(q, hh)
        bias = bias_ref[hh]
        sc = _dot_t(qh, kc) + bias[:, :past]
        sn = _dot_t(qh, kn) + bias[:, past:]
        m = jnp.maximum(jnp.max(sc, axis=-1, keepdims=True), jnp.max(sn, axis=-1, keepdims=True))
        pc = jnp.exp(sc - m)
        pn = jnp.exp(sn - m)
        l = jnp.sum(pc, axis=-1, keepdims=True) + jnp.sum(pn, axis=-1, keepdims=True)
        outs.append((_dot(pc.astype(BF16), vc) + _dot(pn.astype(BF16), vn)) / l)
    o_ref[0] = _merge_pair(outs[0], outs[1]).astype(o_ref.dtype)


def _band_sample(q, kc, vc, kn, vn, bias):
    b, nq, _ = q.shape
    past = kc.shape[1]
    pair_spec = lambda rows: pl.BlockSpec((1, rows, PAIR), lambda i, p: (i, 0, p))
    return pl.pallas_call(
        functools.partial(_band_sample_body, past=past),
        grid=(b, N_PAIRS),
        in_specs=[pair_spec(nq), pair_spec(past), pair_spec(past), pair_spec(LANES), pair_spec(LANES),
                  pl.BlockSpec((2, nq, past + LANES), lambda i, p: (p, 0, 0))],
        out_specs=pair_spec(nq),
        out_shape=jax.ShapeDtypeStruct((b, nq, W_MIX), BF16),
        compiler_params=_params(("parallel", "parallel")),
        name="band_sample",
    )(q, kc, vc, kn, vn, bias)


def _band_bias_sample(rel_bias, n_new, past):
    ii = jnp.arange(n_new)[:, None]
    jj = jnp.arange(past + LANES)[None, :]
    table = rel_bias[:, jnp.clip(jj - past - ii, -MAX_REL, MAX_REL) + MAX_REL].astype(F32)
    return jnp.where((jj < past + n_new)[None], table, NEG)


def _merge_body(x_ref, ya_ref, yb_ref, ga_ref, gb_ref, wua_ref, wub_ref, wo_ref, gffn_ref, h_ref, n2_ref):
    merged = (ga_ref[...].astype(F32) * _dot(ya_ref[...], wua_ref[...])
              + gb_ref[...].astype(F32) * _dot(yb_ref[...], wub_ref[...]))
    h = x_ref[...] + _dot(merged.astype(BF16), wo_ref[...])
    h_ref[...] = h
    n2_ref[...] = _rms(h, gffn_ref[...]).astype(BF16)


def _merge(x, ya, yb, ga, gb, w, tb):
    n = x.shape[0]
    row = lambda width: pl.BlockSpec((tb, width), lambda i: (i, 0))
    consts = [w["w_up_a"], w["w_up_b"], w["w_out"], w["g_ffn"]]
    return pl.pallas_call(
        _merge_body,
        grid=(n // tb,),
        in_specs=[row(D_MODEL), row(W_MIX), row(W_MIX), row(D_MODEL), row(D_MODEL)]
        + [_const_spec(a.shape) for a in consts],
        out_specs=[row(D_MODEL), row(D_MODEL)],
        out_shape=[jax.ShapeDtypeStruct((n, D_MODEL), F32), jax.ShapeDtypeStruct((n, D_MODEL), BF16)],
        compiler_params=_params(("parallel",)),
        name="merge",
    )(x, ya, yb, ga, gb, *consts)


def _leave(work, exact):
    hit = work == jnp.max(work, axis=0, keepdims=True)
    if exact:
        idx = lax.broadcasted_iota(jnp.int32, work.shape, 0)
        hit = idx == jnp.min(jnp.where(hit, idx, work.shape[0]), axis=0, keepdims=True)
    return hit


def _top16(s, vals_ref, exact):
    work = s
    for r in range(TOPK):
        vals_ref[r:r + 1, :] = jnp.max(work, axis=0, keepdims=True)
        work = jnp.where(_leave(work, exact), -RANK_SENTINEL * (1.0 + r / 32.0), work)
    return jnp.where(work <= -RANK_SENTINEL, work * (-32.0 / RANK_SENTINEL) - 32.0, float(TOPK))


def _count(mask):
    return jnp.sum(mask.astype(F32), axis=0, keepdims=True)


def _pair_bf16(x):
    bits = lax.bitcast_convert_type(x.astype(BF16).astype(F32), jnp.uint32)
    return bits | (bits >> 16)


def _retrieve_tile(s_s, tt, r1_ref, b_ref, a_ref, c_ref, va_s, vb_s, exact):
    off = jnp.zeros((1, LANES), F32)
    for h in range(PEER_HEADS):
        s0 = s_s[2 * h, tt]
        s1 = s_s[2 * h + 1, tt]
        rank0 = _top16(s0, va_s, exact)
        rank1 = _top16(s1, vb_s, exact)
        va = va_s[...]
        vb = vb_s[...]
        cand = jnp.concatenate([va[0:1] + vb] + [va[k:k + 1] + vb[0:8] for k in range(1, 8)] + [va[8:16] + vb[0:1]],
                               axis=0)
        work = cand
        picked = jnp.zeros(cand.shape, jnp.bool_)
        for _ in range(TOPK):
            hit = _leave(work, exact)
            picked = picked | hit
            work = jnp.where(hit, -jnp.inf, work)
        top = va[0:1] + vb[0:1]
        z = jnp.sum(jnp.where(picked, jnp.exp(cand - top), 0.0), axis=0, keepdims=True)
        pickf = picked.astype(F32)
        counts = [jnp.sum(pickf[0:16], axis=0, keepdims=True)]
        counts += [jnp.sum(pickf[8 + 8 * k:16 + 8 * k], axis=0, keepdims=True) for k in range(1, 8)]
        counts += [pickf[72 + k:73 + k] for k in range(8)]
        c = jnp.zeros(s0.shape, F32)
        for k in range(TOPK):
            c = jnp.where(rank0 == float(k), counts[k], c)
        a = jnp.where(rank0 < float(TOPK), jnp.exp(s0 - va[0:1]) * (1.0 / z), 0.0)
        b = jnp.where(rank1 < float(TOPK), jnp.exp(s1 - vb[0:1]), 0.0)
        r1_ref[tt, h] = rank1.astype(BF16)
        b_ref[tt, h] = b.astype(BF16)
        a_rows = _pair_bf16(a)
        c_rows = _pair_bf16(c)
        for grp in range(N_KEYS // SUBLANES):
            rows = slice(grp * SUBLANES, (grp + 1) * SUBLANES)
            a_ref[tt, h, grp] = a_rows[rows]
            c_ref[tt, h, grp] = c_rows[rows]
        if not exact:
            for n_left in (_count(rank0 < float(TOPK)), _count(rank1 < float(TOPK)), _count(picked)):
                off = off + jnp.abs(n_left - float(TOPK))
    return off


def _retrieve_body(n2_ref, wq_ref, sk_ref, r1_ref, b_ref, a_ref, c_ref, s_s, va_s, vb_s, *, tb):
    q = _dot(n2_ref[...], wq_ref[...]).astype(BF16)
    for h in range(PEER_HEADS):
        qh = q[:, h * PAIR:(h + 1) * PAIR]
        for half in range(2):
            s = _dot_t(sk_ref[h, half], qh)
            for tt in range(tb // LANES):
                s_s[2 * h + half, tt] = s[:, tt * LANES:(tt + 1) * LANES]

    def tile(tt, carry):
        maps = (r1_ref, b_ref, a_ref, c_ref, va_s, vb_s)
        off = _retrieve_tile(s_s, tt, *maps, exact=False)

        @pl.when(jnp.max(off) > 0.0)
        def _():
            _retrieve_tile(s_s, tt, *maps, exact=True)

        return carry

    lax.fori_loop(0, tb // LANES, tile, 0)


def _retrieve(n2, w, tb):
    n = n2.shape[0]
    nt = tb // LANES
    maps = pl.BlockSpec((nt, PEER_HEADS, N_KEYS, LANES), lambda i: (i, 0, 0, 0))
    rows = pl.BlockSpec((nt, PEER_HEADS, N_KEYS // SUBLANES, SUBLANES, LANES), lambda i: (i, 0, 0, 0, 0))
    map_shape = (n // LANES, PEER_HEADS, N_KEYS, LANES)
    row_shape = (n // LANES, PEER_HEADS, N_KEYS // SUBLANES, SUBLANES, LANES)
    return pl.pallas_call(
        functools.partial(_retrieve_body, tb=tb),
        grid=(n // tb,),
        in_specs=[pl.BlockSpec((tb, D_MODEL), lambda i: (i, 0)), _const_spec(w["peer_wq"].shape),
                  _const_spec(w["peer_sk"].shape)],
        out_specs=[maps, maps, rows, rows],
        out_shape=[jax.ShapeDtypeStruct(map_shape, BF16), jax.ShapeDtypeStruct(map_shape, BF16),
                   jax.ShapeDtypeStruct(row_shape, jnp.uint32), jax.ShapeDtypeStruct(row_shape, jnp.uint32)],
        scratch_shapes=[pltpu.VMEM((2 * PEER_HEADS, nt, N_KEYS, LANES), F32), pltpu.VMEM((TOPK, LANES), F32),
                        pltpu.VMEM((TOPK, LANES), F32)],
        compiler_params=_params(("parallel",)),
        name="peer_retrieve",
    )(n2, w["peer_wq"], w["peer_sk"])


def _gelu_tanh(x):
    return 0.5 * x * (1.0 + jnp.tanh(math.sqrt(2.0 / math.pi) * (x + 0.044715 * (x * x * x))))


def _row_tile(ref, tt, h, g, ii):
    row = jnp.broadcast_to(ref[tt, h, g, ii:ii + 1, :], (SUBLANES, LANES))
    packed = pltpu.bitcast(row, BF16)
    return jnp.concatenate([packed] * (N_KEYS // packed.shape[0]), axis=0)


def _dense_body(n2_ref, u_ref, vt_ref, r1_ref, b_ref, a_ref, c_ref, o_ref, acc_s, gate_s, r1_s, b_s, *, tb):
    g = pl.program_id(1)

    @pl.when(g == 0)
    def _():
        acc_s[...] = jnp.zeros(acc_s.shape, F32)
        r1_s[...] = r1_ref[...]
        b_s[...] = b_ref[...]

    for tt in range(tb // LANES):
        lanes = slice(tt * LANES, (tt + 1) * LANES)
        for ii in range(SUBLANES):
            rows = slice(ii * N_KEYS, (ii + 1) * N_KEYS)
            gate = None
            for h in range(PEER_HEADS):
                wgt = b_s[tt, h] * _row_tile(a_ref, tt, h, g, ii)
                term = jnp.where(r1_s[tt, h] < _row_tile(c_ref, tt, h, g, ii), wgt, jnp.zeros_like(wgt))
                gate = term if gate is None else gate + term
            gate_s[rows, lanes] = gate

    hid = _dot_t(u_ref[...], n2_ref[...])
    weighted = gate_s[...] * _gelu_tanh(hid).astype(BF16)
    acc_s[...] += _dot(vt_ref[0], weighted)

    @pl.when(g == pl.num_programs(1) - 1)
    def _():
        o_ref[...] = acc_s[...].T


def _dense(n2, u, vt, r1, b, a, c, tb):
    n = n2.shape[0]
    ec = EXPERT_CHUNK
    ng = N_EXPERTS // ec
    nt = tb // LANES
    maps = pl.BlockSpec((nt, PEER_HEADS, N_KEYS, LANES), lambda i, g: (i, 0, 0, 0))
    rows = pl.BlockSpec((nt, PEER_HEADS, ng, SUBLANES, LANES), lambda i, g: (i, 0, 0, 0, 0))
    return pl.pallas_call(
        functools.partial(_dense_body, tb=tb),
        grid=(n // tb, ng),
        in_specs=[pl.BlockSpec((tb, D_MODEL), lambda i, g: (i, 0)),
                  pl.BlockSpec((ec, D_MODEL), lambda i, g: (g, 0)),
                  pl.BlockSpec((1, D_MODEL, ec), lambda i, g: (g, 0, 0)),
                  maps, maps, rows, rows],
        out_specs=pl.BlockSpec((tb, D_MODEL), lambda i, g: (i, 0)),
        out_shape=jax.ShapeDtypeStruct((n, D_MODEL), F32),
        scratch_shapes=[pltpu.VMEM((D_MODEL, tb), F32), pltpu.VMEM((ec, tb), BF16),
                        pltpu.VMEM((nt, PEER_HEADS, N_KEYS, LANES), BF16),
                        pltpu.VMEM((nt, PEER_HEADS, N_KEYS, LANES), BF16)],
        compiler_params=_params(("parallel", "arbitrary")),
        name="peer_dense",
    )(n2, u, vt, r1, b, a, c)


def _final_body(h_ref, peer_ref, p_ref, gple_ref, wg_ref, wp_ref, o_ref):
    h = h_ref[...] + peer_ref[...]
    zg = _dot(_rms(h, gple_ref[...]).astype(BF16), wg_ref[...])
    gate = 1.0 / (1.0 + jnp.exp(-zg))
    o_ref[...] = h + gate * _dot(p_ref[...].astype(BF16), wp_ref[...])


def _final(h, peer, p, w, tb):
    n = h.shape[0]
    row = lambda width: pl.BlockSpec((tb, width), lambda i: (i, 0))
    consts = [w["g_ple"], w["w_ple_gate"], w["w_ple_proj"]]
    return pl.pallas_call(
        _final_body,
        grid=(n // tb,),
        in_specs=[row(D_MODEL), row(D_MODEL), row(PLE_DIM)] + [_const_spec(a.shape) for a in consts],
        out_specs=row(D_MODEL),
        out_shape=jax.ShapeDtypeStruct((n, D_MODEL), F32),
        compiler_params=_params(("parallel",)),
        name="final",
    )(h, peer, p, *consts)


def _layer_weights(l, g_mix, w_in, b_f, qn_a, kn_a, qn_b, kn_b, w_up_a, w_up_b, w_out, g_ffn, peer_wq, peer_subkeys,
                   peer_u, peer_v, g_ple, w_ple_gate, w_ple_proj):
    o_f = 3 * W_MIX
    o_b = o_f + N_HEADS
    o_g = o_b + 3 * W_MIX
    wi = w_in[l]
    tile_heads = lambda g: jnp.tile(g[l].astype(F32), N_HEADS)[None, :]
    head_of = jnp.arange(W_MIX) // HEAD_DIM
    sk = peer_subkeys[l].astype(BF16)
    zeros = jnp.zeros_like(sk[:, 0])
    sk_pad = jnp.stack([jnp.concatenate([sk[:, 0], zeros], axis=-1), jnp.concatenate([zeros, sk[:, 1]], axis=-1)],
                       axis=1)
    return {
        "g_mix": g_mix[l][None, :],
        "w_a": wi[:, :o_f].astype(BF16),
        "w_f": jnp.pad(wi[:, o_f:o_b], ((0, 0), (0, LANES - N_HEADS))).astype(BF16),
        "w_b": wi[:, o_b:o_g].astype(BF16),
        "w_g": wi[:, o_g:].astype(BF16),
        "b_f": jnp.pad(b_f[l], (0, LANES - N_HEADS))[None, :],
        "qn_a": tile_heads(qn_a), "kn_a": tile_heads(kn_a), "qn_b": tile_heads(qn_b), "kn_b": tile_heads(kn_b),
        "msum": jnp.where(head_of[:, None] == head_of[None, :], 1.0 / HEAD_DIM, 0.0).astype(BF16),
        "w_up_a": w_up_a[l].astype(BF16), "w_up_b": w_up_b[l].astype(BF16), "w_out": w_out[l].astype(BF16),
        "g_ffn": g_ffn[l][None, :],
        "peer_wq": peer_wq[l].astype(BF16),
        "peer_sk": sk_pad,
        "peer_u": peer_u[l].astype(BF16),
        "peer_vt": peer_v[l].astype(BF16).reshape(-1, EXPERT_CHUNK, D_MODEL).transpose(0, 2, 1),
        "g_ple": g_ple[l][None, :],
        "w_ple_gate": w_ple_gate[l].astype(BF16),
        "w_ple_proj": w_ple_proj[l].astype(BF16),
    }


def _channel(x, ya, yb, ga, gb, p, w, tb, tb_dense):
    h1, n2 = _merge(x, ya, yb, ga, gb, w, tb)
    r1, b, a, c = _retrieve(n2, w, tb_dense)
    peer = _dense(n2, w["peer_u"], w["peer_vt"], r1, b, a, c, tb_dense)
    return _final(h1, peer, p, w, tb)


def _pad_rows(x, rows):
    return jnp.pad(x, ((0, 0), (0, rows - x.shape[1]), (0, 0)))


def kernel(x_prompt, x_sample, cache_a_k, cache_a_v, cache_a_logf, cache_b_k, cache_b_v, p_prompt, p_sample, g_mix, w_in, b_f, qn_a, kn_a, qn_b, kn_b, rel_bias_b, w_up_a, w_up_b, w_out, g_ffn, peer_wq, peer_subkeys, peer_u, peer_v, g_ple, w_ple_gate, w_ple_proj):
    depth = w_in.shape[0]
    bp, tp, _ = x_prompt.shape
    bs, ts, _ = x_sample.shape
    past = cache_a_k.shape[2]
    band_rows = min(BAND_PAST, tp)
    tq = 256
    hp = x_prompt.reshape(bp * tp, D_MODEL)
    hs = x_sample.reshape(bs * ts, D_MODEL)
    outs = [[] for _ in range(10)]
    for l in range(depth):
        w = _layer_weights(l, g_mix, w_in, b_f, qn_a, kn_a, qn_b, kn_b, w_up_a, w_up_b, w_out, g_ffn, peer_wq,
                           peer_subkeys, peer_u, peer_v, g_ple, w_ple_gate, w_ple_proj)
        qa, ka, va, lf, qb, kb, vb, ga, gb = _proj(hp, w, 256)
        as_seq = lambda z: z.reshape(bp, tp, z.shape[-1])
        ya = _fox_prompt(as_seq(qa), as_seq(ka), as_seq(va), as_seq(lf).transpose(0, 2, 1), tq)
        yb = _band_prompt(as_seq(qb), as_seq(kb), as_seq(vb), _band_bias_prompt(rel_bias_b[l], tq), tq)
        hp = _channel(hp, ya.reshape(-1, W_MIX), yb.reshape(-1, W_MIX), ga, gb, p_prompt[l].reshape(-1, PLE_DIM), w,
                      256, 512)
        heads = lambda z, b_, t_: z.reshape(b_, t_, N_HEADS, HEAD_DIM)
        outs[0].append(heads(ka, bp, tp)); outs[1].append(heads(va, bp, tp)); outs[2].append(as_seq(lf))
        outs[3].append(heads(kb, bp, tp)[:, -band_rows:]); outs[4].append(heads(vb, bp, tp)[:, -band_rows:])
        qa, ka, va, lf, qb, kb, vb, ga, gb = _proj(hs, w, bs * ts)
        as_seq = lambda z: z.reshape(bs, ts, z.shape[-1])
        lft = jnp.concatenate([cache_a_logf[l].astype(F32), as_seq(lf),
                               jnp.zeros((bs, LANES - ts, N_HEADS), F32)], axis=1).transpose(0, 2, 1)
        flat_cache = lambda z: z[l].reshape(bs, z.shape[2], W_MIX)
        ya = _fox_sample(as_seq(qa), flat_cache(cache_a_k), flat_cache(cache_a_v), _pad_rows(as_seq(ka), LANES),
                         _pad_rows(as_seq(va), LANES), lft, ts)
        yb = _band_sample(as_seq(qb), flat_cache(cache_b_k), flat_cache(cache_b_v), _pad_rows(as_seq(kb), LANES),
                          _pad_rows(as_seq(vb), LANES), _band_bias_sample(rel_bias_b[l], ts, cache_b_k.shape[2]))
        hs = _channel(hs, ya.reshape(-1, W_MIX), yb.reshape(-1, W_MIX), ga, gb, p_sample[l].reshape(-1, PLE_DIM), w,
                      bs * ts, bs * ts)
        outs[5].append(heads(ka, bs, ts)); outs[6].append(heads(va, bs, ts)); outs[7].append(as_seq(lf))
        outs[8].append(heads(kb, bs, ts)); outs[9].append(heads(vb, bs, ts))
    return (hp.reshape(bp, tp, D_MODEL), hs.reshape(bs, ts, D_MODEL)) + tuple(jnp.stack(o) for o in outs)
```

```python
import functools
import math

import jax
import jax.numpy as jnp
from jax import lax
from jax.experimental import pallas as pl
from jax.experimental.pallas import tpu as pltpu

F32 = jnp.float32
BF16 = jnp.bfloat16

D_MODEL = 1024
HEAD_DIM = 64
N_HEADS = 8
W_MIX = N_HEADS * HEAD_DIM
PAIR = 2 * HEAD_DIM
N_PAIRS = N_HEADS // 2
CHUNK = 64
BAND_PAST = 8 * CHUNK
MAX_REL = 128
PLE_DIM = 256
PEER_HEADS = 8
N_KEYS = 128
N_EXPERTS = N_KEYS * N_KEYS
TOPK = 16
RMS_EPS = 1e-6
ATT_SCALE = HEAD_DIM ** -0.5
NEG = -1e30
RANK_SENTINEL = 2.0 ** 100
LANES = 128
SUBLANES = 8
EXPERT_CHUNK = SUBLANES * N_KEYS
VMEM_LIMIT = 56 * 1024 * 1024

N_CAND = 16 + 7 * 8 + 8


def _params(sem, vmem=VMEM_LIMIT):
    return pltpu.CompilerParams(dimension_semantics=sem, vmem_limit_bytes=vmem)


def _rms(x, g):
    return x * lax.rsqrt(jnp.mean(x * x, axis=-1, keepdims=True) + RMS_EPS) * g


def _dot(a, b):
    return jnp.dot(a, b, preferred_element_type=F32)


def _dot_t(a, b):
    return lax.dot_general(a, b, (((1,), (1,)), ((), ())), preferred_element_type=F32)


def _const_spec(shape):
    nd = len(shape)
    return pl.BlockSpec(shape, lambda *_: (0,) * nd)


def _proj_body(x_ref, g_ref, wa_ref, wf_ref, wb_ref, wg_ref, bf_ref, qna_ref, kna_ref, qnb_ref, knb_ref, msum_ref,
               qa_ref, ka_ref, va_ref, lf_ref, qb_ref, kb_ref, vb_ref, ga_ref, gb_ref):
    n1 = _rms(x_ref[...], g_ref[...]).astype(BF16)

    def head_norm(z, gain):
        ms = _dot((z * z).astype(BF16), msum_ref[...])
        return z * lax.rsqrt(ms + RMS_EPS) * gain

    za = _dot(n1, wa_ref[...])
    qa_ref[...] = (head_norm(za[:, :W_MIX], qna_ref[...]) * ATT_SCALE).astype(BF16)
    ka_ref[...] = head_norm(za[:, W_MIX:2 * W_MIX], kna_ref[...])
    va_ref[...] = za[:, 2 * W_MIX:]
    zb = _dot(n1, wb_ref[...])
    qb_ref[...] = (head_norm(zb[:, :W_MIX], qnb_ref[...]) * ATT_SCALE).astype(BF16)
    kb_ref[...] = head_norm(zb[:, W_MIX:2 * W_MIX], knb_ref[...])
    vb_ref[...] = zb[:, 2 * W_MIX:]
    fl = _dot(n1, wf_ref[...]) + bf_ref[...]
    ls = jnp.minimum(fl, 0.0) - jnp.log1p(jnp.exp(-jnp.abs(fl)))
    lf_ref[...] = ls[:, :N_HEADS]
    zg = _dot(n1, wg_ref[...])
    sg = 1.0 / (1.0 + jnp.exp(-zg))
    ga_ref[...] = sg[:, :D_MODEL].astype(BF16)
    gb_ref[...] = sg[:, D_MODEL:].astype(BF16)


def _proj(x, w, tb):
    n = x.shape[0]
    row = lambda width: pl.BlockSpec((tb, width), lambda i: (i, 0))
    ins = [x, w["g_mix"], w["w_a"], w["w_f"], w["w_b"], w["w_g"], w["b_f"], w["qn_a"], w["kn_a"], w["qn_b"],
           w["kn_b"], w["msum"]]
    in_specs = [row(D_MODEL)] + [_const_spec(a.shape) for a in ins[1:]]
    widths = [(W_MIX, BF16), (W_MIX, F32), (W_MIX, F32), (N_HEADS, F32), (W_MIX, BF16), (W_MIX, F32), (W_MIX, F32),
              (D_MODEL, BF16), (D_MODEL, BF16)]
    return pl.pallas_call(
        _proj_body,
        grid=(n // tb,),
        in_specs=in_specs,
        out_specs=[row(wd) for wd, _ in widths],
        out_shape=[jax.ShapeDtypeStruct((n, wd), dt) for wd, dt in widths],
        compiler_params=_params(("parallel",)),
        name="proj",
    )(*ins)


def _cumsum_lanes(x):
    n = x.shape[-1]
    lane = lax.broadcasted_iota(jnp.int32, x.shape, x.ndim - 1)
    s = 1
    while s < n:
        x = x + jnp.where(lane >= s, pltpu.roll(x, s, axis=x.ndim - 1), 0.0)
        s *= 2
    return x


def _head_of_pair(x, hh):
    lane = lax.broadcasted_iota(jnp.int32, (1, PAIR), 1)
    keep = (lane < HEAD_DIM) if hh == 0 else (lane >= HEAD_DIM)
    return jnp.where(keep, x, jnp.zeros_like(x))


def _merge_pair(o0, o1):
    lane = lax.broadcasted_iota(jnp.int32, (1, PAIR), 1)
    return jnp.where(lane < HEAD_DIM, o0, o1)


def _fox_body(q_ref, k_ref, v_ref, lft_ref, o_ref, kb_s, vb_s, c_s, *, tq, nq):
    hp = pl.program_id(1)
    qi = pl.program_id(2)

    @pl.when(qi == 0)
    def _():
        kb_s[...] = k_ref[0].astype(BF16)
        vb_s[...] = v_ref[0].astype(BF16)
        c_s[...] = _cumsum_lanes(lft_ref[0])

    q = q_ref[0]
    q2 = jnp.concatenate([_head_of_pair(q, 0), _head_of_pair(q, 1)], axis=0)
    row = lax.broadcasted_iota(jnp.int32, (tq, tq), 0)
    col = lax.broadcasted_iota(jnp.int32, (tq, tq), 1)

    def tile(n_blocks):
        past = (n_blocks - 1) * tq
        keys = n_blocks * tq
        s2 = _dot_t(q2, kb_s[:keys, :])
        probs, sums = [], []
        for hh in range(2):
            s = s2[hh * tq:(hh + 1) * tq] - c_s[pl.ds(2 * hp + hh, 1), :keys]
            s_diag = jnp.where(col <= row, s[:, past:], -jnp.inf)
            m = jnp.max(s_diag, axis=-1, keepdims=True)
            if past:
                m = jnp.maximum(m, jnp.max(s[:, :past], axis=-1, keepdims=True))
            p_diag = jnp.exp(s_diag - m)
            l = jnp.sum(p_diag, axis=-1, keepdims=True)
            if past:
                p_past = jnp.exp(s[:, :past] - m)
                l = l + jnp.sum(p_past, axis=-1, keepdims=True)
                p_diag = jnp.concatenate([p_past, p_diag], axis=1)
            probs.append(p_diag.astype(BF16))
            sums.append(l)
        o2 = _dot(jnp.concatenate(probs, axis=0), vb_s[:keys, :])
        o_ref[0] = _merge_pair(o2[:tq] / sums[0], o2[tq:] / sums[1]).astype(o_ref.dtype)

    for n_blocks in range(1, nq + 1):
        pl.when(qi == n_blocks - 1)(functools.partial(tile, n_blocks))


def _fox_prompt(q, k, v, lft, tq):
    b, t, _ = q.shape
    nq = t // tq
    return pl.pallas_call(
        functools.partial(_fox_body, tq=tq, nq=nq),
        grid=(b, N_PAIRS, nq),
        in_specs=[
            pl.BlockSpec((1, tq, PAIR), lambda i, p, j: (i, j, p)),
            pl.BlockSpec((1, t, PAIR), lambda i, p, j: (i, 0, p)),
            pl.BlockSpec((1, t, PAIR), lambda i, p, j: (i, 0, p)),
            pl.BlockSpec((1, N_HEADS, t), lambda i, p, j: (i, 0, 0)),
        ],
        out_specs=pl.BlockSpec((1, tq, PAIR), lambda i, p, j: (i, j, p)),
        out_shape=jax.ShapeDtypeStruct((b, t, W_MIX), BF16),
        scratch_shapes=[pltpu.VMEM((t, PAIR), BF16), pltpu.VMEM((t, PAIR), BF16), pltpu.VMEM((N_HEADS, t), F32)],
        compiler_params=_params(("parallel", "parallel", "arbitrary")),
        name="fox_prompt",
    )(q, k, v, lft)


def _band_body(q_ref, k_ref, v_ref, bias_ref, o_ref, kp_s, vp_s, *, tq):
    qi = pl.program_id(1)
    win = tq + BAND_PAST

    @pl.when(qi == 0)
    def _():
        zeros = jnp.zeros((BAND_PAST, W_MIX), BF16)
        kp_s[:BAND_PAST, :] = zeros
        vp_s[:BAND_PAST, :] = zeros
        kp_s[BAND_PAST:, :] = k_ref[0].astype(BF16)
        vp_s[BAND_PAST:, :] = v_ref[0].astype(BF16)

    off = pl.multiple_of(qi * tq, tq)
    exists = lax.broadcasted_iota(jnp.int32, (1, win), 1) >= BAND_PAST - qi * tq
    for pair in range(N_PAIRS):
        lanes = slice(pair * PAIR, (pair + 1) * PAIR)
        kw = kp_s[pl.ds(off, win), lanes]
        vw = vp_s[pl.ds(off, win), lanes]
        q = q_ref[0, :, lanes]
        outs = []
        for hh in range(2):
            s = _dot_t(_head_of_pair(q, hh), kw) + bias_ref[2 * pair + hh]
            s = jnp.where(exists, s, NEG)
            m = jnp.max(s, axis=-1, keepdims=True)
            p = jnp.exp(s - m)
            l = jnp.sum(p, axis=-1, keepdims=True)
            outs.append(_dot(p.astype(BF16), vw) / l)
        o_ref[0, :, lanes] = _merge_pair(outs[0], outs[1]).astype(o_ref.dtype)


def _band_prompt(q, k, v, bias, tq):
    b, t, _ = q.shape
    win = tq + BAND_PAST
    return pl.pallas_call(
        functools.partial(_band_body, tq=tq),
        grid=(b, t // tq),
        in_specs=[
            pl.BlockSpec((1, tq, W_MIX), lambda i, j: (i, j, 0)),
            pl.BlockSpec((1, t, W_MIX), lambda i, j: (i, 0, 0)),
            pl.BlockSpec((1, t, W_MIX), lambda i, j: (i, 0, 0)),
            _const_spec((N_HEADS, tq, win)),
        ],
        out_specs=pl.BlockSpec((1, tq, W_MIX), lambda i, j: (i, j, 0)),
        out_shape=jax.ShapeDtypeStruct((b, t, W_MIX), BF16),
        scratch_shapes=[pltpu.VMEM((t + BAND_PAST, W_MIX), BF16), pltpu.VMEM((t + BAND_PAST, W_MIX), BF16)],
        compiler_params=_params(("parallel", "arbitrary")),
        name="band_prompt",
    )(q, k, v, bias)


def _toeplitz(w, n, m):
    heads, span = w.shape
    hankel = jnp.tile(w, (1, n + 1))[:, :n * (span + 1)].reshape(heads, n, span + 1)[:, :, :m]
    return hankel[:, ::-1, :]


def _band_bias_prompt(rel_bias, tq):
    win = tq + BAND_PAST
    rel = jnp.arange(tq + win - 1) - (tq - 1) - BAND_PAST
    table = _toeplitz(rel_bias[:, jnp.clip(rel, -MAX_REL, MAX_REL) + MAX_REL].astype(F32), tq, win)
    ii = jnp.arange(tq)[:, None]
    jj = jnp.arange(win)[None, :]
    lo = (ii // CHUNK) * CHUNK
    in_band = (jj >= lo) & (jj < lo + BAND_PAST + CHUNK)
    return jnp.where(in_band[None], table, NEG)


def _fox_sample_body(q_ref, kc_ref, vc_ref, kn_ref, vn_ref, lft_ref, o_ref, *, past, n_new):
    hp = pl.program_id(1)
    c = _cumsum_lanes(lft_ref[0])
    kc = kc_ref[0].astype(BF16)
    vc = vc_ref[0].astype(BF16)
    kn = kn_ref[0].astype(BF16)
    vn = vn_ref[0].astype(BF16)
    q = q_ref[0]
    nq = q.shape[0]
    row = lax.broadcasted_iota(jnp.int32, (nq, LANES), 0)
    col = lax.broadcasted_iota(jnp.int32, (nq, LANES), 1)
    outs = []
    for hh in range(2):
        qh = _head_of_pair(q, hh)
        sel = lax.broadcasted_iota(jnp.int32, (N_HEADS, 1), 0) == 2 * hp + hh
        crow = jnp.sum(jnp.where(sel, c, 0.0), axis=0, keepdims=True)
        sc = _dot_t(qh, kc) - crow[:, :past]
        sn = _dot_t(qh, kn) - crow[:, past:]
        sn = jnp.where((col <= row) & (col < n_new), sn, -jnp.inf)
        m = jnp.maximum(jnp.max(sc, axis=-1, keepdims=True), jnp.max(sn, axis=-1, keepdims=True))
        pc = jnp.exp(sc - m)
        pn = jnp.exp(sn - m)
        l = jnp.sum(pc, axis=-1, keepdims=True) + jnp.sum(pn, axis=-1, keepdims=True)
        outs.append((_dot(pc.astype(BF16), vc) + _dot(pn.astype(BF16), vn)) / l)
    o_ref[0] = _merge_pair(outs[0], outs[1]).astype(o_ref.dtype)


def _fox_sample(q, kc, vc, kn, vn, lft, n_new):
    b, nq, _ = q.shape
    past = kc.shape[1]
    pair_spec = lambda rows: pl.BlockSpec((1, rows, PAIR), lambda i, p: (i, 0, p))
    return pl.pallas_call(
        functools.partial(_fox_sample_body, past=past, n_new=n_new),
        grid=(b, N_PAIRS),
        in_specs=[pair_spec(nq), pair_spec(past), pair_spec(past), pair_spec(LANES), pair_spec(LANES),
                  pl.BlockSpec((1, N_HEADS, past + LANES), lambda i, p: (i, 0, 0))],
        out_specs=pair_spec(nq),
        out_shape=jax.ShapeDtypeStruct((b, nq, W_MIX), BF16),
        compiler_params=_params(("parallel", "parallel")),
        name="fox_sample",
    )(q, kc, vc, kn, vn, lft)


def _band_sample_body(q_ref, kc_ref, vc_ref, kn_ref, vn_ref, bias_ref, o_ref, *, past):
    kc = kc_ref[0].astype(BF16)
    vc = vc_ref[0].astype(BF16)
    kn = kn_ref[0].astype(BF16)
    vn = vn_ref[0].astype(BF16)
    q = q_ref[0]
    outs = []
    for hh in range(2):
        qh = _head_of_pair(q, hh)
        bias = bias_ref[hh]
        sc = _dot_t(qh, kc) + bias[:, :past]
        sn = _dot_t(qh, kn) + bias[:, past:]
        m = jnp.maximum(jnp.max(sc, axis=-1, keepdims=True), jnp.max(sn, axis=-1, keepdims=True))
        pc = jnp.exp(sc - m)
        pn = jnp.exp(sn - m)
        l = jnp.sum(pc, axis=-1, keepdims=True) + jnp.sum(pn, axis=-1, keepdims=True)
        outs.append((_dot(pc.astype(BF16), vc) + _dot(pn.astype(BF16), vn)) / l)
    o_ref[0] = _merge_pair(outs[0], outs[1]).astype(o_ref.dtype)


def _band_sample(q, kc, vc, kn, vn, bias):
    b, nq, _ = q.shape
    past = kc.shape[1]
    pair_spec = lambda rows: pl.BlockSpec((1, rows, PAIR), lambda i, p: (i, 0, p))
    return pl.pallas_call(
        functools.partial(_band_sample_body, past=past),
        grid=(b, N_PAIRS),
        in_specs=[pair_spec(nq), pair_spec(past), pair_spec(past), pair_spec(LANES), pair_spec(LANES),
                  pl.BlockSpec((2, nq, past + LANES), lambda i, p: (p, 0, 0))],
        out_specs=pair_spec(nq),
        out_shape=jax.ShapeDtypeStruct((b, nq, W_MIX), BF16),
        compiler_params=_params(("parallel", "parallel")),
        name="band_sample",
    )(q, kc, vc, kn, vn, bias)


def _band_bias_sample(rel_bias, n_new, past):
    ii = jnp.arange(n_new)[:, None]
    jj = jnp.arange(past + LANES)[None, :]
    table = rel_bias[:, jnp.clip(jj - past - ii, -MAX_REL, MAX_REL) + MAX_REL].astype(F32)
    return jnp.where((jj < past + n_new)[None], table, NEG)


def _merge_body(x_ref, ya_ref, yb_ref, ga_ref, gb_ref, wua_ref, wub_ref, wo_ref, gffn_ref, h_ref, n2_ref):
    merged = (ga_ref[...].astype(F32) * _dot(ya_ref[...], wua_ref[...])
              + gb_ref[...].astype(F32) * _dot(yb_ref[...], wub_ref[...]))
    h = x_ref[...] + _dot(merged.astype(BF16), wo_ref[...])
    h_ref[...] = h
    n2_ref[...] = _rms(h, gffn_ref[...]).astype(BF16)


def _merge(x, ya, yb, ga, gb, w, tb):
    n = x.shape[0]
    row = lambda width: pl.BlockSpec((tb, width), lambda i: (i, 0))
    consts = [w["w_up_a"], w["w_up_b"], w["w_out"], w["g_ffn"]]
    return pl.pallas_call(
        _merge_body,
        grid=(n // tb,),
        in_specs=[row(D_MODEL), row(W_MIX), row(W_MIX), row(D_MODEL), row(D_MODEL)]
        + [_const_spec(a.shape) for a in consts],
        out_specs=[row(D_MODEL), row(D_MODEL)],
        out_shape=[jax.ShapeDtypeStruct((n, D_MODEL), F32), jax.ShapeDtypeStruct((n, D_MODEL), BF16)],
        compiler_params=_params(("parallel",)),
        name="merge",
    )(x, ya, yb, ga, gb, *consts)


def _leave(work, exact):
    hit = work == jnp.max(work, axis=0, keepdims=True)
    if exact:
        idx = lax.broadcasted_iota(jnp.int32, work.shape, 0)
        hit = idx == jnp.min(jnp.where(hit, idx, work.shape[0]), axis=0, keepdims=True)
    return hit


def _top16(s, vals_ref, exact):
    work = s
    for r in range(TOPK):
        vals_ref[r:r + 1, :] = jnp.max(work, axis=0, keepdims=True)
        work = jnp.where(_leave(work, exact), -RANK_SENTINEL * (1.0 + r / 32.0), work)
    return jnp.where(work <= -RANK_SENTINEL, work * (-32.0 / RANK_SENTINEL) - 32.0, float(TOPK))


def _count(mask):
    return jnp.sum(mask.astype(F32), axis=0, keepdims=True)


def _pair_bf16(x):
    bits = lax.bitcast_convert_type(x.astype(BF16).astype(F32), jnp.uint32)
    return bits | (bits >> 16)


def _retrieve_tile(s_s, tt, r1_ref, b_ref, a_ref, c_ref, va_s, vb_s, exact):
    off = jnp.zeros((1, LANES), F32)
    for h in range(PEER_HEADS):
        s0 = s_s[2 * h, tt]
        s1 = s_s[2 * h + 1, tt]
        rank0 = _top16(s0, va_s, exact)
        rank1 = _top16(s1, vb_s, exact)
        va = va_s[...]
        vb = vb_s[...]
        cand = jnp.concatenate([va[0:1] + vb] + [va[k:k + 1] + vb[0:8] for k in range(1, 8)] + [va[8:16] + vb[0:1]],
                               axis=0)
        work = cand
        picked = jnp.zeros(cand.shape, jnp.bool_)
        for _ in range(TOPK):
            hit = _leave(work, exact)
            picked = picked | hit
            work = jnp.where(hit, -jnp.inf, work)
        top = va[0:1] + vb[0:1]
        z = jnp.sum(jnp.where(picked, jnp.exp(cand - top), 0.0), axis=0, keepdims=True)
        pickf = picked.astype(F32)
        counts = [jnp.sum(pickf[0:16], axis=0, keepdims=True)]
        counts += [jnp.sum(pickf[8 + 8 * k:16 + 8 * k], axis=0, keepdims=True) for k in range(1, 8)]
        counts += [pickf[72 + k:73 + k] for k in range(8)]
        c = jnp.zeros(s0.shape, F32)
        for k in range(TOPK):
            c = jnp.where(rank0 == float(k), counts[k], c)
        a = jnp.where(rank0 < float(TOPK), jnp.exp(s0 - va[0:1]) * (1.0 / z), 0.0)
        b = jnp.where(rank1 < float(TOPK), jnp.exp(s1 - vb[0:1]), 0.0)
        r1_ref[tt, h] = rank1.astype(BF16)
        b_ref[tt, h] = b.astype(BF16)
        a_rows = _pair_bf16(a)
        c_rows = _pair_bf16(c)
        for grp in range(N_KEYS // SUBLANES):
            rows = slice(grp * SUBLANES, (grp + 1) * SUBLANES)
            a_ref[tt, h, grp] = a_rows[rows]
            c_ref[tt, h, grp] = c_rows[rows]
        if not exact:
            for n_left in (_count(rank0 < float(TOPK)), _count(rank1 < float(TOPK)), _count(picked)):
                off = off + jnp.abs(n_left - float(TOPK))
    return off


def _retrieve_body(n2_ref, wq_ref, sk_ref, r1_ref, b_ref, a_ref, c_ref, s_s, va_s, vb_s, *, tb):
    q = _dot(n2_ref[...], wq_ref[...]).astype(BF16)
    for h in range(PEER_HEADS):
        qh = q[:, h * PAIR:(h + 1) * PAIR]
        for half in range(2):
            s = _dot_t(sk_ref[h, half], qh)
            for tt in range(tb // LANES):
                s_s[2 * h + half, tt] = s[:, tt * LANES:(tt + 1) * LANES]

    def tile(tt, carry):
        maps = (r1_ref, b_ref, a_ref, c_ref, va_s, vb_s)
        off = _retrieve_tile(s_s, tt, *maps, exact=False)

        @pl.when(jnp.max(off) > 0.0)
        def _():
            _retrieve_tile(s_s, tt, *maps, exact=True)

        return carry

    lax.fori_loop(0, tb // LANES, tile, 0)


def _retrieve(n2, w, tb):
    n = n2.shape[0]
    nt = tb // LANES
    maps = pl.BlockSpec((nt, PEER_HEADS, N_KEYS, LANES), lambda i: (i, 0, 0, 0))
    rows = pl.BlockSpec((nt, PEER_HEADS, N_KEYS // SUBLANES, SUBLANES, LANES), lambda i: (i, 0, 0, 0, 0))
    map_shape = (n // LANES, PEER_HEADS, N_KEYS, LANES)
    row_shape = (n // LANES, PEER_HEADS, N_KEYS // SUBLANES, SUBLANES, LANES)
    return pl.pallas_call(
        functools.partial(_retrieve_body, tb=tb),
        grid=(n // tb,),
        in_specs=[pl.BlockSpec((tb, D_MODEL), lambda i: (i, 0)), _const_spec(w["peer_wq"].shape),
                  _const_spec(w["peer_sk"].shape)],
        out_specs=[maps, maps, rows, rows],
        out_shape=[jax.ShapeDtypeStruct(map_shape, BF16), jax.ShapeDtypeStruct(map_shape, BF16),
                   jax.ShapeDtypeStruct(row_shape, jnp.uint32), jax.ShapeDtypeStruct(row_shape, jnp.uint32)],
        scratch_shapes=[pltpu.VMEM((2 * PEER_HEADS, nt, N_KEYS, LANES), F32), pltpu.VMEM((TOPK, LANES), F32),
                        pltpu.VMEM((TOPK, LANES), F32)],
        compiler_params=_params(("parallel",)),
        name="peer_retrieve",
    )(n2, w["peer_wq"], w["peer_sk"])


def _gelu_tanh(x):
    return 0.5 * x * (1.0 + jnp.tanh(math.sqrt(2.0 / math.pi) * (x + 0.044715 * (x * x * x))))


def _row_tile(ref, tt, h, g, ii):
    row = jnp.broadcast_to(ref[tt, h, g, ii:ii + 1, :], (SUBLANES, LANES))
    packed = pltpu.bitcast(row, BF16)
    return jnp.concatenate([packed] * (N_KEYS // packed.shape[0]), axis=0)


def _dense_body(n2_ref, u_ref, vt_ref, r1_ref, b_ref, a_ref, c_ref, o_ref, acc_s, gate_s, r1_s, b_s, *, tb):
    g = pl.program_id(1)

    @pl.when(g == 0)
    def _():
        acc_s[...] = jnp.zeros(acc_s.shape, F32)
        r1_s[...] = r1_ref[...]
        b_s[...] = b_ref[...]

    for tt in range(tb // LANES):
        lanes = slice(tt * LANES, (tt + 1) * LANES)
        for ii in range(SUBLANES):
            rows = slice(ii * N_KEYS, (ii + 1) * N_KEYS)
            gate = None
            for h in range(PEER_HEADS):
                wgt = b_s[tt, h] * _row_tile(a_ref, tt, h, g, ii)
                term = jnp.where(r1_s[tt, h] < _row_tile(c_ref, tt, h, g, ii), wgt, jnp.zeros_like(wgt))
                gate = term if gate is None else gate + term
            gate_s[rows, lanes] = gate

    hid = _dot_t(u_ref[...], n2_ref[...])
    weighted = gate_s[...] * _gelu_tanh(hid).astype(BF16)
    acc_s[...] += _dot(vt_ref[0], weighted)

    @pl.when(g == pl.num_programs(1) - 1)
    def _():
        o_ref[...] = acc_s[...].T


def _dense(n2, u, vt, r1, b, a, c, tb):
    n = n2.shape[0]
    ec = EXPERT_CHUNK
    ng = N_EXPERTS // ec
    nt = tb // LANES
    once = pl.Buffered(1)
    maps = pl.BlockSpec((nt, PEER_HEADS, N_KEYS, LANES), lambda i, g: (i, 0, 0, 0), pipeline_mode=once)
    rows = pl.BlockSpec((nt, PEER_HEADS, ng, SUBLANES, LANES), lambda i, g: (i, 0, 0, 0, 0), pipeline_mode=once)
    return pl.pallas_call(
        functools.partial(_dense_body, tb=tb),
        grid=(n // tb, ng),
        in_specs=[pl.BlockSpec((tb, D_MODEL), lambda i, g: (i, 0), pipeline_mode=once),
                  pl.BlockSpec((ec, D_MODEL), lambda i, g: (g, 0)),
                  pl.BlockSpec((1, D_MODEL, ec), lambda i, g: (g, 0, 0)),
                  maps, maps, rows, rows],
        out_specs=pl.BlockSpec((tb, D_MODEL), lambda i, g: (i, 0)),
        out_shape=jax.ShapeDtypeStruct((n, D_MODEL), F32),
        scratch_shapes=[pltpu.VMEM((D_MODEL, tb), F32), pltpu.VMEM((ec, tb), BF16),
                        pltpu.VMEM((nt, PEER_HEADS, N_KEYS, LANES), BF16),
                        pltpu.VMEM((nt, PEER_HEADS, N_KEYS, LANES), BF16)],
        compiler_params=_params(("parallel", "arbitrary")),
        name="peer_dense",
    )(n2, u, vt, r1, b, a, c)


def _final_body(h_ref, peer_ref, p_ref, gple_ref, wg_ref, wp_ref, o_ref):
    h = h_ref[...] + peer_ref[...]
    zg = _dot(_rms(h, gple_ref[...]).astype(BF16), wg_ref[...])
    gate = 1.0 / (1.0 + jnp.exp(-zg))
    o_ref[...] = h + gate * _dot(p_ref[...].astype(BF16), wp_ref[...])


def _final(h, peer, p, w, tb):
    n = h.shape[0]
    row = lambda width: pl.BlockSpec((tb, width), lambda i: (i, 0))
    consts = [w["g_ple"], w["w_ple_gate"], w["w_ple_proj"]]
    return pl.pallas_call(
        _final_body,
        grid=(n // tb,),
        in_specs=[row(D_MODEL), row(D_MODEL), row(PLE_DIM)] + [_const_spec(a.shape) for a in consts],
        out_specs=row(D_MODEL),
        out_shape=jax.ShapeDtypeStruct((n, D_MODEL), F32),
        compiler_params=_params(("parallel",)),
        name="final",
    )(h, peer, p, *consts)


def _layer_weights(l, g_mix, w_in, b_f, qn_a, kn_a, qn_b, kn_b, w_up_a, w_up_b, w_out, g_ffn, peer_wq, peer_subkeys,
                   peer_u, peer_v, g_ple, w_ple_gate, w_ple_proj):
    o_f = 3 * W_MIX
    o_b = o_f + N_HEADS
    o_g = o_b + 3 * W_MIX
    wi = w_in[l]
    tile_heads = lambda g: jnp.tile(g[l].astype(F32), N_HEADS)[None, :]
    head_of = jnp.arange(W_MIX) // HEAD_DIM
    sk = peer_subkeys[l].astype(BF16)
    zeros = jnp.zeros_like(sk[:, 0])
    sk_pad = jnp.stack([jnp.concatenate([sk[:, 0], zeros], axis=-1), jnp.concatenate([zeros, sk[:, 1]], axis=-1)],
                       axis=1)
    return {
        "g_mix": g_mix[l][None, :],
        "w_a": wi[:, :o_f].astype(BF16),
        "w_f": jnp.pad(wi[:, o_f:o_b], ((0, 0), (0, LANES - N_HEADS))).astype(BF16),
        "w_b": wi[:, o_b:o_g].astype(BF16),
        "w_g": wi[:, o_g:].astype(BF16),
        "b_f": jnp.pad(b_f[l], (0, LANES - N_HEADS))[None, :],
        "qn_a": tile_heads(qn_a), "kn_a": tile_heads(kn_a), "qn_b": tile_heads(qn_b), "kn_b": tile_heads(kn_b),
        "msum": jnp.where(head_of[:, None] == head_of[None, :], 1.0 / HEAD_DIM, 0.0).astype(BF16),
        "w_up_a": w_up_a[l].astype(BF16), "w_up_b": w_up_b[l].astype(BF16), "w_out": w_out[l].astype(BF16),
        "g_ffn": g_ffn[l][None, :],
        "peer_wq": peer_wq[l].astype(BF16),
        "peer_sk": sk_pad,
        "peer_u": peer_u[l].astype(BF16),
        "peer_vt": peer_v[l].astype(BF16).reshape(-1, EXPERT_CHUNK, D_MODEL).transpose(0, 2, 1),
        "g_ple": g_ple[l][None, :],
        "w_ple_gate": w_ple_gate[l].astype(BF16),
        "w_ple_proj": w_ple_proj[l].astype(BF16),
    }


def _channel(x, ya, yb, ga, gb, p, w, tb, tb_dense):
    h1, n2 = _merge(x, ya, yb, ga, gb, w, tb)
    r1, b, a, c = _retrieve(n2, w, tb_dense)
    peer = _dense(n2, w["peer_u"], w["peer_vt"], r1, b, a, c, tb_dense)
    return _final(h1, peer, p, w, tb)


def _pad_rows(x, rows):
    return jnp.pad(x, ((0, 0), (0, rows - x.shape[1]), (0, 0)))


def kernel(x_prompt, x_sample, cache_a_k, cache_a_v, cache_a_logf, cache_b_k, cache_b_v, p_prompt, p_sample, g_mix, w_in, b_f, qn_a, kn_a, qn_b, kn_b, rel_bias_b, w_up_a, w_up_b, w_out, g_ffn, peer_wq, peer_subkeys, peer_u, peer_v, g_ple, w_ple_gate, w_ple_proj):
    depth = w_in.shape[0]
    bp, tp, _ = x_prompt.shape
    bs, ts, _ = x_sample.shape
    past = cache_a_k.shape[2]
    band_rows = min(BAND_PAST, tp)
    tq = 256
    hp = x_prompt.reshape(bp * tp, D_MODEL)
    hs = x_sample.reshape(bs * ts, D_MODEL)
    outs = [[] for _ in range(10)]
    for l in range(depth):
        w = _layer_weights(l, g_mix, w_in, b_f, qn_a, kn_a, qn_b, kn_b, w_up_a, w_up_b, w_out, g_ffn, peer_wq,
                           peer_subkeys, peer_u, peer_v, g_ple, w_ple_gate, w_ple_proj)
        qa, ka, va, lf, qb, kb, vb, ga, gb = _proj(hp, w, 256)
        as_seq = lambda z: z.reshape(bp, tp, z.shape[-1])
        ya = _fox_prompt(as_seq(qa), as_seq(ka), as_seq(va), as_seq(lf).transpose(0, 2, 1), tq)
        yb = _band_prompt(as_seq(qb), as_seq(kb), as_seq(vb), _band_bias_prompt(rel_bias_b[l], tq), tq)
        hp = _channel(hp, ya.reshape(-1, W_MIX), yb.reshape(-1, W_MIX), ga, gb, p_prompt[l].reshape(-1, PLE_DIM), w,
                      256, 1024)
        heads = lambda z, b_, t_: z.reshape(b_, t_, N_HEADS, HEAD_DIM)
        outs[0].append(heads(ka, bp, tp)); outs[1].append(heads(va, bp, tp)); outs[2].append(as_seq(lf))
        outs[3].append(heads(kb, bp, tp)[:, -band_rows:]); outs[4].append(heads(vb, bp, tp)[:, -band_rows:])
        qa, ka, va, lf, qb, kb, vb, ga, gb = _proj(hs, w, bs * ts)
        as_seq = lambda z: z.reshape(bs, ts, z.shape[-1])
        lft = jnp.concatenate([cache_a_logf[l].astype(F32), as_seq(lf),
                               jnp.zeros((bs, LANES - ts, N_HEADS), F32)], axis=1).transpose(0, 2, 1)
        flat_cache = lambda z: z[l].reshape(bs, z.shape[2], W_MIX)
        ya = _fox_sample(as_seq(qa), flat_cache(cache_a_k), flat_cache(cache_a_v), _pad_rows(as_seq(ka), LANES),
                         _pad_rows(as_seq(va), LANES), lft, ts)
        yb = _band_sample(as_seq(qb), flat_cache(cache_b_k), flat_cache(cache_b_v), _pad_rows(as_seq(kb), LANES),
                          _pad_rows(as_seq(vb), LANES), _band_bias_sample(rel_bias_b[l], ts, cache_b_k.shape[2]))
        hs = _channel(hs, ya.reshape(-1, W_MIX), yb.reshape(-1, W_MIX), ga, gb, p_sample[l].reshape(-1, PLE_DIM), w,
                      bs * ts, bs * ts)
        outs[5].append(heads(ka, bs, ts)); outs[6].append(heads(va, bs, ts)); outs[7].append(as_seq(lf))
        outs[8].append(heads(kb, bs, ts)); outs[9].append(heads(vb, bs, ts))
    return (hp.reshape(bp, tp, D_MODEL), hs.reshape(bs, ts, D_MODEL)) + tuple(jnp.stack(o) for o in outs)
```

```python
import functools
import math

import jax
import jax.numpy as jnp
from jax import lax
from jax.experimental import pallas as pl
from jax.experimental.pallas import tpu as pltpu

F32 = jnp.float32
BF16 = jnp.bfloat16

D_MODEL = 1024
HEAD_DIM = 64
N_HEADS = 8
W_MIX = N_HEADS * HEAD_DIM
PAIR = 2 * HEAD_DIM
N_PAIRS = N_HEADS // 2
CHUNK = 64
BAND_PAST = 8 * CHUNK
MAX_REL = 128
PLE_DIM = 256
PEER_HEADS = 8
N_KEYS = 128
N_EXPERTS = N_KEYS * N_KEYS
TOPK = 16
RMS_EPS = 1e-6
ATT_SCALE = HEAD_DIM ** -0.5
NEG = -1e30
RANK_SENTINEL = 2.0 ** 100
LANES = 128
SUBLANES = 8
EXPERT_CHUNK = SUBLANES * N_KEYS
VMEM_LIMIT = 56 * 1024 * 1024

N_CAND = 16 + 7 * 8 + 8


def _params(sem, vmem=VMEM_LIMIT):
    return pltpu.CompilerParams(dimension_semantics=sem, vmem_limit_bytes=vmem)


def _rms(x, g):
    return x * lax.rsqrt(jnp.mean(x * x, axis=-1, keepdims=True) + RMS_EPS) * g


def _dot(a, b):
    return jnp.dot(a, b, preferred_element_type=F32)


def _dot_t(a, b):
    return lax.dot_general(a, b, (((1,), (1,)), ((), ())), preferred_element_type=F32)


def _const_spec(shape):
    nd = len(shape)
    return pl.BlockSpec(shape, lambda *_: (0,) * nd)


def _proj_body(x_ref, g_ref, wa_ref, wf_ref, wb_ref, wg_ref, bf_ref, qna_ref, kna_ref, qnb_ref, knb_ref, msum_ref,
               qa_ref, ka_ref, va_ref, lf_ref, qb_ref, kb_ref, vb_ref, ga_ref, gb_ref):
    n1 = _rms(x_ref[...], g_ref[...]).astype(BF16)

    def head_norm(z, gain):
        ms = _dot((z * z).astype(BF16), msum_ref[...])
        return z * lax.rsqrt(ms + RMS_EPS) * gain

    za = _dot(n1, wa_ref[...])
    qa_ref[...] = (head_norm(za[:, :W_MIX], qna_ref[...]) * ATT_SCALE).astype(BF16)
    ka_ref[...] = head_norm(za[:, W_MIX:2 * W_MIX], kna_ref[...])
    va_ref[...] = za[:, 2 * W_MIX:]
    zb = _dot(n1, wb_ref[...])
    qb_ref[...] = (head_norm(zb[:, :W_MIX], qnb_ref[...]) * ATT_SCALE).astype(BF16)
    kb_ref[...] = head_norm(zb[:, W_MIX:2 * W_MIX], knb_ref[...])
    vb_ref[...] = zb[:, 2 * W_MIX:]
    fl = _dot(n1, wf_ref[...]) + bf_ref[...]
    ls = jnp.minimum(fl, 0.0) - jnp.log1p(jnp.exp(-jnp.abs(fl)))
    lf_ref[...] = ls[:, :N_HEADS]
    zg = _dot(n1, wg_ref[...])
    sg = 1.0 / (1.0 + jnp.exp(-zg))
    ga_ref[...] = sg[:, :D_MODEL].astype(BF16)
    gb_ref[...] = sg[:, D_MODEL:].astype(BF16)


def _proj(x, w, tb):
    n = x.shape[0]
    row = lambda width: pl.BlockSpec((tb, width), lambda i: (i, 0))
    ins = [x, w["g_mix"], w["w_a"], w["w_f"], w["w_b"], w["w_g"], w["b_f"], w["qn_a"], w["kn_a"], w["qn_b"],
           w["kn_b"], w["msum"]]
    in_specs = [row(D_MODEL)] + [_const_spec(a.shape) for a in ins[1:]]
    widths = [(W_MIX, BF16), (W_MIX, F32), (W_MIX, F32), (N_HEADS, F32), (W_MIX, BF16), (W_MIX, F32), (W_MIX, F32),
              (D_MODEL, BF16), (D_MODEL, BF16)]
    return pl.pallas_call(
        _proj_body,
        grid=(n // tb,),
        in_specs=in_specs,
        out_specs=[row(wd) for wd, _ in widths],
        out_shape=[jax.ShapeDtypeStruct((n, wd), dt) for wd, dt in widths],
        compiler_params=_params(("parallel",)),
        name="proj",
    )(*ins)


def _cumsum_lanes(x):
    n = x.shape[-1]
    lane = lax.broadcasted_iota(jnp.int32, x.shape, x.ndim - 1)
    s = 1
    while s < n:
        x = x + jnp.where(lane >= s, pltpu.roll(x, s, axis=x.ndim - 1), 0.0)
        s *= 2
    return x


def _head_of_pair(x, hh):
    lane = lax.broadcasted_iota(jnp.int32, (1, PAIR), 1)
    keep = (lane < HEAD_DIM) if hh == 0 else (lane >= HEAD_DIM)
    return jnp.where(keep, x, jnp.zeros_like(x))


def _merge_pair(o0, o1):
    lane = lax.broadcasted_iota(jnp.int32, (1, PAIR), 1)
    return jnp.where(lane < HEAD_DIM, o0, o1)


def _fox_body(q_ref, k_ref, v_ref, lft_ref, o_ref, kb_s, vb_s, c_s, *, tq, nq):
    hp = pl.program_id(1)
    qi = pl.program_id(2)

    @pl.when(qi == 0)
    def _():
        kb_s[...] = k_ref[0].astype(BF16)
        vb_s[...] = v_ref[0].astype(BF16)
        c_s[...] = _cumsum_lanes(lft_ref[0])

    q = q_ref[0]
    q2 = jnp.concatenate([_head_of_pair(q, 0), _head_of_pair(q, 1)], axis=0)
    row = lax.broadcasted_iota(jnp.int32, (tq, tq), 0)
    col = lax.broadcasted_iota(jnp.int32, (tq, tq), 1)

    def tile(n_blocks):
        past = (n_blocks - 1) * tq
        keys = n_blocks * tq
        s2 = _dot_t(q2, kb_s[:keys, :])
        probs, sums = [], []
        for hh in range(2):
            s = s2[hh * tq:(hh + 1) * tq] - c_s[pl.ds(2 * hp + hh, 1), :keys]
            s_diag = jnp.where(col <= row, s[:, past:], -jnp.inf)
            m = jnp.max(s_diag, axis=-1, keepdims=True)
            if past:
                m = jnp.maximum(m, jnp.max(s[:, :past], axis=-1, keepdims=True))
            p_diag = jnp.exp(s_diag - m)
            l = jnp.sum(p_diag, axis=-1, keepdims=True)
            if past:
                p_past = jnp.exp(s[:, :past] - m)
                l = l + jnp.sum(p_past, axis=-1, keepdims=True)
                p_diag = jnp.concatenate([p_past, p_diag], axis=1)
            probs.append(p_diag.astype(BF16))
            sums.append(l)
        o2 = _dot(jnp.concatenate(probs, axis=0), vb_s[:keys, :])
        o_ref[0] = _merge_pair(o2[:tq] / sums[0], o2[tq:] / sums[1]).astype(o_ref.dtype)

    for n_blocks in range(1, nq + 1):
        pl.when(qi == n_blocks - 1)(functools.partial(tile, n_blocks))


def _fox_prompt(q, k, v, lft, tq):
    b, t, _ = q.shape
    nq = t // tq
    return pl.pallas_call(
        functools.partial(_fox_body, tq=tq, nq=nq),
        grid=(b, N_PAIRS, nq),
        in_specs=[
            pl.BlockSpec((1, tq, PAIR), lambda i, p, j: (i, j, p)),
            pl.BlockSpec((1, t, PAIR), lambda i, p, j: (i, 0, p)),
            pl.BlockSpec((1, t, PAIR), lambda i, p, j: (i, 0, p)),
            pl.BlockSpec((1, N_HEADS, t), lambda i, p, j: (i, 0, 0)),
        ],
        out_specs=pl.BlockSpec((1, tq, PAIR), lambda i, p, j: (i, j, p)),
        out_shape=jax.ShapeDtypeStruct((b, t, W_MIX), BF16),
        scratch_shapes=[pltpu.VMEM((t, PAIR), BF16), pltpu.VMEM((t, PAIR), BF16), pltpu.VMEM((N_HEADS, t), F32)],
        compiler_params=_params(("parallel", "parallel", "arbitrary")),
        name="fox_prompt",
    )(q, k, v, lft)


def _band_body(q_ref, k_ref, v_ref, bias_ref, o_ref, kp_s, vp_s, *, tq):
    qi = pl.program_id(1)
    win = tq + BAND_PAST

    @pl.when(qi == 0)
    def _():
        zeros = jnp.zeros((BAND_PAST, W_MIX), BF16)
        kp_s[:BAND_PAST, :] = zeros
        vp_s[:BAND_PAST, :] = zeros
        kp_s[BAND_PAST:, :] = k_ref[0].astype(BF16)
        vp_s[BAND_PAST:, :] = v_ref[0].astype(BF16)

    off = pl.multiple_of(qi * tq, tq)
    exists = lax.broadcasted_iota(jnp.int32, (1, win), 1) >= BAND_PAST - qi * tq
    for pair in range(N_PAIRS):
        lanes = slice(pair * PAIR, (pair + 1) * PAIR)
        kw = kp_s[pl.ds(off, win), lanes]
        vw = vp_s[pl.ds(off, win), lanes]
        q = q_ref[0, :, lanes]
        outs = []
        for hh in range(2):
            s = _dot_t(_head_of_pair(q, hh), kw) + bias_ref[2 * pair + hh]
            s = jnp.where(exists, s, NEG)
            m = jnp.max(s, axis=-1, keepdims=True)
            p = jnp.exp(s - m)
            l = jnp.sum(p, axis=-1, keepdims=True)
            outs.append(_dot(p.astype(BF16), vw) / l)
        o_ref[0, :, lanes] = _merge_pair(outs[0], outs[1]).astype(o_ref.dtype)


def _band_prompt(q, k, v, bias, tq):
    b, t, _ = q.shape
    win = tq + BAND_PAST
    return pl.pallas_call(
        functools.partial(_band_body, tq=tq),
        grid=(b, t // tq),
        in_specs=[
            pl.BlockSpec((1, tq, W_MIX), lambda i, j: (i, j, 0)),
            pl.BlockSpec((1, t, W_MIX), lambda i, j: (i, 0, 0)),
            pl.BlockSpec((1, t, W_MIX), lambda i, j: (i, 0, 0)),
            _const_spec((N_HEADS, tq, win)),
        ],
        out_specs=pl.BlockSpec((1, tq, W_MIX), lambda i, j: (i, j, 0)),
        out_shape=jax.ShapeDtypeStruct((b, t, W_MIX), BF16),
        scratch_shapes=[pltpu.VMEM((t + BAND_PAST, W_MIX), BF16), pltpu.VMEM((t + BAND_PAST, W_MIX), BF16)],
        compiler_params=_params(("parallel", "arbitrary")),
        name="band_prompt",
    )(q, k, v, bias)


def _toeplitz(w, n, m):
    heads, span = w.shape
    hankel = jnp.tile(w, (1, n + 1))[:, :n * (span + 1)].reshape(heads, n, span + 1)[:, :, :m]
    return hankel[:, ::-1, :]


def _band_bias_prompt(rel_bias, tq):
    win = tq + BAND_PAST
    rel = jnp.arange(tq + win - 1) - (tq - 1) - BAND_PAST
    table = _toeplitz(rel_bias[:, jnp.clip(rel, -MAX_REL, MAX_REL) + MAX_REL].astype(F32), tq, win)
    ii = jnp.arange(tq)[:, None]
    jj = jnp.arange(win)[None, :]
    lo = (ii // CHUNK) * CHUNK
    in_band = (jj >= lo) & (jj < lo + BAND_PAST + CHUNK)
    return jnp.where(in_band[None], table, NEG)


def _fox_sample_body(q_ref, kc_ref, vc_ref, kn_ref, vn_ref, lft_ref, o_ref, *, past, n_new):
    hp = pl.program_id(1)
    c = _cumsum_lanes(lft_ref[0])
    kc = kc_ref[0].astype(BF16)
    vc = vc_ref[0].astype(BF16)
    kn = kn_ref[0].astype(BF16)
    vn = vn_ref[0].astype(BF16)
    q = q_ref[0]
    nq = q.shape[0]
    row = lax.broadcasted_iota(jnp.int32, (nq, LANES), 0)
    col = lax.broadcasted_iota(jnp.int32, (nq, LANES), 1)
    outs = []
    for hh in range(2):
        qh = _head_of_pair(q, hh)
        sel = lax.broadcasted_iota(jnp.int32, (N_HEADS, 1), 0) == 2 * hp + hh
        crow = jnp.sum(jnp.where(sel, c, 0.0), axis=0, keepdims=True)
        sc = _dot_t(qh, kc) - crow[:, :past]
        sn = _dot_t(qh, kn) - crow[:, past:]
        sn = jnp.where((col <= row) & (col < n_new), sn, -jnp.inf)
        m = jnp.maximum(jnp.max(sc, axis=-1, keepdims=True), jnp.max(sn, axis=-1, keepdims=True))
        pc = jnp.exp(sc - m)
        pn = jnp.exp(sn - m)
        l = jnp.sum(pc, axis=-1, keepdims=True) + jnp.sum(pn, axis=-1, keepdims=True)
        outs.append((_dot(pc.astype(BF16), vc) + _dot(pn.astype(BF16), vn)) / l)
    o_ref[0] = _merge_pair(outs[0], outs[1]).astype(o_ref.dtype)


def _fox_sample(q, kc, vc, kn, vn, lft, n_new):
    b, nq, _ = q.shape
    past = kc.shape[1]
    pair_spec = lambda rows: pl.BlockSpec((1, rows, PAIR), lambda i, p: (i, 0, p))
    return pl.pallas_call(
        functools.partial(_fox_sample_body, past=past, n_new=n_new),
        grid=(b, N_PAIRS),
        in_specs=[pair_spec(nq), pair_spec(past), pair_spec(past), pair_spec(LANES), pair_spec(LANES),
                  pl.BlockSpec((1, N_HEADS, past + LANES), lambda i, p: (i, 0, 0))],
        out_specs=pair_spec(nq),
        out_shape=jax.ShapeDtypeStruct((b, nq, W_MIX), BF16),
        compiler_params=_params(("parallel", "parallel")),
        name="fox_sample",
    )(q, kc, vc, kn, vn, lft)


def _band_sample_body(q_ref, kc_ref, vc_ref, kn_ref, vn_ref, bias_ref, o_ref, *, past):
    kc = kc_ref[0].astype(BF16)
    vc = vc_ref[0].astype(BF16)
    kn = kn_ref[0].astype(BF16)
    vn = vn_ref[0].astype(BF16)
    q = q_ref[0]
    outs = []
    for hh in range(2):
        qh = _head_of_pair(q, hh)
        bias = bias_ref[hh]
        sc = _dot_t(qh, kc) + bias[:, :past]
        sn = _dot_t(qh, kn) + bias[:, past:]
        m = jnp.maximum(jnp.max(sc, axis=-1, keepdims=True), jnp.max(sn, axis=-1, keepdims=True))
        pc = jnp.exp(sc - m)
        pn = jnp.exp(sn - m)
        l = jnp.sum(pc, axis=-1, keepdims=True) + jnp.sum(pn, axis=-1, keepdims=True)
        outs.append((_dot(pc.astype(BF16), vc) + _dot(pn.astype(BF16), vn)) / l)
    o_ref[0] = _merge_pair(outs[0], outs[1]).astype(o_ref.dtype)


def _band_sample(q, kc, vc, kn, vn, bias):
    b, nq, _ = q.shape
    past = kc.shape[1]
    pair_spec = lambda rows: pl.BlockSpec((1, rows, PAIR), lambda i, p: (i, 0, p))
    return pl.pallas_call(
        functools.partial(_band_sample_body, past=past),
        grid=(b, N_PAIRS),
        in_specs=[pair_spec(nq), pair_spec(past), pair_spec(past), pair_spec(LANES), pair_spec(LANES),
                  pl.BlockSpec((2, nq, past + LANES), lambda i, p: (p, 0, 0))],
        out_specs=pair_spec(nq),
        out_shape=jax.ShapeDtypeStruct((b, nq, W_MIX), BF16),
        compiler_params=_params(("parallel", "parallel")),
        name="band_sample",
    )(q, kc, vc, kn, vn, bias)


def _band_bias_sample(rel_bias, n_new, past):
    ii = jnp.arange(n_new)[:, None]
    jj = jnp.arange(past + LANES)[None, :]
    table = rel_bias[:, jnp.clip(jj - past - ii, -MAX_REL, MAX_REL) + MAX_REL].astype(F32)
    return jnp.where((jj < past + n_new)[None], table, NEG)


def _merge_body(x_ref, ya_ref, yb_ref, ga_ref, gb_ref, wua_ref, wub_ref, wo_ref, gffn_ref, h_ref, n2_ref):
    merged = (ga_ref[...].astype(F32) * _dot(ya_ref[...], wua_ref[...])
              + gb_ref[...].astype(F32) * _dot(yb_ref[...], wub_ref[...]))
    h = x_ref[...] + _dot(merged.astype(BF16), wo_ref[...])
    h_ref[...] = h
    n2_ref[...] = _rms(h, gffn_ref[...]).astype(BF16)


def _merge(x, ya, yb, ga, gb, w, tb):
    n = x.shape[0]
    row = lambda width: pl.BlockSpec((tb, width), lambda i: (i, 0))
    consts = [w["w_up_a"], w["w_up_b"], w["w_out"], w["g_ffn"]]
    return pl.pallas_call(
        _merge_body,
        grid=(n // tb,),
        in_specs=[row(D_MODEL), row(W_MIX), row(W_MIX), row(D_MODEL), row(D_MODEL)]
        + [_const_spec(a.shape) for a in consts],
        out_specs=[row(D_MODEL), row(D_MODEL)],
        out_shape=[jax.ShapeDtypeStruct((n, D_MODEL), F32), jax.ShapeDtypeStruct((n, D_MODEL), BF16)],
        compiler_params=_params(("parallel",)),
        name="merge",
    )(x, ya, yb, ga, gb, *consts)


def _leave(work, exact):
    hit = work == jnp.max(work, axis=0, keepdims=True)
    if exact:
        idx = lax.broadcasted_iota(jnp.int32, work.shape, 0)
        hit = idx == jnp.min(jnp.where(hit, idx, work.shape[0]), axis=0, keepdims=True)
    return hit


def _top16(s, vals_ref, exact):
    work = s
    for r in range(TOPK):
        vals_ref[r:r + 1, :] = jnp.max(work, axis=0, keepdims=True)
        work = jnp.where(_leave(work, exact), -RANK_SENTINEL * (1.0 + r / 32.0), work)
    return jnp.where(work <= -RANK_SENTINEL, work * (-32.0 / RANK_SENTINEL) - 32.0, float(TOPK))


def _count(mask):
    return jnp.sum(mask.astype(F32), axis=0, keepdims=True)


def _retrieve_tile(s_s, tt, r1_ref, b_ref, a_ref, c_ref, va_s, vb_s, exact):
    off = jnp.zeros((1, LANES), F32)
    for h in range(PEER_HEADS):
        s0 = s_s[2 * h, tt]
        s1 = s_s[2 * h + 1, tt]
        rank0 = _top16(s0, va_s, exact)
        rank1 = _top16(s1, vb_s, exact)
        va = va_s[...]
        vb = vb_s[...]
        cand = jnp.concatenate([va[0:1] + vb] + [va[k:k + 1] + vb[0:8] for k in range(1, 8)] + [va[8:16] + vb[0:1]],
                               axis=0)
        work = cand
        picked = jnp.zeros(cand.shape, jnp.bool_)
        for _ in range(TOPK):
            hit = _leave(work, exact)
            picked = picked | hit
            work = jnp.where(hit, -jnp.inf, work)
        top = va[0:1] + vb[0:1]
        z = jnp.sum(jnp.where(picked, jnp.exp(cand - top), 0.0), axis=0, keepdims=True)
        pickf = picked.astype(F32)
        counts = [jnp.sum(pickf[0:16], axis=0, keepdims=True)]
        counts += [jnp.sum(pickf[8 + 8 * k:16 + 8 * k], axis=0, keepdims=True) for k in range(1, 8)]
        counts += [pickf[72 + k:73 + k] for k in range(8)]
        c = jnp.zeros(s0.shape, F32)
        for k in range(TOPK):
            c = jnp.where(rank0 == float(k), counts[k], c)
        a = jnp.where(rank0 < float(TOPK), jnp.exp(s0 - va[0:1]) * (0.5 / z), 0.0)
        b = jnp.where(rank1 < float(TOPK), jnp.exp(s1 - vb[0:1]), 0.0)
        r1_ref[tt, h] = rank1.astype(BF16)
        b_ref[tt, h] = b.astype(BF16)
        for grp in range(N_KEYS // SUBLANES):
            rows = slice(grp * SUBLANES, (grp + 1) * SUBLANES)
            a_ref[tt, h, grp] = a[rows]
            c_ref[tt, h, grp] = c[rows]
        if not exact:
            for n_left in (_count(rank0 < float(TOPK)), _count(rank1 < float(TOPK)), _count(picked)):
                off = off + jnp.abs(n_left - float(TOPK))
    return off


def _retrieve_body(n2_ref, wq_ref, sk_ref, r1_ref, b_ref, a_ref, c_ref, s_s, va_s, vb_s, *, tb):
    q = _dot(n2_ref[...], wq_ref[...]).astype(BF16)
    for h in range(PEER_HEADS):
        qh = q[:, h * PAIR:(h + 1) * PAIR]
        for half in range(2):
            s = _dot_t(sk_ref[h, half], qh)
            for tt in range(tb // LANES):
                s_s[2 * h + half, tt] = s[:, tt * LANES:(tt + 1) * LANES]

    def tile(tt, carry):
        maps = (r1_ref, b_ref, a_ref, c_ref, va_s, vb_s)
        off = _retrieve_tile(s_s, tt, *maps, exact=False)

        @pl.when(jnp.max(off) > 0.0)
        def _():
            _retrieve_tile(s_s, tt, *maps, exact=True)

        return carry

    lax.fori_loop(0, tb // LANES, tile, 0)


def _retrieve(n2, w, tb):
    n = n2.shape[0]
    nt = tb // LANES
    maps = pl.BlockSpec((nt, PEER_HEADS, N_KEYS, LANES), lambda i: (i, 0, 0, 0))
    rows = pl.BlockSpec((nt, PEER_HEADS, N_KEYS // SUBLANES, SUBLANES, LANES), lambda i: (i, 0, 0, 0, 0))
    map_shape = (n // LANES, PEER_HEADS, N_KEYS, LANES)
    row_shape = (n // LANES, PEER_HEADS, N_KEYS // SUBLANES, SUBLANES, LANES)
    return pl.pallas_call(
        functools.partial(_retrieve_body, tb=tb),
        grid=(n // tb,),
        in_specs=[pl.BlockSpec((tb, D_MODEL), lambda i: (i, 0)), _const_spec(w["peer_wq"].shape),
                  _const_spec(w["peer_sk"].shape)],
        out_specs=[maps, maps, rows, rows],
        out_shape=[jax.ShapeDtypeStruct(map_shape, BF16), jax.ShapeDtypeStruct(map_shape, BF16),
                   jax.ShapeDtypeStruct(row_shape, F32), jax.ShapeDtypeStruct(row_shape, F32)],
        scratch_shapes=[pltpu.VMEM((2 * PEER_HEADS, nt, N_KEYS, LANES), F32), pltpu.VMEM((TOPK, LANES), F32),
                        pltpu.VMEM((TOPK, LANES), F32)],
        compiler_params=_params(("parallel",)),
        name="peer_retrieve",
    )(n2, w["peer_wq"], w["peer_sk"])


def _twice_gelu(x):
    k0 = math.sqrt(2.0 / math.pi)
    return x * (1.0 + jnp.tanh(x * (k0 + (k0 * 0.044715) * (x * x))))


def _dense_body(n2_ref, u_ref, vt_ref, r1_ref, b_ref, a_ref, c_ref, o_ref, acc_s, w_s, r1_s, b_s, *, tb):
    g = pl.program_id(1)

    @pl.when(g == 0)
    def _():
        acc_s[...] = jnp.zeros(acc_s.shape, F32)
        r1_s[...] = r1_ref[...].astype(F32)
        b_s[...] = b_ref[...].astype(F32)

    hid = _dot_t(u_ref[...], n2_ref[...])
    for tt in range(tb // LANES):
        lanes = slice(tt * LANES, (tt + 1) * LANES)
        for ii in range(SUBLANES):
            rows = slice(ii * N_KEYS, (ii + 1) * N_KEYS)
            gate = None
            for h in range(PEER_HEADS):
                wgt = b_s[tt, h] * a_ref[tt, h, g, ii:ii + 1, :]
                term = jnp.where(r1_s[tt, h] < c_ref[tt, h, g, ii:ii + 1, :], wgt, 0.0)
                gate = term if gate is None else gate + term
            w_s[rows, lanes] = (gate * _twice_gelu(hid[rows, lanes])).astype(BF16)
    acc_s[...] += _dot(vt_ref[0], w_s[...])

    @pl.when(g == pl.num_programs(1) - 1)
    def _():
        o_ref[...] = acc_s[...].T


def _dense(n2, u, vt, r1, b, a, c, tb):
    n = n2.shape[0]
    ec = EXPERT_CHUNK
    ng = N_EXPERTS // ec
    nt = tb // LANES
    once = pl.Buffered(1)
    maps = pl.BlockSpec((nt, PEER_HEADS, N_KEYS, LANES), lambda i, g: (i, 0, 0, 0), pipeline_mode=once)
    rows = pl.BlockSpec((nt, PEER_HEADS, ng, SUBLANES, LANES), lambda i, g: (i, 0, 0, 0, 0), pipeline_mode=once)
    return pl.pallas_call(
        functools.partial(_dense_body, tb=tb),
        grid=(n // tb, ng),
        in_specs=[pl.BlockSpec((tb, D_MODEL), lambda i, g: (i, 0), pipeline_mode=once),
                  pl.BlockSpec((ec, D_MODEL), lambda i, g: (g, 0)),
                  pl.BlockSpec((1, D_MODEL, ec), lambda i, g: (g, 0, 0)),
                  maps, maps, rows, rows],
        out_specs=pl.BlockSpec((tb, D_MODEL), lambda i, g: (i, 0)),
        out_shape=jax.ShapeDtypeStruct((n, D_MODEL), F32),
        scratch_shapes=[pltpu.VMEM((D_MODEL, tb), F32), pltpu.VMEM((ec, tb), BF16),
                        pltpu.VMEM((nt, PEER_HEADS, N_KEYS, LANES), F32),
                        pltpu.VMEM((nt, PEER_HEADS, N_KEYS, LANES), F32)],
        compiler_params=_params(("parallel", "arbitrary")),
        name="peer_dense",
    )(n2, u, vt, r1, b, a, c)


def _final_body(h_ref, peer_ref, p_ref, gple_ref, wg_ref, wp_ref, o_ref):
    h = h_ref[...] + peer_ref[...]
    zg = _dot(_rms(h, gple_ref[...]).astype(BF16), wg_ref[...])
    gate = 1.0 / (1.0 + jnp.exp(-zg))
    o_ref[...] = h + gate * _dot(p_ref[...].astype(BF16), wp_ref[...])


def _final(h, peer, p, w, tb):
    n = h.shape[0]
    row = lambda width: pl.BlockSpec((tb, width), lambda i: (i, 0))
    consts = [w["g_ple"], w["w_ple_gate"], w["w_ple_proj"]]
    return pl.pallas_call(
        _final_body,
        grid=(n // tb,),
        in_specs=[row(D_MODEL), row(D_MODEL), row(PLE_DIM)] + [_const_spec(a.shape) for a in consts],
        out_specs=row(D_MODEL),
        out_shape=jax.ShapeDtypeStruct((n, D_MODEL), F32),
        compiler_params=_params(("parallel",)),
        name="final",
    )(h, peer, p, *consts)


def _layer_weights(l, g_mix, w_in, b_f, qn_a, kn_a, qn_b, kn_b, w_up_a, w_up_b, w_out, g_ffn, peer_wq, peer_subkeys,
                   peer_u, peer_v, g_ple, w_ple_gate, w_ple_proj):
    o_f = 3 * W_MIX
    o_b = o_f + N_HEADS
    o_g = o_b + 3 * W_MIX
    wi = w_in[l]
    tile_heads = lambda g: jnp.tile(g[l].astype(F32), N_HEADS)[None, :]
    head_of = jnp.arange(W_MIX) // HEAD_DIM
    sk = peer_subkeys[l].astype(BF16)
    zeros = jnp.zeros_like(sk[:, 0])
    sk_pad = jnp.stack([jnp.concatenate([sk[:, 0], zeros], axis=-1), jnp.concatenate([zeros, sk[:, 1]], axis=-1)],
                       axis=1)
    return {
        "g_mix": g_mix[l][None, :],
        "w_a": wi[:, :o_f].astype(BF16),
        "w_f": jnp.pad(wi[:, o_f:o_b], ((0, 0), (0, LANES - N_HEADS))).astype(BF16),
        "w_b": wi[:, o_b:o_g].astype(BF16),
        "w_g": wi[:, o_g:].astype(BF16),
        "b_f": jnp.pad(b_f[l], (0, LANES - N_HEADS))[None, :],
        "qn_a": tile_heads(qn_a), "kn_a": tile_heads(kn_a), "qn_b": tile_heads(qn_b), "kn_b": tile_heads(kn_b),
        "msum": jnp.where(head_of[:, None] == head_of[None, :], 1.0 / HEAD_DIM, 0.0).astype(BF16),
        "w_up_a": w_up_a[l].astype(BF16), "w_up_b": w_up_b[l].astype(BF16), "w_out": w_out[l].astype(BF16),
        "g_ffn": g_ffn[l][None, :],
        "peer_wq": peer_wq[l].astype(BF16),
        "peer_sk": sk_pad,
        "peer_u": peer_u[l].astype(BF16),
        "peer_vt": peer_v[l].astype(BF16).reshape(-1, EXPERT_CHUNK, D_MODEL).transpose(0, 2, 1),
        "g_ple": g_ple[l][None, :],
        "w_ple_gate": w_ple_gate[l].astype(BF16),
        "w_ple_proj": w_ple_proj[l].astype(BF16),
    }


def _channel(x, ya, yb, ga, gb, p, w, tb, tb_dense):
    h1, n2 = _merge(x, ya, yb, ga, gb, w, tb)
    r1, b, a, c = _retrieve(n2, w, tb_dense)
    peer = _dense(n2, w["peer_u"], w["peer_vt"], r1, b, a, c, tb_dense)
    return _final(h1, peer, p, w, tb)


def _pad_rows(x, rows):
    return jnp.pad(x, ((0, 0), (0, rows - x.shape[1]), (0, 0)))


def kernel(x_prompt, x_sample, cache_a_k, cache_a_v, cache_a_logf, cache_b_k, cache_b_v, p_prompt, p_sample, g_mix, w_in, b_f, qn_a, kn_a, qn_b, kn_b, rel_bias_b, w_up_a, w_up_b, w_out, g_ffn, peer_wq, peer_subkeys, peer_u, peer_v, g_ple, w_ple_gate, w_ple_proj):
    depth = w_in.shape[0]
    bp, tp, _ = x_prompt.shape
    bs, ts, _ = x_sample.shape
    past = cache_a_k.shape[2]
    band_rows = min(BAND_PAST, tp)
    tq = 256
    hp = x_prompt.reshape(bp * tp, D_MODEL)
    hs = x_sample.reshape(bs * ts, D_MODEL)
    outs = [[] for _ in range(10)]
    for l in range(depth):
        w = _layer_weights(l, g_mix, w_in, b_f, qn_a, kn_a, qn_b, kn_b, w_up_a, w_up_b, w_out, g_ffn, peer_wq,
                           peer_subkeys, peer_u, peer_v, g_ple, w_ple_gate, w_ple_proj)
        qa, ka, va, lf, qb, kb, vb, ga, gb = _proj(hp, w, 256)
        as_seq = lambda z: z.reshape(bp, tp, z.shape[-1])
        ya = _fox_prompt(as_seq(qa), as_seq(ka), as_seq(va), as_seq(lf).transpose(0, 2, 1), tq)
        yb = _band_prompt(as_seq(qb), as_seq(kb), as_seq(vb), _band_bias_prompt(rel_bias_b[l], tq), tq)
        hp = _channel(hp, ya.reshape(-1, W_MIX), yb.reshape(-1, W_MIX), ga, gb, p_prompt[l].reshape(-1, PLE_DIM), w,
                      256, 1024)
        heads = lambda z, b_, t_: z.reshape(b_, t_, N_HEADS, HEAD_DIM)
        outs[0].append(heads(ka, bp, tp)); outs[1].append(heads(va, bp, tp)); outs[2].append(as_seq(lf))
        outs[3].append(heads(kb, bp, tp)[:, -band_rows:]); outs[4].append(heads(vb, bp, tp)[:, -band_rows:])
        qa, ka, va, lf, qb, kb, vb, ga, gb = _proj(hs, w, bs * ts)
        as_seq = lambda z: z.reshape(bs, ts, z.shape[-1])
        lft = jnp.concatenate([cache_a_logf[l].astype(F32), as_seq(lf),
                               jnp.zeros((bs, LANES - ts, N_HEADS), F32)], axis=1).transpose(0, 2, 1)
        flat_cache = lambda z: z[l].reshape(bs, z.shape[2], W_MIX)
        ya = _fox_sample(as_seq(qa), flat_cache(cache_a_k), flat_cache(cache_a_v), _pad_rows(as_seq(ka), LANES),
                         _pad_rows(as_seq(va), LANES), lft, ts)
        yb = _band_sample(as_seq(qb), flat_cache(cache_b_k), flat_cache(cache_b_v), _pad_rows(as_seq(kb), LANES),
                          _pad_rows(as_seq(vb), LANES), _band_bias_sample(rel_bias_b[l], ts, cache_b_k.shape[2]))
        hs = _channel(hs, ya.reshape(-1, W_MIX), yb.reshape(-1, W_MIX), ga, gb, p_sample[l].reshape(-1, PLE_DIM), w,
                      bs * ts, bs * ts)
        outs[5].append(heads(ka, bs, ts)); outs[6].append(heads(va, bs, ts)); outs[7].append(as_seq(lf))
        outs[8].append(heads(kb, bs, ts)); outs[9].append(heads(vb, bs, ts))
    return (hp.reshape(bp, tp, D_MODEL), hs.reshape(bs, ts, D_MODEL)) + tuple(jnp.stack(o) for o in outs)
```

```python
import functools
import math

import jax
import jax.numpy as jnp
from jax import lax
from jax.experimental import pallas as pl
from jax.experimental.pallas import tpu as pltpu

F32 = jnp.float32
BF16 = jnp.bfloat16

D_MODEL = 1024
HEAD_DIM = 64
N_HEADS = 8
W_MIX = N_HEADS * HEAD_DIM
PAIR = 2 * HEAD_DIM
N_PAIRS = N_HEADS // 2
CHUNK = 64
BAND_PAST = 8 * CHUNK
MAX_REL = 128
PLE_DIM = 256
PEER_HEADS = 8
N_KEYS = 128
N_EXPERTS = N_KEYS * N_KEYS
TOPK = 16
RMS_EPS = 1e-6
ATT_SCALE = HEAD_DIM ** -0.5
NEG = -1e30
RANK_SENTINEL = 2.0 ** 100
LANES = 128
SUBLANES = 8
EXPERT_CHUNK = SUBLANES * N_KEYS
VMEM_LIMIT = 56 * 1024 * 1024

N_CAND = 16 + 7 * 8 + 8


def _params(sem, vmem=VMEM_LIMIT):
    return pltpu.CompilerParams(dimension_semantics=sem, vmem_limit_bytes=vmem)


def _rms(x, g):
    return x * lax.rsqrt(jnp.mean(x * x, axis=-1, keepdims=True) + RMS_EPS) * g


def _dot(a, b):
    return jnp.dot(a, b, preferred_element_type=F32)


def _dot_t(a, b):
    return lax.dot_general(a, b, (((1,), (1,)), ((), ())), preferred_element_type=F32)


def _const_spec(shape):
    nd = len(shape)
    return pl.BlockSpec(shape, lambda *_: (0,) * nd)


def _proj_body(x_ref, g_ref, wa_ref, wf_ref, wb_ref, wg_ref, bf_ref, qna_ref, kna_ref, qnb_ref, knb_ref, msum_ref,
               qa_ref, ka_ref, va_ref, lf_ref, qb_ref, kb_ref, vb_ref, ga_ref, gb_ref):
    n1 = _rms(x_ref[...], g_ref[...]).astype(BF16)

    def head_norm(z, gain):
        ms = _dot((z * z).astype(BF16), msum_ref[...])
        return z * lax.rsqrt(ms + RMS_EPS) * gain

    za = _dot(n1, wa_ref[...])
    qa_ref[...] = (head_norm(za[:, :W_MIX], qna_ref[...]) * ATT_SCALE).astype(BF16)
    ka_ref[...] = head_norm(za[:, W_MIX:2 * W_MIX], kna_ref[...])
    va_ref[...] = za[:, 2 * W_MIX:]
    zb = _dot(n1, wb_ref[...])
    qb_ref[...] = (head_norm(zb[:, :W_MIX], qnb_ref[...]) * ATT_SCALE).astype(BF16)
    kb_ref[...] = head_norm(zb[:, W_MIX:2 * W_MIX], knb_ref[...])
    vb_ref[...] = zb[:, 2 * W_MIX:]
    fl = _dot(n1, wf_ref[...]) + bf_ref[...]
    ls = jnp.minimum(fl, 0.0) - jnp.log1p(jnp.exp(-jnp.abs(fl)))
    lf_ref[...] = ls[:, :N_HEADS]
    zg = _dot(n1, wg_ref[...])
    sg = 1.0 / (1.0 + jnp.exp(-zg))
    ga_ref[...] = sg[:, :D_MODEL].astype(BF16)
    gb_ref[...] = sg[:, D_MODEL:].astype(BF16)


def _proj(x, w, tb):
    n = x.shape[0]
    row = lambda width: pl.BlockSpec((tb, width), lambda i: (i, 0))
    ins = [x, w["g_mix"], w["w_a"], w["w_f"], w["w_b"], w["w_g"], w["b_f"], w["qn_a"], w["kn_a"], w["qn_b"],
           w["kn_b"], w["msum"]]
    in_specs = [row(D_MODEL)] + [_const_spec(a.shape) for a in ins[1:]]
    widths = [(W_MIX, BF16), (W_MIX, F32), (W_MIX, F32), (N_HEADS, F32), (W_MIX, BF16), (W_MIX, F32), (W_MIX, F32),
              (D_MODEL, BF16), (D_MODEL, BF16)]
    return pl.pallas_call(
        _proj_body,
        grid=(n // tb,),
        in_specs=in_specs,
        out_specs=[row(wd) for wd, _ in widths],
        out_shape=[jax.ShapeDtypeStruct((n, wd), dt) for wd, dt in widths],
        compiler_params=_params(("parallel",)),
        name="proj",
    )(*ins)


def _cumsum_lanes(x):
    n = x.shape[-1]
    lane = lax.broadcasted_iota(jnp.int32, x.shape, x.ndim - 1)
    s = 1
    while s < n:
        x = x + jnp.where(lane >= s, pltpu.roll(x, s, axis=x.ndim - 1), 0.0)
        s *= 2
    return x


def _head_of_pair(x, hh):
    lane = lax.broadcasted_iota(jnp.int32, (1, PAIR), 1)
    keep = (lane < HEAD_DIM) if hh == 0 else (lane >= HEAD_DIM)
    return jnp.where(keep, x, jnp.zeros_like(x))


def _merge_pair(o0, o1):
    lane = lax.broadcasted_iota(jnp.int32, (1, PAIR), 1)
    return jnp.where(lane < HEAD_DIM, o0, o1)


def _fox_body(q_ref, k_ref, v_ref, lft_ref, o_ref, kb_s, vb_s, c_s, *, tq, nq):
    hp = pl.program_id(1)
    qi = pl.program_id(2)

    @pl.when(qi == 0)
    def _():
        kb_s[...] = k_ref[0].astype(BF16)
        vb_s[...] = v_ref[0].astype(BF16)
        c_s[...] = _cumsum_lanes(lft_ref[0])

    q = q_ref[0]
    q2 = jnp.concatenate([_head_of_pair(q, 0), _head_of_pair(q, 1)], axis=0)
    row = lax.broadcasted_iota(jnp.int32, (tq, tq), 0)
    col = lax.broadcasted_iota(jnp.int32, (tq, tq), 1)

    def tile(n_blocks):
        past = (n_blocks - 1) * tq
        keys = n_blocks * tq
        s2 = _dot_t(q2, kb_s[:keys, :])
        probs, sums = [], []
        for hh in range(2):
            s = s2[hh * tq:(hh + 1) * tq] - c_s[pl.ds(2 * hp + hh, 1), :keys]
            s_diag = jnp.where(col <= row, s[:, past:], -jnp.inf)
            m = jnp.max(s_diag, axis=-1, keepdims=True)
            if past:
                m = jnp.maximum(m, jnp.max(s[:, :past], axis=-1, keepdims=True))
            p_diag = jnp.exp(s_diag - m)
            l = jnp.sum(p_diag, axis=-1, keepdims=True)
            if past:
                p_past = jnp.exp(s[:, :past] - m)
                l = l + jnp.sum(p_past, axis=-1, keepdims=True)
                p_diag = jnp.concatenate([p_past, p_diag], axis=1)
            probs.append(p_diag.astype(BF16))
            sums.append(l)
        o2 = _dot(jnp.concatenate(probs, axis=0), vb_s[:keys, :])
        o_ref[0] = _merge_pair(o2[:tq] / sums[0], o2[tq:] / sums[1]).astype(o_ref.dtype)

    for n_blocks in range(1, nq + 1):
        pl.when(qi == n_blocks - 1)(functools.partial(tile, n_blocks))


def _fox_prompt(q, k, v, lft, tq):
    b, t, _ = q.shape
    nq = t // tq
    return pl.pallas_call(
        functools.partial(_fox_body, tq=tq, nq=nq),
        grid=(b, N_PAIRS, nq),
        in_specs=[
            pl.BlockSpec((1, tq, PAIR), lambda i, p, j: (i, j, p)),
            pl.BlockSpec((1, t, PAIR), lambda i, p, j: (i, 0, p)),
            pl.BlockSpec((1, t, PAIR), lambda i, p, j: (i, 0, p)),
            pl.BlockSpec((1, N_HEADS, t), lambda i, p, j: (i, 0, 0)),
        ],
        out_specs=pl.BlockSpec((1, tq, PAIR), lambda i, p, j: (i, j, p)),
        out_shape=jax.ShapeDtypeStruct((b, t, W_MIX), BF16),
        scratch_shapes=[pltpu.VMEM((t, PAIR), BF16), pltpu.VMEM((t, PAIR), BF16), pltpu.VMEM((N_HEADS, t), F32)],
        compiler_params=_params(("parallel", "parallel", "arbitrary")),
        name="fox_prompt",
    )(q, k, v, lft)


def _band_body(q_ref, k_ref, v_ref, bias_ref, o_ref, kp_s, vp_s, *, tq):
    qi = pl.program_id(1)
    win = tq + BAND_PAST

    @pl.when(qi == 0)
    def _():
        zeros = jnp.zeros((BAND_PAST, W_MIX), BF16)
        kp_s[:BAND_PAST, :] = zeros
        vp_s[:BAND_PAST, :] = zeros
        kp_s[BAND_PAST:, :] = k_ref[0].astype(BF16)
        vp_s[BAND_PAST:, :] = v_ref[0].astype(BF16)

    off = pl.multiple_of(qi * tq, tq)
    exists = lax.broadcasted_iota(jnp.int32, (1, win), 1) >= BAND_PAST - qi * tq
    for pair in range(N_PAIRS):
        lanes = slice(pair * PAIR, (pair + 1) * PAIR)
        kw = kp_s[pl.ds(off, win), lanes]
        vw = vp_s[pl.ds(off, win), lanes]
        q = q_ref[0, :, lanes]
        outs = []
        for hh in range(2):
            s = _dot_t(_head_of_pair(q, hh), kw) + bias_ref[2 * pair + hh]
            s = jnp.where(exists, s, NEG)
            m = jnp.max(s, axis=-1, keepdims=True)
            p = jnp.exp(s - m)
            l = jnp.sum(p, axis=-1, keepdims=True)
            outs.append(_dot(p.astype(BF16), vw) / l)
        o_ref[0, :, lanes] = _merge_pair(outs[0], outs[1]).astype(o_ref.dtype)


def _band_prompt(q, k, v, bias, tq):
    b, t, _ = q.shape
    win = tq + BAND_PAST
    return pl.pallas_call(
        functools.partial(_band_body, tq=tq),
        grid=(b, t // tq),
        in_specs=[
            pl.BlockSpec((1, tq, W_MIX), lambda i, j: (i, j, 0)),
            pl.BlockSpec((1, t, W_MIX), lambda i, j: (i, 0, 0)),
            pl.BlockSpec((1, t, W_MIX), lambda i, j: (i, 0, 0)),
            _const_spec((N_HEADS, tq, win)),
        ],
        out_specs=pl.BlockSpec((1, tq, W_MIX), lambda i, j: (i, j, 0)),
        out_shape=jax.ShapeDtypeStruct((b, t, W_MIX), BF16),
        scratch_shapes=[pltpu.VMEM((t + BAND_PAST, W_MIX), BF16), pltpu.VMEM((t + BAND_PAST, W_MIX), BF16)],
        compiler_params=_params(("parallel", "arbitrary")),
        name="band_prompt",
    )(q, k, v, bias)


def _toeplitz(w, n, m):
    heads, span = w.shape
    hankel = jnp.tile(w, (1, n + 1))[:, :n * (span + 1)].reshape(heads, n, span + 1)[:, :, :m]
    return hankel[:, ::-1, :]


def _band_bias_prompt(rel_bias, tq):
    win = tq + BAND_PAST
    rel = jnp.arange(tq + win - 1) - (tq - 1) - BAND_PAST
    table = _toeplitz(rel_bias[:, jnp.clip(rel, -MAX_REL, MAX_REL) + MAX_REL].astype(F32), tq, win)
    ii = jnp.arange(tq)[:, None]
    jj = jnp.arange(win)[None, :]
    lo = (ii // CHUNK) * CHUNK
    in_band = (jj >= lo) & (jj < lo + BAND_PAST + CHUNK)
    return jnp.where(in_band[None], table, NEG)


def _fox_sample_body(q_ref, kc_ref, vc_ref, kn_ref, vn_ref, lft_ref, o_ref, *, past, n_new):
    hp = pl.program_id(1)
    c = _cumsum_lanes(lft_ref[0])
    kc = kc_ref[0].astype(BF16)
    vc = vc_ref[0].astype(BF16)
    kn = kn_ref[0].astype(BF16)
    vn = vn_ref[0].astype(BF16)
    q = q_ref[0]
    nq = q.shape[0]
    row = lax.broadcasted_iota(jnp.int32, (nq, LANES), 0)
    col = lax.broadcasted_iota(jnp.int32, (nq, LANES), 1)
    outs = []
    for hh in range(2):
        qh = _head_of_pair(q, hh)
        sel = lax.broadcasted_iota(jnp.int32, (N_HEADS, 1), 0) == 2 * hp + hh
        crow = jnp.sum(jnp.where(sel, c, 0.0), axis=0, keepdims=True)
        sc = _dot_t(qh, kc) - crow[:, :past]
        sn = _dot_t(qh, kn) - crow[:, past:]
        sn = jnp.where((col <= row) & (col < n_new), sn, -jnp.inf)
        m = jnp.maximum(jnp.max(sc, axis=-1, keepdims=True), jnp.max(sn, axis=-1, keepdims=True))
        pc = jnp.exp(sc - m)
        pn = jnp.exp(sn - m)
        l = jnp.sum(pc, axis=-1, keepdims=True) + jnp.sum(pn, axis=-1, keepdims=True)
        outs.append((_dot(pc.astype(BF16), vc) + _dot(pn.astype(BF16), vn)) / l)
    o_ref[0] = _merge_pair(outs[0], outs[1]).astype(o_ref.dtype)


def _fox_sample(q, kc, vc, kn, vn, lft, n_new):
    b, nq, _ = q.shape
    past = kc.shape[1]
    pair_spec = lambda rows: pl.BlockSpec((1, rows, PAIR), lambda i, p: (i, 0, p))
    return pl.pallas_call(
        functools.partial(_fox_sample_body, past=past, n_new=n_new),
        grid=(b, N_PAIRS),
        in_specs=[pair_spec(nq), pair_spec(past), pair_spec(past), pair_spec(LANES), pair_spec(LANES),
                  pl.BlockSpec((1, N_HEADS, past + LANES), lambda i, p: (i, 0, 0))],
        out_specs=pair_spec(nq),
        out_shape=jax.ShapeDtypeStruct((b, nq, W_MIX), BF16),
        compiler_params=_params(("parallel", "parallel")),
        name="fox_sample",
    )(q, kc, vc, kn, vn, lft)


def _band_sample_body(q_ref, kc_ref, vc_ref, kn_ref, vn_ref, bias_ref, o_ref, *, past):
    kc = kc_ref[0].astype(BF16)
    vc = vc_ref[0].astype(BF16)
    kn = kn_ref[0].astype(BF16)
    vn = vn_ref[0].astype(BF16)
    q = q_ref[0]
    outs = []
    for hh in range(2):
        qh = _head_of_pair(q, hh)
        bias = bias_ref[hh]
        sc = _dot_t(qh, kc) + bias[:, :past]
        sn = _dot_t(qh, kn) + bias[:, past:]
        m = jnp.maximum(jnp.max(sc, axis=-1, keepdims=True), jnp.max(sn, axis=-1, keepdims=True))
        pc = jnp.exp(sc - m)
        pn = jnp.exp(sn - m)
        l = jnp.sum(pc, axis=-1, keepdims=True) + jnp.sum(pn, axis=-1, keepdims=True)
        outs.append((_dot(pc.astype(BF16), vc) + _dot(pn.astype(BF16), vn)) / l)
    o_ref[0] = _merge_pair(outs[0], outs[1]).astype(o_ref.dtype)


def _band_sample(q, kc, vc, kn, vn, bias):
    b, nq, _ = q.shape
    past = kc.shape[1]
    pair_spec = lambda rows: pl.BlockSpec((1, rows, PAIR), lambda i, p: (i, 0, p))
    return pl.pallas_call(
        functools.partial(_band_sample_body, past=past),
        grid=(b, N_PAIRS),
        in_specs=[pair_spec(nq), pair_spec(past), pair_spec(past), pair_spec(LANES), pair_spec(LANES),
                  pl.BlockSpec((2, nq, past + LANES), lambda i, p: (p, 0, 0))],
        out_specs=pair_spec(nq),
        out_shape=jax.ShapeDtypeStruct((b, nq, W_MIX), BF16),
        compiler_params=_params(("parallel", "parallel")),
        name="band_sample",
    )(q, kc, vc, kn, vn, bias)


def _band_bias_sample(rel_bias, n_new, past):
    ii = jnp.arange(n_new)[:, None]
    jj = jnp.arange(past + LANES)[None, :]
    table = rel_bias[:, jnp.clip(jj - past - ii, -MAX_REL, MAX_REL) + MAX_REL].astype(F32)
    return jnp.where((jj < past + n_new)[None], table, NEG)


def _merge_body(x_ref, ya_ref, yb_ref, ga_ref, gb_ref, wua_ref, wub_ref, wo_ref, gffn_ref, h_ref, n2_ref):
    merged = (ga_ref[...].astype(F32) * _dot(ya_ref[...], wua_ref[...])
              + gb_ref[...].astype(F32) * _dot(yb_ref[...], wub_ref[...]))
    h = x_ref[...] + _dot(merged.astype(BF16), wo_ref[...])
    h_ref[...] = h
    n2_ref[...] = _rms(h, gffn_ref[...]).astype(BF16)


def _merge(x, ya, yb, ga, gb, w, tb):
    n = x.shape[0]
    row = lambda width: pl.BlockSpec((tb, width), lambda i: (i, 0))
    consts = [w["w_up_a"], w["w_up_b"], w["w_out"], w["g_ffn"]]
    return pl.pallas_call(
        _merge_body,
        grid=(n // tb,),
        in_specs=[row(D_MODEL), row(W_MIX), row(W_MIX), row(D_MODEL), row(D_MODEL)]
        + [_const_spec(a.shape) for a in consts],
        out_specs=[row(D_MODEL), row(D_MODEL)],
        out_shape=[jax.ShapeDtypeStruct((n, D_MODEL), F32), jax.ShapeDtypeStruct((n, D_MODEL), BF16)],
        compiler_params=_params(("parallel",)),
        name="merge",
    )(x, ya, yb, ga, gb, *consts)


def _leave(work, exact):
    hit = work == jnp.max(work, axis=0, keepdims=True)
    if exact:
        idx = lax.broadcasted_iota(jnp.int32, work.shape, 0)
        hit = idx == jnp.min(jnp.where(hit, idx, work.shape[0]), axis=0, keepdims=True)
    return hit


def _top16(s, vals_ref, exact):
    work = s
    for r in range(TOPK):
        vals_ref[r:r + 1, :] = jnp.max(work, axis=0, keepdims=True)
        work = jnp.where(_leave(work, exact), -RANK_SENTINEL * (1.0 + r / 32.0), work)
    return jnp.where(work <= -RANK_SENTINEL, work * (-32.0 / RANK_SENTINEL) - 32.0, float(TOPK))


def _count(mask):
    return jnp.sum(mask.astype(F32), axis=0, keepdims=True)


def _pair_bf16(x):
    bits = lax.bitcast_convert_type(x.astype(BF16).astype(F32), jnp.uint32)
    return bits | (bits >> 16)


def _retrieve_tile(s_s, tt, r1_ref, b_ref, a_ref, c_ref, va_s, vb_s, exact):
    off = jnp.zeros((1, LANES), F32)
    for h in range(PEER_HEADS):
        s0 = s_s[2 * h, tt]
        s1 = s_s[2 * h + 1, tt]
        rank0 = _top16(s0, va_s, exact)
        rank1 = _top16(s1, vb_s, exact)
        va = va_s[...]
        vb = vb_s[...]
        cand = jnp.concatenate([va[0:1] + vb] + [va[k:k + 1] + vb[0:8] for k in range(1, 8)] + [va[8:16] + vb[0:1]],
                               axis=0)
        work = cand
        picked = jnp.zeros(cand.shape, jnp.bool_)
        for _ in range(TOPK):
            hit = _leave(work, exact)
            picked = picked | hit
            work = jnp.where(hit, -jnp.inf, work)
        top = va[0:1] + vb[0:1]
        z = jnp.sum(jnp.where(picked, jnp.exp(cand - top), 0.0), axis=0, keepdims=True)
        pickf = picked.astype(F32)
        counts = [jnp.sum(pickf[0:16], axis=0, keepdims=True)]
        counts += [jnp.sum(pickf[8 + 8 * k:16 + 8 * k], axis=0, keepdims=True) for k in range(1, 8)]
        counts += [pickf[72 + k:73 + k] for k in range(8)]
        c = jnp.zeros(s0.shape, F32)
        for k in range(TOPK):
            c = jnp.where(rank0 == float(k), counts[k], c)
        a = jnp.where(rank0 < float(TOPK), jnp.exp(s0 - va[0:1]) * (0.5 / z), 0.0)
        b = jnp.where(rank1 < float(TOPK), jnp.exp(s1 - vb[0:1]), 0.0)
        r1_ref[tt, h] = rank1.astype(BF16)
        b_ref[tt, h] = b.astype(BF16)
        a_rows = _pair_bf16(a)
        c_rows = _pair_bf16(c)
        for grp in range(N_KEYS // SUBLANES):
            rows = slice(grp * SUBLANES, (grp + 1) * SUBLANES)
            a_ref[tt, h, grp] = a_rows[rows]
            c_ref[tt, h, grp] = c_rows[rows]
        if not exact:
            for n_left in (_count(rank0 < float(TOPK)), _count(rank1 < float(TOPK)), _count(picked)):
                off = off + jnp.abs(n_left - float(TOPK))
    return off


def _retrieve_body(n2_ref, wq_ref, sk_ref, r1_ref, b_ref, a_ref, c_ref, s_s, va_s, vb_s, *, tb):
    q = _dot(n2_ref[...], wq_ref[...]).astype(BF16)
    for h in range(PEER_HEADS):
        qh = q[:, h * PAIR:(h + 1) * PAIR]
        for half in range(2):
            s = _dot_t(sk_ref[h, half], qh)
            for tt in range(tb // LANES):
                s_s[2 * h + half, tt] = s[:, tt * LANES:(tt + 1) * LANES]

    def tile(tt, carry):
        maps = (r1_ref, b_ref, a_ref, c_ref, va_s, vb_s)
        off = _retrieve_tile(s_s, tt, *maps, exact=False)

        @pl.when(jnp.max(off) > 0.0)
        def _():
            _retrieve_tile(s_s, tt, *maps, exact=True)

        return carry

    lax.fori_loop(0, tb // LANES, tile, 0)


def _retrieve(n2, w, tb):
    n = n2.shape[0]
    nt = tb // LANES
    maps = pl.BlockSpec((nt, PEER_HEADS, N_KEYS, LANES), lambda i: (i, 0, 0, 0))
    rows = pl.BlockSpec((nt, PEER_HEADS, N_KEYS // SUBLANES, SUBLANES, LANES), lambda i: (i, 0, 0, 0, 0))
    map_shape = (n // LANES, PEER_HEADS, N_KEYS, LANES)
    row_shape = (n // LANES, PEER_HEADS, N_KEYS // SUBLANES, SUBLANES, LANES)
    return pl.pallas_call(
        functools.partial(_retrieve_body, tb=tb),
        grid=(n // tb,),
        in_specs=[pl.BlockSpec((tb, D_MODEL), lambda i: (i, 0)), _const_spec(w["peer_wq"].shape),
                  _const_spec(w["peer_sk"].shape)],
        out_specs=[maps, maps, rows, rows],
        out_shape=[jax.ShapeDtypeStruct(map_shape, BF16), jax.ShapeDtypeStruct(map_shape, BF16),
                   jax.ShapeDtypeStruct(row_shape, jnp.uint32), jax.ShapeDtypeStruct(row_shape, jnp.uint32)],
        scratch_shapes=[pltpu.VMEM((2 * PEER_HEADS, nt, N_KEYS, LANES), F32), pltpu.VMEM((TOPK, LANES), F32),
                        pltpu.VMEM((TOPK, LANES), F32)],
        compiler_params=_params(("parallel",)),
        name="peer_retrieve",
    )(n2, w["peer_wq"], w["peer_sk"])


def _twice_gelu(x):
    k0 = math.sqrt(2.0 / math.pi)
    return x * (1.0 + jnp.tanh(x * (k0 + (k0 * 0.044715) * (x * x))))


def _row_tile(ref, tt, h, g, ii):
    row = jnp.broadcast_to(ref[tt, h, g, ii:ii + 1, :], (SUBLANES, LANES))
    packed = pltpu.bitcast(row, BF16)
    return jnp.concatenate([packed] * (N_KEYS // packed.shape[0]), axis=0)


def _dense_body(n2_ref, u_ref, vt_ref, r1_ref, b_ref, a_ref, c_ref, o_ref, acc_s, gate_s, r1_s, b_s, *, tb):
    g = pl.program_id(1)

    @pl.when(g == 0)
    def _():
        acc_s[...] = jnp.zeros(acc_s.shape, F32)
        r1_s[...] = r1_ref[...]
        b_s[...] = b_ref[...]

    rows_per_load = 2
    for tt in range(tb // LANES):
        lanes = slice(tt * LANES, (tt + 1) * LANES)
        for ii0 in range(0, SUBLANES, rows_per_load):
            gates = [None] * rows_per_load
            for h in range(PEER_HEADS):
                rank_tile = r1_s[tt, h]
                weight_tile = b_s[tt, h]
                for k in range(rows_per_load):
                    wgt = weight_tile * _row_tile(a_ref, tt, h, g, ii0 + k)
                    term = jnp.where(rank_tile < _row_tile(c_ref, tt, h, g, ii0 + k), wgt, jnp.zeros_like(wgt))
                    gates[k] = term if gates[k] is None else gates[k] + term
            for k in range(rows_per_load):
                gate_s[(ii0 + k) * N_KEYS:(ii0 + k + 1) * N_KEYS, lanes] = gates[k]

    hid = _dot_t(u_ref[...], n2_ref[...])
    weighted = gate_s[...] * _twice_gelu(hid.astype(BF16))
    acc_s[...] += _dot(vt_ref[0], weighted)

    @pl.when(g == pl.num_programs(1) - 1)
    def _():
        o_ref[...] = acc_s[...].T


def _dense(n2, u, vt, r1, b, a, c, tb):
    n = n2.shape[0]
    ec = EXPERT_CHUNK
    ng = N_EXPERTS // ec
    nt = tb // LANES
    once = pl.Buffered(1)
    maps = pl.BlockSpec((nt, PEER_HEADS, N_KEYS, LANES), lambda i, g: (i, 0, 0, 0), pipeline_mode=once)
    rows = pl.BlockSpec((nt, PEER_HEADS, ng, SUBLANES, LANES), lambda i, g: (i, 0, 0, 0, 0), pipeline_mode=once)
    return pl.pallas_call(
        functools.partial(_dense_body, tb=tb),
        grid=(n // tb, ng),
        in_specs=[pl.BlockSpec((tb, D_MODEL), lambda i, g: (i, 0), pipeline_mode=once),
                  pl.BlockSpec((ec, D_MODEL), lambda i, g: (g, 0)),
                  pl.BlockSpec((1, D_MODEL, ec), lambda i, g: (g, 0, 0)),
                  maps, maps, rows, rows],
        out_specs=pl.BlockSpec((tb, D_MODEL), lambda i, g: (i, 0)),
        out_shape=jax.ShapeDtypeStruct((n, D_MODEL), F32),
        scratch_shapes=[pltpu.VMEM((D_MODEL, tb), F32), pltpu.VMEM((ec, tb), BF16),
                        pltpu.VMEM((nt, PEER_HEADS, N_KEYS, LANES), BF16),
                        pltpu.VMEM((nt, PEER_HEADS, N_KEYS, LANES), BF16)],
        compiler_params=_params(("parallel", "arbitrary")),
        name="peer_dense",
    )(n2, u, vt, r1, b, a, c)


def _final_body(h_ref, peer_ref, p_ref, gple_ref, wg_ref, wp_ref, o_ref):
    h = h_ref[...] + peer_ref[...]
    zg = _dot(_rms(h, gple_ref[...]).astype(BF16), wg_ref[...])
    gate = 1.0 / (1.0 + jnp.exp(-zg))
    o_ref[...] = h + gate * _dot(p_ref[...].astype(BF16), wp_ref[...])


def _final(h, peer, p, w, tb):
    n = h.shape[0]
    row = lambda width: pl.BlockSpec((tb, width), lambda i: (i, 0))
    consts = [w["g_ple"], w["w_ple_gate"], w["w_ple_proj"]]
    return pl.pallas_call(
        _final_body,
        grid=(n // tb,),
        in_specs=[row(D_MODEL), row(D_MODEL), row(PLE_DIM)] + [_const_spec(a.shape) for a in consts],
        out_specs=row(D_MODEL),
        out_shape=jax.ShapeDtypeStruct((n, D_MODEL), F32),
        compiler_params=_params(("parallel",)),
        name="final",
    )(h, peer, p, *consts)


def _layer_weights(l, g_mix, w_in, b_f, qn_a, kn_a, qn_b, kn_b, w_up_a, w_up_b, w_out, g_ffn, peer_wq, peer_subkeys,
                   peer_u, peer_v, g_ple, w_ple_gate, w_ple_proj):
    o_f = 3 * W_MIX
    o_b = o_f + N_HEADS
    o_g = o_b + 3 * W_MIX
    wi = w_in[l]
    tile_heads = lambda g: jnp.tile(g[l].astype(F32), N_HEADS)[None, :]
    head_of = jnp.arange(W_MIX) // HEAD_DIM
    sk = peer_subkeys[l].astype(BF16)
    zeros = jnp.zeros_like(sk[:, 0])
    sk_pad = jnp.stack([jnp.concatenate([sk[:, 0], zeros], axis=-1), jnp.concatenate([zeros, sk[:, 1]], axis=-1)],
                       axis=1)
    return {
        "g_mix": g_mix[l][None, :],
        "w_a": wi[:, :o_f].astype(BF16),
        "w_f": jnp.pad(wi[:, o_f:o_b], ((0, 0), (0, LANES - N_HEADS))).astype(BF16),
        "w_b": wi[:, o_b:o_g].astype(BF16),
        "w_g": wi[:, o_g:].astype(BF16),
        "b_f": jnp.pad(b_f[l], (0, LANES - N_HEADS))[None, :],
        "qn_a": tile_heads(qn_a), "kn_a": tile_heads(kn_a), "qn_b": tile_heads(qn_b), "kn_b": tile_heads(kn_b),
        "msum": jnp.where(head_of[:, None] == head_of[None, :], 1.0 / HEAD_DIM, 0.0).astype(BF16),
        "w_up_a": w_up_a[l].astype(BF16), "w_up_b": w_up_b[l].astype(BF16), "w_out": w_out[l].astype(BF16),
        "g_ffn": g_ffn[l][None, :],
        "peer_wq": peer_wq[l].astype(BF16),
        "peer_sk": sk_pad,
        "peer_u": peer_u[l].astype(BF16),
        "peer_vt": peer_v[l].astype(BF16).reshape(-1, EXPERT_CHUNK, D_MODEL).transpose(0, 2, 1),
        "g_ple": g_ple[l][None, :],
        "w_ple_gate": w_ple_gate[l].astype(BF16),
        "w_ple_proj": w_ple_proj[l].astype(BF16),
    }


def _channel(x, ya, yb, ga, gb, p, w, tb, tb_dense):
    h1, n2 = _merge(x, ya, yb, ga, gb, w, tb)
    r1, b, a, c = _retrieve(n2, w, tb_dense)
    peer = _dense(n2, w["peer_u"], w["peer_vt"], r1, b, a, c, tb_dense)
    return _final(h1, peer, p, w, tb)


def _pad_rows(x, rows):
    return jnp.pad(x, ((0, 0), (0, rows - x.shape[1]), (0, 0)))


def kernel(x_prompt, x_sample, cache_a_k, cache_a_v, cache_a_logf, cache_b_k, cache_b_v, p_prompt, p_sample, g_mix, w_in, b_f, qn_a, kn_a, qn_b, kn_b, rel_bias_b, w_up_a, w_up_b, w_out, g_ffn, peer_wq, peer_subkeys, peer_u, peer_v, g_ple, w_ple_gate, w_ple_proj):
    depth = w_in.shape[0]
    bp, tp, _ = x_prompt.shape
    bs, ts, _ = x_sample.shape
    past = cache_a_k.shape[2]
    band_rows = min(BAND_PAST, tp)
    tq = 256
    hp = x_prompt.reshape(bp * tp, D_MODEL)
    hs = x_sample.reshape(bs * ts, D_MODEL)
    outs = [[] for _ in range(10)]
    for l in range(depth):
        w = _layer_weights(l, g_mix, w_in, b_f, qn_a, kn_a, qn_b, kn_b, w_up_a, w_up_b, w_out, g_ffn, peer_wq,
                           peer_subkeys, peer_u, peer_v, g_ple, w_ple_gate, w_ple_proj)
        qa, ka, va, lf, qb, kb, vb, ga, gb = _proj(hp, w, 256)
        as_seq = lambda z: z.reshape(bp, tp, z.shape[-1])
        ya = _fox_prompt(as_seq(qa), as_seq(ka), as_seq(va), as_seq(lf).transpose(0, 2, 1), tq)
        yb = _band_prompt(as_seq(qb), as_seq(kb), as_seq(vb), _band_bias_prompt(rel_bias_b[l], tq), tq)
        hp = _channel(hp, ya.reshape(-1, W_MIX), yb.reshape(-1, W_MIX), ga, gb, p_prompt[l].reshape(-1, PLE_DIM), w,
                      256, 1024)
        heads = lambda z, b_, t_: z.reshape(b_, t_, N_HEADS, HEAD_DIM)
        outs[0].append(heads(ka, bp, tp)); outs[1].append(heads(va, bp, tp)); outs[2].append(as_seq(lf))
        outs[3].append(heads(kb, bp, tp)[:, -band_rows:]); outs[4].append(heads(vb, bp, tp)[:, -band_rows:])
        qa, ka, va, lf, qb, kb, vb, ga, gb = _proj(hs, w, bs * ts)
        as_seq = lambda z: z.reshape(bs, ts, z.shape[-1])
        lft = jnp.concatenate([cache_a_logf[l].astype(F32), as_seq(lf),
                               jnp.zeros((bs, LANES - ts, N_HEADS), F32)], axis=1).transpose(0, 2, 1)
        flat_cache = lambda z: z[l].reshape(bs, z.shape[2], W_MIX)
        ya = _fox_sample(as_seq(qa), flat_cache(cache_a_k), flat_cache(cache_a_v), _pad_rows(as_seq(ka), LANES),
                         _pad_rows(as_seq(va), LANES), lft, ts)
        yb = _band_sample(as_seq(qb), flat_cache(cache_b_k), flat_cache(cache_b_v), _pad_rows(as_seq(kb), LANES),
                          _pad_rows(as_seq(vb), LANES), _band_bias_sample(rel_bias_b[l], ts, cache_b_k.shape[2]))
        hs = _channel(hs, ya.reshape(-1, W_MIX), yb.reshape(-1, W_MIX), ga, gb, p_sample[l].reshape(-1, PLE_DIM), w,
                      bs * ts, bs * ts)
        outs[5].append(heads(ka, bs, ts)); outs[6].append(heads(va, bs, ts)); outs[7].append(as_seq(lf))
        outs[8].append(heads(kb, bs, ts)); outs[9].append(heads(vb, bs, ts))
    return (hp.reshape(bp, tp, D_MODEL), hs.reshape(bs, ts, D_MODEL)) + tuple(jnp.stack(o) for o in outs)
```

```python
import functools
import math

import jax
import jax.numpy as jnp
from jax import lax
from jax.experimental import pallas as pl
from jax.experimental.pallas import tpu as pltpu

F32 = jnp.float32
BF16 = jnp.bfloat16

D_MODEL = 1024
HEAD_DIM = 64
N_HEADS = 8
W_MIX = N_HEADS * HEAD_DIM
PAIR = 2 * HEAD_DIM
N_PAIRS = N_HEADS // 2
CHUNK = 64
BAND_PAST = 8 * CHUNK
MAX_REL = 128
PLE_DIM = 256
PEER_HEADS = 8
N_KEYS = 128
N_EXPERTS = N_KEYS * N_KEYS
TOPK = 16
RMS_EPS = 1e-6
ATT_SCALE = HEAD_DIM ** -0.5
NEG = -1e30
RANK_SENTINEL = 2.0 ** 100
FOX_KEY_CHUNK = 512
LANES = 128
SUBLANES = 8
EXPERT_CHUNK = SUBLANES * N_KEYS
VMEM_LIMIT = 56 * 1024 * 1024

N_CAND = 16 + 7 * 8 + 8


def _params(sem, vmem=VMEM_LIMIT):
    return pltpu.CompilerParams(dimension_semantics=sem, vmem_limit_bytes=vmem)


def _rms(x, g):
    return x * lax.rsqrt(jnp.mean(x * x, axis=-1, keepdims=True) + RMS_EPS) * g


def _dot(a, b):
    return jnp.dot(a, b, preferred_element_type=F32)


def _dot_t(a, b):
    return lax.dot_general(a, b, (((1,), (1,)), ((), ())), preferred_element_type=F32)


def _const_spec(shape):
    nd = len(shape)
    return pl.BlockSpec(shape, lambda *_: (0,) * nd)


def _proj_body(x_ref, g_ref, wa_ref, wf_ref, wb_ref, wg_ref, bf_ref, qna_ref, kna_ref, qnb_ref, knb_ref, msum_ref,
               qa_ref, ka_ref, va_ref, lf_ref, qb_ref, kb_ref, vb_ref, ga_ref, gb_ref):
    n1 = _rms(x_ref[...], g_ref[...]).astype(BF16)

    def head_norm(z, gain):
        ms = _dot((z * z).astype(BF16), msum_ref[...])
        return z * lax.rsqrt(ms + RMS_EPS) * gain

    za = _dot(n1, wa_ref[...])
    qa_ref[...] = (head_norm(za[:, :W_MIX], qna_ref[...]) * ATT_SCALE).astype(BF16)
    ka_ref[...] = head_norm(za[:, W_MIX:2 * W_MIX], kna_ref[...])
    va_ref[...] = za[:, 2 * W_MIX:]
    zb = _dot(n1, wb_ref[...])
    qb_ref[...] = (head_norm(zb[:, :W_MIX], qnb_ref[...]) * ATT_SCALE).astype(BF16)
    kb_ref[...] = head_norm(zb[:, W_MIX:2 * W_MIX], knb_ref[...])
    vb_ref[...] = zb[:, 2 * W_MIX:]
    fl = _dot(n1, wf_ref[...]) + bf_ref[...]
    ls = jnp.minimum(fl, 0.0) - jnp.log1p(jnp.exp(-jnp.abs(fl)))
    lf_ref[...] = ls[:, :N_HEADS]
    zg = _dot(n1, wg_ref[...])
    sg = 1.0 / (1.0 + jnp.exp(-zg))
    ga_ref[...] = sg[:, :D_MODEL].astype(BF16)
    gb_ref[...] = sg[:, D_MODEL:].astype(BF16)


def _proj(x, w, tb):
    n = x.shape[0]
    row = lambda width: pl.BlockSpec((tb, width), lambda i: (i, 0))
    ins = [x, w["g_mix"], w["w_a"], w["w_f"], w["w_b"], w["w_g"], w["b_f"], w["qn_a"], w["kn_a"], w["qn_b"],
           w["kn_b"], w["msum"]]
    in_specs = [row(D_MODEL)] + [_const_spec(a.shape) for a in ins[1:]]
    widths = [(W_MIX, BF16), (W_MIX, F32), (W_MIX, F32), (N_HEADS, F32), (W_MIX, BF16), (W_MIX, F32), (W_MIX, F32),
              (D_MODEL, BF16), (D_MODEL, BF16)]
    return pl.pallas_call(
        _proj_body,
        grid=(n // tb,),
        in_specs=in_specs,
        out_specs=[row(wd) for wd, _ in widths],
        out_shape=[jax.ShapeDtypeStruct((n, wd), dt) for wd, dt in widths],
        compiler_params=_params(("parallel",)),
        name="proj",
    )(*ins)


def _cumsum_lanes(x):
    n = x.shape[-1]
    lane = lax.broadcasted_iota(jnp.int32, x.shape, x.ndim - 1)
    s = 1
    while s < n:
        x = x + jnp.where(lane >= s, pltpu.roll(x, s, axis=x.ndim - 1), 0.0)
        s *= 2
    return x


def _head_of_pair(x, hh):
    lane = lax.broadcasted_iota(jnp.int32, (1, PAIR), 1)
    keep = (lane < HEAD_DIM) if hh == 0 else (lane >= HEAD_DIM)
    return jnp.where(keep, x, jnp.zeros_like(x))


def _merge_pair(o0, o1):
    lane = lax.broadcasted_iota(jnp.int32, (1, PAIR), 1)
    return jnp.where(lane < HEAD_DIM, o0, o1)


def _fox_body(q_ref, k_ref, v_ref, lft_ref, o_ref, kb_s, vb_s, c_s, *, tq, nq):
    hp = pl.program_id(1)
    qi = pl.program_id(2)

    @pl.when(qi == 0)
    def _():
        kb_s[...] = k_ref[0].astype(BF16)
        vb_s[...] = v_ref[0].astype(BF16)
        c_s[...] = _cumsum_lanes(lft_ref[0])

    q = q_ref[0]
    q2 = jnp.concatenate([_head_of_pair(q, 0), _head_of_pair(q, 1)], axis=0)
    row = lax.broadcasted_iota(jnp.int32, (tq, tq), 0)
    col = lax.broadcasted_iota(jnp.int32, (tq, tq), 1)

    def tile(n_blocks):
        keys = n_blocks * tq
        parts = [[], []]
        for start in range(0, keys, FOX_KEY_CHUNK):
            size = min(FOX_KEY_CHUNK, keys - start)
            s2 = _dot_t(q2, kb_s[start:start + size, :])
            probs, stats = [], []
            for hh in range(2):
                s = s2[hh * tq:(hh + 1) * tq] - c_s[pl.ds(2 * hp + hh, 1), start:start + size]
                if start + size == keys:
                    diag = jnp.where(col <= row, s[:, size - tq:], -jnp.inf)
                    s = diag if size == tq else jnp.concatenate([s[:, :size - tq], diag], axis=1)
                m = jnp.max(s, axis=-1, keepdims=True)
                p = jnp.exp(s - m)
                stats.append((m, jnp.sum(p, axis=-1, keepdims=True)))
                probs.append(p.astype(BF16))
            o2 = _dot(jnp.concatenate(probs, axis=0), vb_s[start:start + size, :])
            for hh in range(2):
                parts[hh].append(stats[hh] + (o2[hh * tq:(hh + 1) * tq],))
        outs = []
        for hh in range(2):
            m = functools.reduce(jnp.maximum, [mc for mc, _, _ in parts[hh]])
            scales = [jnp.exp(mc - m) for mc, _, _ in parts[hh]]
            l = sum(lc * sc for (_, lc, _), sc in zip(parts[hh], scales))
            o = sum(oc * sc for (_, _, oc), sc in zip(parts[hh], scales))
            outs.append(o / l)
        o_ref[0] = _merge_pair(outs[0], outs[1]).astype(o_ref.dtype)

    for n_blocks in range(1, nq + 1):
        pl.when(qi == n_blocks - 1)(functools.partial(tile, n_blocks))


def _fox_prompt(q, k, v, lft, tq):
    b, t, _ = q.shape
    nq = t // tq
    return pl.pallas_call(
        functools.partial(_fox_body, tq=tq, nq=nq),
        grid=(b, N_PAIRS, nq),
        in_specs=[
            pl.BlockSpec((1, tq, PAIR), lambda i, p, j: (i, j, p)),
            pl.BlockSpec((1, t, PAIR), lambda i, p, j: (i, 0, p)),
            pl.BlockSpec((1, t, PAIR), lambda i, p, j: (i, 0, p)),
            pl.BlockSpec((1, N_HEADS, t), lambda i, p, j: (i, 0, 0)),
        ],
        out_specs=pl.BlockSpec((1, tq, PAIR), lambda i, p, j: (i, j, p)),
        out_shape=jax.ShapeDtypeStruct((b, t, W_MIX), BF16),
        scratch_shapes=[pltpu.VMEM((t, PAIR), BF16), pltpu.VMEM((t, PAIR), BF16), pltpu.VMEM((N_HEADS, t), F32)],
        compiler_params=_params(("parallel", "parallel", "arbitrary")),
        name="fox_prompt",
    )(q, k, v, lft)


def _band_body(q_ref, k_ref, v_ref, bias_ref, o_ref, kp_s, vp_s, *, tq):
    qi = pl.program_id(1)
    win = tq + BAND_PAST

    @pl.when(qi == 0)
    def _():
        zeros = jnp.zeros((BAND_PAST, W_MIX), BF16)
        kp_s[:BAND_PAST, :] = zeros
        vp_s[:BAND_PAST, :] = zeros
        kp_s[BAND_PAST:, :] = k_ref[0].astype(BF16)
        vp_s[BAND_PAST:, :] = v_ref[0].astype(BF16)

    off = pl.multiple_of(qi * tq, tq)
    exists = lax.broadcasted_iota(jnp.int32, (1, win), 1) >= BAND_PAST - qi * tq
    for pair in range(N_PAIRS):
        lanes = slice(pair * PAIR, (pair + 1) * PAIR)
        kw = kp_s[pl.ds(off, win), lanes]
        vw = vp_s[pl.ds(off, win), lanes]
        q = q_ref[0, :, lanes]
        outs = []
        for hh in range(2):
            s = _dot_t(_head_of_pair(q, hh), kw) + bias_ref[2 * pair + hh]
            s = jnp.where(exists, s, NEG)
            m = jnp.max(s, axis=-1, keepdims=True)
            p = jnp.exp(s - m)
            l = jnp.sum(p, axis=-1, keepdims=True)
            outs.append(_dot(p.astype(BF16), vw) / l)
        o_ref[0, :, lanes] = _merge_pair(outs[0], outs[1]).astype(o_ref.dtype)


def _band_prompt(q, k, v, bias, tq):
    b, t, _ = q.shape
    win = tq + BAND_PAST
    return pl.pallas_call(
        functools.partial(_band_body, tq=tq),
        grid=(b, t // tq),
        in_specs=[
            pl.BlockSpec((1, tq, W_MIX), lambda i, j: (i, j, 0)),
            pl.BlockSpec((1, t, W_MIX), lambda i, j: (i, 0, 0)),
            pl.BlockSpec((1, t, W_MIX), lambda i, j: (i, 0, 0)),
            _const_spec((N_HEADS, tq, win)),
        ],
        out_specs=pl.BlockSpec((1, tq, W_MIX), lambda i, j: (i, j, 0)),
        out_shape=jax.ShapeDtypeStruct((b, t, W_MIX), BF16),
        scratch_shapes=[pltpu.VMEM((t + BAND_PAST, W_MIX), BF16), pltpu.VMEM((t + BAND_PAST, W_MIX), BF16)],
        compiler_params=_params(("parallel", "arbitrary")),
        name="band_prompt",
    )(q, k, v, bias)


def _toeplitz(w, n, m):
    heads, span = w.shape
    hankel = jnp.tile(w, (1, n + 1))[:, :n * (span + 1)].reshape(heads, n, span + 1)[:, :, :m]
    return hankel[:, ::-1, :]


def _band_bias_prompt(rel_bias, tq):
    win = tq + BAND_PAST
    rel = jnp.arange(tq + win - 1) - (tq - 1) - BAND_PAST
    table = _toeplitz(rel_bias[:, jnp.clip(rel, -MAX_REL, MAX_REL) + MAX_REL].astype(F32), tq, win)
    ii = jnp.arange(tq)[:, None]
    jj = jnp.arange(win)[None, :]
    lo = (ii // CHUNK) * CHUNK
    in_band = (jj >= lo) & (jj < lo + BAND_PAST + CHUNK)
    return jnp.where(in_band[None], table, NEG)


def _fox_sample_body(q_ref, kc_ref, vc_ref, kn_ref, vn_ref, lft_ref, o_ref, *, past, n_new):
    hp = pl.program_id(1)
    c = _cumsum_lanes(lft_ref[0])
    kc = kc_ref[0].astype(BF16)
    vc = vc_ref[0].astype(BF16)
    kn = kn_ref[0].astype(BF16)
    vn = vn_ref[0].astype(BF16)
    q = q_ref[0]
    nq = q.shape[0]
    row = lax.broadcasted_iota(jnp.int32, (nq, LANES), 0)
    col = lax.broadcasted_iota(jnp.int32, (nq, LANES), 1)
    outs = []
    for hh in range(2):
        qh = _head_of_pair(q, hh)
        sel = lax.broadcasted_iota(jnp.int32, (N_HEADS, 1), 0) == 2 * hp + hh
        crow = jnp.sum(jnp.where(sel, c, 0.0), axis=0, keepdims=True)
        sc = _dot_t(qh, kc) - crow[:, :past]
        sn = _dot_t(qh, kn) - crow[:, past:]
        sn = jnp.where((col <= row) & (col < n_new), sn, -jnp.inf)
        m = jnp.maximum(jnp.max(sc, axis=-1, keepdims=True), jnp.max(sn, axis=-1, keepdims=True))
        pc = jnp.exp(sc - m)
        pn = jnp.exp(sn - m)
        l = jnp.sum(pc, axis=-1, keepdims=True) + jnp.sum(pn, axis=-1, keepdims=True)
        outs.append((_dot(pc.astype(BF16), vc) + _dot(pn.astype(BF16), vn)) / l)
    o_ref[0] = _merge_pair(outs[0], outs[1]).astype(o_ref.dtype)


def _fox_sample(q, kc, vc, kn, vn, lft, n_new):
    b, nq, _ = q.shape
    past = kc.shape[1]
    pair_spec = lambda rows: pl.BlockSpec((1, rows, PAIR), lambda i, p: (i, 0, p))
    return pl.pallas_call(
        functools.partial(_fox_sample_body, past=past, n_new=n_new),
        grid=(b, N_PAIRS),
        in_specs=[pair_spec(nq), pair_spec(past), pair_spec(past), pair_spec(LANES), pair_spec(LANES),
                  pl.BlockSpec((1, N_HEADS, past + LANES), lambda i, p: (i, 0, 0))],
        out_specs=pair_spec(nq),
        out_shape=jax.ShapeDtypeStruct((b, nq, W_MIX), BF16),
        compiler_params=_params(("parallel", "parallel")),
        name="fox_sample",
    )(q, kc, vc, kn, vn, lft)


def _band_sample_body(q_ref, kc_ref, vc_ref, kn_ref, vn_ref, bias_ref, o_ref, *, past):
    kc = kc_ref[0].astype(BF16)
    vc = vc_ref[0].astype(BF16)
    kn = kn_ref[0].astype(BF16)
    vn = vn_ref[0].astype(BF16)
    q = q_ref[0]
    outs = []
    for hh in range(2):
        qh = _head_of_pair(q, hh)
        bias = bias_ref[hh]
        sc = _dot_t(qh, kc) + bias[:, :past]
        sn = _dot_t(qh, kn) + bias[:, past:]
        m = jnp.maximum(jnp.max(sc, axis=-1, keepdims=True), jnp.max(sn, axis=-1, keepdims=True))
        pc = jnp.exp(sc - m)
        pn = jnp.exp(sn - m)
        l = jnp.sum(pc, axis=-1, keepdims=True) + jnp.sum(pn, axis=-1, keepdims=True)
        outs.append((_dot(pc.astype(BF16), vc) + _dot(pn.astype(BF16), vn)) / l)
    o_ref[0] = _merge_pair(outs[0], outs[1]).astype(o_ref.dtype)


def _band_sample(q, kc, vc, kn, vn, bias):
    b, nq, _ = q.shape
    past = kc.shape[1]
    pair_spec = lambda rows: pl.BlockSpec((1, rows, PAIR), lambda i, p: (i, 0, p))
    return pl.pallas_call(
        functools.partial(_band_sample_body, past=past),
        grid=(b, N_PAIRS),
        in_specs=[pair_spec(nq), pair_spec(past), pair_spec(past), pair_spec(LANES), pair_spec(LANES),
                  pl.BlockSpec((2, nq, past + LANES), lambda i, p: (p, 0, 0))],
        out_specs=pair_spec(nq),
        out_shape=jax.ShapeDtypeStruct((b, nq, W_MIX), BF16),
        compiler_params=_params(("parallel", "parallel")),
        name="band_sample",
    )(q, kc, vc, kn, vn, bias)


def _band_bias_sample(rel_bias, n_new, past):
    ii = jnp.arange(n_new)[:, None]
    jj = jnp.arange(past + LANES)[None, :]
    table = rel_bias[:, jnp.clip(jj - past - ii, -MAX_REL, MAX_REL) + MAX_REL].astype(F32)
    return jnp.where((jj < past + n_new)[None], table, NEG)


def _merge_body(x_ref, ya_ref, yb_ref, ga_ref, gb_ref, wua_ref, wub_ref, wo_ref, gffn_ref, h_ref, n2_ref):
    merged = (ga_ref[...].astype(F32) * _dot(ya_ref[...], wua_ref[...])
              + gb_ref[...].astype(F32) * _dot(yb_ref[...], wub_ref[...]))
    h = x_ref[...] + _dot(merged.astype(BF16), wo_ref[...])
    h_ref[...] = h
    n2_ref[...] = _rms(h, gffn_ref[...]).astype(BF16)


def _merge(x, ya, yb, ga, gb, w, tb):
    n = x.shape[0]
    row = lambda width: pl.BlockSpec((tb, width), lambda i: (i, 0))
    consts = [w["w_up_a"], w["w_up_b"], w["w_out"], w["g_ffn"]]
    return pl.pallas_call(
        _merge_body,
        grid=(n // tb,),
        in_specs=[row(D_MODEL), row(W_MIX), row(W_MIX), row(D_MODEL), row(D_MODEL)]
        + [_const_spec(a.shape) for a in consts],
        out_specs=[row(D_MODEL), row(D_MODEL)],
        out_shape=[jax.ShapeDtypeStruct((n, D_MODEL), F32), jax.ShapeDtypeStruct((n, D_MODEL), BF16)],
        compiler_params=_params(("parallel",)),
        name="merge",
    )(x, ya, yb, ga, gb, *consts)


def _leave(work, exact):
    hit = work == jnp.max(work, axis=0, keepdims=True)
    if exact:
        idx = lax.broadcasted_iota(jnp.int32, work.shape, 0)
        hit = idx == jnp.min(jnp.where(hit, idx, work.shape[0]), axis=0, keepdims=True)
    return hit


def _top16(s, vals_ref, exact):
    work = s
    for r in range(TOPK):
        vals_ref[r:r + 1, :] = jnp.max(work, axis=0, keepdims=True)
        work = jnp.where(_leave(work, exact), -RANK_SENTINEL * (1.0 + r / 32.0), work)
    return jnp.where(work <= -RANK_SENTINEL, work * (-32.0 / RANK_SENTINEL) - 32.0, float(TOPK))


def _count(mask):
    return jnp.sum(mask.astype(F32), axis=0, keepdims=True)


def _pair_bf16(x):
    bits = lax.bitcast_convert_type(x.astype(BF16).astype(F32), jnp.uint32)
    return bits | (bits >> 16)


def _retrieve_tile(s_s, tt, r1_ref, b_ref, a_ref, c_ref, va_s, vb_s, exact):
    off = jnp.zeros((1, LANES), F32)
    for h in range(PEER_HEADS):
        s0 = s_s[2 * h, tt]
        s1 = s_s[2 * h + 1, tt]
        rank0 = _top16(s0, va_s, exact)
        rank1 = _top16(s1, vb_s, exact)
        va = va_s[...]
        vb = vb_s[...]
        cand = jnp.concatenate([va[0:1] + vb] + [va[k:k + 1] + vb[0:8] for k in range(1, 8)] + [va[8:16] + vb[0:1]],
                               axis=0)
        work = cand
        picked = jnp.zeros(cand.shape, jnp.bool_)
        for _ in range(TOPK):
            hit = _leave(work, exact)
            picked = picked | hit
            work = jnp.where(hit, -jnp.inf, work)
        top = va[0:1] + vb[0:1]
        z = jnp.sum(jnp.where(picked, jnp.exp(cand - top), 0.0), axis=0, keepdims=True)
        pickf = picked.astype(F32)
        counts = [jnp.sum(pickf[0:16], axis=0, keepdims=True)]
        counts += [jnp.sum(pickf[8 + 8 * k:16 + 8 * k], axis=0, keepdims=True) for k in range(1, 8)]
        counts += [pickf[72 + k:73 + k] for k in range(8)]
        c = jnp.zeros(s0.shape, F32)
        for k in range(TOPK):
            c = jnp.where(rank0 == float(k), counts[k], c)
        a = jnp.where(rank0 < float(TOPK), jnp.exp(s0 - va[0:1]) * (0.5 / z), 0.0)
        b = jnp.where(rank1 < float(TOPK), jnp.exp(s1 - vb[0:1]), 0.0)
        r1_ref[tt, h] = rank1.astype(BF16)
        b_ref[tt, h] = b.astype(BF16)
        a_rows = _pair_bf16(a)
        c_rows = _pair_bf16(c)
        for grp in range(N_KEYS // SUBLANES):
            rows = slice(grp * SUBLANES, (grp + 1) * SUBLANES)
            a_ref[tt, h, grp] = a_rows[rows]
            c_ref[tt, h, grp] = c_rows[rows]
        if not exact:
            for n_left in (_count(rank0 < float(TOPK)), _count(rank1 < float(TOPK)), _count(picked)):
                off = off + jnp.abs(n_left - float(TOPK))
    return off


def _retrieve_body(n2_ref, wq_ref, sk_ref, r1_ref, b_ref, a_ref, c_ref, s_s, va_s, vb_s, *, tb):
    q = _dot(n2_ref[...], wq_ref[...]).astype(BF16)
    for h in range(PEER_HEADS):
        qh = q[:, h * PAIR:(h + 1) * PAIR]
        for half in range(2):
            s = _dot_t(sk_ref[h, half], qh)
            for tt in range(tb // LANES):
                s_s[2 * h + half, tt] = s[:, tt * LANES:(tt + 1) * LANES]

    def tile(tt, carry):
        maps = (r1_ref, b_ref, a_ref, c_ref, va_s, vb_s)
        off = _retrieve_tile(s_s, tt, *maps, exact=False)

        @pl.when(jnp.max(off) > 0.0)
        def _():
            _retrieve_tile(s_s, tt, *maps, exact=True)

        return carry

    lax.fori_loop(0, tb // LANES, tile, 0)


def _retrieve(n2, w, tb):
    n = n2.shape[0]
    nt = tb // LANES
    maps = pl.BlockSpec((nt, PEER_HEADS, N_KEYS, LANES), lambda i: (i, 0, 0, 0))
    rows = pl.BlockSpec((nt, PEER_HEADS, N_KEYS // SUBLANES, SUBLANES, LANES), lambda i: (i, 0, 0, 0, 0))
    map_shape = (n // LANES, PEER_HEADS, N_KEYS, LANES)
    row_shape = (n // LANES, PEER_HEADS, N_KEYS // SUBLANES, SUBLANES, LANES)
    return pl.pallas_call(
        functools.partial(_retrieve_body, tb=tb),
        grid=(n // tb,),
        in_specs=[pl.BlockSpec((tb, D_MODEL), lambda i: (i, 0)), _const_spec(w["peer_wq"].shape),
                  _const_spec(w["peer_sk"].shape)],
        out_specs=[maps, maps, rows, rows],
        out_shape=[jax.ShapeDtypeStruct(map_shape, BF16), jax.ShapeDtypeStruct(map_shape, BF16),
                   jax.ShapeDtypeStruct(row_shape, jnp.uint32), jax.ShapeDtypeStruct(row_shape, jnp.uint32)],
        scratch_shapes=[pltpu.VMEM((2 * PEER_HEADS, nt, N_KEYS, LANES), F32), pltpu.VMEM((TOPK, LANES), F32),
                        pltpu.VMEM((TOPK, LANES), F32)],
        compiler_params=_params(("parallel",)),
        name="peer_retrieve",
    )(n2, w["peer_wq"], w["peer_sk"])


def _twice_gelu(x):
    k0 = math.sqrt(2.0 / math.pi)
    return x * (1.0 + jnp.tanh(x * (k0 + (k0 * 0.044715) * (x * x))))


def _row_tile(ref, tt, h, g, ii):
    row = jnp.broadcast_to(ref[tt, h, g, ii:ii + 1, :], (SUBLANES, LANES))
    packed = pltpu.bitcast(row, BF16)
    return jnp.concatenate([packed] * (N_KEYS // packed.shape[0]), axis=0)


def _dense_body(n2_ref, u_ref, vt_ref, r1_ref, b_ref, a_ref, c_ref, o_ref, acc_s, gate_s, r1_s, b_s, *, tb):
    g = pl.program_id(1)

    @pl.when(g == 0)
    def _():
        acc_s[...] = jnp.zeros(acc_s.shape, F32)
        r1_s[...] = r1_ref[...]
        b_s[...] = b_ref[...]

    rows_per_load = 2
    for tt in range(tb // LANES):
        lanes = slice(tt * LANES, (tt + 1) * LANES)
        for ii0 in range(0, SUBLANES, rows_per_load):
            gates = [None] * rows_per_load
            for h in range(PEER_HEADS):
                rank_tile = r1_s[tt, h]
                weight_tile = b_s[tt, h]
                for k in range(rows_per_load):
                    wgt = weight_tile * _row_tile(a_ref, tt, h, g, ii0 + k)
                    term = jnp.where(rank_tile < _row_tile(c_ref, tt, h, g, ii0 + k), wgt, jnp.zeros_like(wgt))
                    gates[k] = term if gates[k] is None else gates[k] + term
            for k in range(rows_per_load):
                gate_s[(ii0 + k) * N_KEYS:(ii0 + k + 1) * N_KEYS, lanes] = gates[k]

    hid = _dot_t(u_ref[...], n2_ref[...])
    weighted = gate_s[...] * _twice_gelu(hid.astype(BF16))
    acc_s[...] += _dot(vt_ref[0], weighted)

    @pl.when(g == pl.num_programs(1) - 1)
    def _():
        o_ref[...] = acc_s[...].T


def _dense(n2, u, vt, r1, b, a, c, tb):
    n = n2.shape[0]
    ec = EXPERT_CHUNK
    ng = N_EXPERTS // ec
    nt = tb // LANES
    once = pl.Buffered(1)
    maps = pl.BlockSpec((nt, PEER_HEADS, N_KEYS, LANES), lambda i, g: (i, 0, 0, 0), pipeline_mode=once)
    rows = pl.BlockSpec((nt, PEER_HEADS, ng, SUBLANES, LANES), lambda i, g: (i, 0, 0, 0, 0), pipeline_mode=once)
    return pl.pallas_call(
        functools.partial(_dense_body, tb=tb),
        grid=(n // tb, ng),
        in_specs=[pl.BlockSpec((tb, D_MODEL), lambda i, g: (i, 0), pipeline_mode=once),
                  pl.BlockSpec((ec, D_MODEL), lambda i, g: (g, 0)),
                  pl.BlockSpec((1, D_MODEL, ec), lambda i, g: (g, 0, 0)),
                  maps, maps, rows, rows],
        out_specs=pl.BlockSpec((tb, D_MODEL), lambda i, g: (i, 0)),
        out_shape=jax.ShapeDtypeStruct((n, D_MODEL), F32),
        scratch_shapes=[pltpu.VMEM((D_MODEL, tb), F32), pltpu.VMEM((ec, tb), BF16),
                        pltpu.VMEM((nt, PEER_HEADS, N_KEYS, LANES), BF16),
                        pltpu.VMEM((nt, PEER_HEADS, N_KEYS, LANES), BF16)],
        compiler_params=_params(("parallel", "arbitrary")),
        name="peer_dense",
    )(n2, u, vt, r1, b, a, c)


def _final_body(h_ref, peer_ref, p_ref, gple_ref, wg_ref, wp_ref, o_ref):
    h = h_ref[...] + peer_ref[...]
    zg = _dot(_rms(h, gple_ref[...]).astype(BF16), wg_ref[...])
    gate = 1.0 / (1.0 + jnp.exp(-zg))
    o_ref[...] = h + gate * _dot(p_ref[...].astype(BF16), wp_ref[...])


def _final(h, peer, p, w, tb):
    n = h.shape[0]
    row = lambda width: pl.BlockSpec((tb, width), lambda i: (i, 0))
    consts = [w["g_ple"], w["w_ple_gate"], w["w_ple_proj"]]
    return pl.pallas_call(
        _final_body,
        grid=(n // tb,),
        in_specs=[row(D_MODEL), row(D_MODEL), row(PLE_DIM)] + [_const_spec(a.shape) for a in consts],
        out_specs=row(D_MODEL),
        out_shape=jax.ShapeDtypeStruct((n, D_MODEL), F32),
        compiler_params=_params(("parallel",)),
        name="final",
    )(h, peer, p, *consts)


def _layer_weights(l, g_mix, w_in, b_f, qn_a, kn_a, qn_b, kn_b, w_up_a, w_up_b, w_out, g_ffn, peer_wq, peer_subkeys,
                   peer_u, peer_v, g_ple, w_ple_gate, w_ple_proj):
    o_f = 3 * W_MIX
    o_b = o_f + N_HEADS
    o_g = o_b + 3 * W_MIX
    wi = w_in[l]
    tile_heads = lambda g: jnp.tile(g[l].astype(F32), N_HEADS)[None, :]
    head_of = jnp.arange(W_MIX) // HEAD_DIM
    sk = peer_subkeys[l].astype(BF16)
    zeros = jnp.zeros_like(sk[:, 0])
    sk_pad = jnp.stack([jnp.concatenate([sk[:, 0], zeros], axis=-1), jnp.concatenate([zeros, sk[:, 1]], axis=-1)],
                       axis=1)
    return {
        "g_mix": g_mix[l][None, :],
        "w_a": wi[:, :o_f].astype(BF16),
        "w_f": jnp.pad(wi[:, o_f:o_b], ((0, 0), (0, LANES - N_HEADS))).astype(BF16),
        "w_b": wi[:, o_b:o_g].astype(BF16),
        "w_g": wi[:, o_g:].astype(BF16),
        "b_f": jnp.pad(b_f[l], (0, LANES - N_HEADS))[None, :],
        "qn_a": tile_heads(qn_a), "kn_a": tile_heads(kn_a), "qn_b": tile_heads(qn_b), "kn_b": tile_heads(kn_b),
        "msum": jnp.where(head_of[:, None] == head_of[None, :], 1.0 / HEAD_DIM, 0.0).astype(BF16),
        "w_up_a": w_up_a[l].astype(BF16), "w_up_b": w_up_b[l].astype(BF16), "w_out": w_out[l].astype(BF16),
        "g_ffn": g_ffn[l][None, :],
        "peer_wq": peer_wq[l].astype(BF16),
        "peer_sk": sk_pad,
        "peer_u": peer_u[l].astype(BF16),
        "peer_vt": peer_v[l].astype(BF16).reshape(-1, EXPERT_CHUNK, D_MODEL).transpose(0, 2, 1),
        "g_ple": g_ple[l][None, :],
        "w_ple_gate": w_ple_gate[l].astype(BF16),
        "w_ple_proj": w_ple_proj[l].astype(BF16),
    }


def _channel(x, ya, yb, ga, gb, p, w, tb, tb_dense):
    h1, n2 = _merge(x, ya, yb, ga, gb, w, tb)
    r1, b, a, c = _retrieve(n2, w, tb_dense)
    peer = _dense(n2, w["peer_u"], w["peer_vt"], r1, b, a, c, tb_dense)
    return _final(h1, peer, p, w, tb)


def _pad_rows(x, rows):
    return jnp.pad(x, ((0, 0), (0, rows - x.shape[1]), (0, 0)))


def kernel(x_prompt, x_sample, cache_a_k, cache_a_v, cache_a_logf, cache_b_k, cache_b_v, p_prompt, p_sample, g_mix, w_in, b_f, qn_a, kn_a, qn_b, kn_b, rel_bias_b, w_up_a, w_up_b, w_out, g_ffn, peer_wq, peer_subkeys, peer_u, peer_v, g_ple, w_ple_gate, w_ple_proj):
    depth = w_in.shape[0]
    bp, tp, _ = x_prompt.shape
    bs, ts, _ = x_sample.shape
    past = cache_a_k.shape[2]
    band_rows = min(BAND_PAST, tp)
    tq = 256
    hp = x_prompt.reshape(bp * tp, D_MODEL)
    hs = x_sample.reshape(bs * ts, D_MODEL)
    outs = [[] for _ in range(10)]
    for l in range(depth):
        w = _layer_weights(l, g_mix, w_in, b_f, qn_a, kn_a, qn_b, kn_b, w_up_a, w_up_b, w_out, g_ffn, peer_wq,
                           peer_subkeys, peer_u, peer_v, g_ple, w_ple_gate, w_ple_proj)
        qa, ka, va, lf, qb, kb, vb, ga, gb = _proj(hp, w, 256)
        as_seq = lambda z: z.reshape(bp, tp, z.shape[-1])
        ya = _fox_prompt(as_seq(qa), as_seq(ka), as_seq(va), as_seq(lf).transpose(0, 2, 1), tq)
        yb = _band_prompt(as_seq(qb), as_seq(kb), as_seq(vb), _band_bias_prompt(rel_bias_b[l], tq), tq)
        hp = _channel(hp, ya.reshape(-1, W_MIX), yb.reshape(-1, W_MIX), ga, gb, p_prompt[l].reshape(-1, PLE_DIM), w,
                      256, 1024)
        heads = lambda z, b_, t_: z.reshape(b_, t_, N_HEADS, HEAD_DIM)
        outs[0].append(heads(ka, bp, tp)); outs[1].append(heads(va, bp, tp)); outs[2].append(as_seq(lf))
        outs[3].append(heads(kb, bp, tp)[:, -band_rows:]); outs[4].append(heads(vb, bp, tp)[:, -band_rows:])
        qa, ka, va, lf, qb, kb, vb, ga, gb = _proj(hs, w, bs * ts)
        as_seq = lambda z: z.reshape(bs, ts, z.shape[-1])
        lft = jnp.concatenate([cache_a_logf[l].astype(F32), as_seq(lf),
                               jnp.zeros((bs, LANES - ts, N_HEADS), F32)], axis=1).transpose(0, 2, 1)
        flat_cache = lambda z: z[l].reshape(bs, z.shape[2], W_MIX)
        ya = _fox_sample(as_seq(qa), flat_cache(cache_a_k), flat_cache(cache_a_v), _pad_rows(as_seq(ka), LANES),
                         _pad_rows(as_seq(va), LANES), lft, ts)
        yb = _band_sample(as_seq(qb), flat_cache(cache_b_k), flat_cache(cache_b_v), _pad_rows(as_seq(kb), LANES),
                          _pad_rows(as_seq(vb), LANES), _band_bias_sample(rel_bias_b[l], ts, cache_b_k.shape[2]))
        hs = _channel(hs, ya.reshape(-1, W_MIX), yb.reshape(-1, W_MIX), ga, gb, p_sample[l].reshape(-1, PLE_DIM), w,
                      bs * ts, bs * ts)
        outs[5].append(heads(ka, bs, ts)); outs[6].append(heads(va, bs, ts)); outs[7].append(as_seq(lf))
        outs[8].append(heads(kb, bs, ts)); outs[9].append(heads(vb, bs, ts))
    return (hp.reshape(bp, tp, D_MODEL), hs.reshape(bs, ts, D_MODEL)) + tuple(jnp.stack(o) for o in outs)
```

```python
import functools
import math

import jax
import jax.numpy as jnp
from jax import lax
from jax.experimental import pallas as pl
from jax.experimental.pallas import tpu as pltpu

F32 = jnp.float32
BF16 = jnp.bfloat16

D_MODEL = 1024
HEAD_DIM = 64
N_HEADS = 8
W_MIX = N_HEADS * HEAD_DIM
PAIR = 2 * HEAD_DIM
N_PAIRS = N_HEADS // 2
CHUNK = 64
BAND_PAST = 8 * CHUNK
MAX_REL = 128
PLE_DIM = 256
PEER_HEADS = 8
N_KEYS = 128
N_EXPERTS = N_KEYS * N_KEYS
TOPK = 16
RMS_EPS = 1e-6
ATT_SCALE = HEAD_DIM ** -0.5
NEG = -1e30
RANK_SENTINEL = 2.0 ** 100
FOX_KEY_CHUNK = 512
LANES = 128
SUBLANES = 8
EXPERT_CHUNK = SUBLANES * N_KEYS
VMEM_LIMIT = 56 * 1024 * 1024

N_CAND = 16 + 7 * 8 + 8


def _params(sem, vmem=VMEM_LIMIT):
    return pltpu.CompilerParams(dimension_semantics=sem, vmem_limit_bytes=vmem)


def _rms(x, g):
    return x * lax.rsqrt(jnp.mean(x * x, axis=-1, keepdims=True) + RMS_EPS) * g


def _dot(a, b):
    return jnp.dot(a, b, preferred_element_type=F32)


def _dot_t(a, b):
    return lax.dot_general(a, b, (((1,), (1,)), ((), ())), preferred_element_type=F32)


def _const_spec(shape):
    nd = len(shape)
    return pl.BlockSpec(shape, lambda *_: (0,) * nd)


def _proj_body(x_ref, g_ref, wa_ref, wf_ref, wb_ref, wg_ref, bf_ref, qna_ref, kna_ref, qnb_ref, knb_ref, msum_ref,
               qa_ref, ka_ref, va_ref, lf_ref, qb_ref, kb_ref, vb_ref, ga_ref, gb_ref):
    n1 = _rms(x_ref[...], g_ref[...]).astype(BF16)

    def head_norm(z, gain):
        ms = _dot((z * z).astype(BF16), msum_ref[...])
        return z * lax.rsqrt(ms + RMS_EPS) * gain

    za = _dot(n1, wa_ref[...])
    qa_ref[...] = (head_norm(za[:, :W_MIX], qna_ref[...]) * ATT_SCALE).astype(BF16)
    ka_ref[...] = head_norm(za[:, W_MIX:2 * W_MIX], kna_ref[...])
    va_ref[...] = za[:, 2 * W_MIX:]
    zb = _dot(n1, wb_ref[...])
    qb_ref[...] = (head_norm(zb[:, :W_MIX], qnb_ref[...]) * ATT_SCALE).astype(BF16)
    kb_ref[...] = head_norm(zb[:, W_MIX:2 * W_MIX], knb_ref[...])
    vb_ref[...] = zb[:, 2 * W_MIX:]
    fl = _dot(n1, wf_ref[...]) + bf_ref[...]
    ls = jnp.minimum(fl, 0.0) - jnp.log1p(jnp.exp(-jnp.abs(fl)))
    lf_ref[...] = ls[:, :N_HEADS]
    zg = _dot(n1, wg_ref[...])
    sg = 1.0 / (1.0 + jnp.exp(-zg))
    ga_ref[...] = sg[:, :D_MODEL].astype(BF16)
    gb_ref[...] = sg[:, D_MODEL:].astype(BF16)


def _proj(x, w, tb):
    n = x.shape[0]
    row = lambda width: pl.BlockSpec((tb, width), lambda i: (i, 0))
    ins = [x, w["g_mix"], w["w_a"], w["w_f"], w["w_b"], w["w_g"], w["b_f"], w["qn_a"], w["kn_a"], w["qn_b"],
           w["kn_b"], w["msum"]]
    in_specs = [row(D_MODEL)] + [_const_spec(a.shape) for a in ins[1:]]
    widths = [(W_MIX, BF16), (W_MIX, F32), (W_MIX, F32), (N_HEADS, F32), (W_MIX, BF16), (W_MIX, F32), (W_MIX, F32),
              (D_MODEL, BF16), (D_MODEL, BF16)]
    return pl.pallas_call(
        _proj_body,
        grid=(n // tb,),
        in_specs=in_specs,
        out_specs=[row(wd) for wd, _ in widths],
        out_shape=[jax.ShapeDtypeStruct((n, wd), dt) for wd, dt in widths],
        compiler_params=_params(("parallel",)),
        name="proj",
    )(*ins)


def _cumsum_lanes(x):
    n = x.shape[-1]
    lane = lax.broadcasted_iota(jnp.int32, x.shape, x.ndim - 1)
    s = 1
    while s < n:
        x = x + jnp.where(lane >= s, pltpu.roll(x, s, axis=x.ndim - 1), 0.0)
        s *= 2
    return x


def _head_of_pair(x, hh):
    lane = lax.broadcasted_iota(jnp.int32, (1, PAIR), 1)
    keep = (lane < HEAD_DIM) if hh == 0 else (lane >= HEAD_DIM)
    return jnp.where(keep, x, jnp.zeros_like(x))


def _merge_pair(o0, o1):
    lane = lax.broadcasted_iota(jnp.int32, (1, PAIR), 1)
    return jnp.where(lane < HEAD_DIM, o0, o1)


def _fox_body(q_ref, k_ref, v_ref, lft_ref, o_ref, kb_s, vb_s, c_s, *, tq, nq):
    hp = pl.program_id(1)
    qi = pl.program_id(2)

    @pl.when(qi == 0)
    def _():
        kb_s[...] = k_ref[0].astype(BF16)
        vb_s[...] = v_ref[0].astype(BF16)
        c_s[...] = _cumsum_lanes(lft_ref[0])

    q = q_ref[0]
    q2 = jnp.concatenate([_head_of_pair(q, 0), _head_of_pair(q, 1)], axis=0)
    row = lax.broadcasted_iota(jnp.int32, (tq, tq), 0)
    col = lax.broadcasted_iota(jnp.int32, (tq, tq), 1)

    def tile(n_blocks):
        keys = n_blocks * tq
        parts = [[], []]
        for start in range(0, keys, FOX_KEY_CHUNK):
            size = min(FOX_KEY_CHUNK, keys - start)
            s2 = _dot_t(q2, kb_s[start:start + size, :])
            probs, stats = [], []
            for hh in range(2):
                s = s2[hh * tq:(hh + 1) * tq] - c_s[pl.ds(2 * hp + hh, 1), start:start + size]
                if start + size == keys:
                    diag = jnp.where(col <= row, s[:, size - tq:], -jnp.inf)
                    s = diag if size == tq else jnp.concatenate([s[:, :size - tq], diag], axis=1)
                m = jnp.max(s, axis=-1, keepdims=True)
                p = jnp.exp(s - m)
                stats.append((m, jnp.sum(p, axis=-1, keepdims=True)))
                probs.append(p.astype(BF16))
            o2 = _dot(jnp.concatenate(probs, axis=0), vb_s[start:start + size, :])
            for hh in range(2):
                parts[hh].append(stats[hh] + (o2[hh * tq:(hh + 1) * tq],))
        outs = []
        for hh in range(2):
            m = functools.reduce(jnp.maximum, [mc for mc, _, _ in parts[hh]])
            scales = [jnp.exp(mc - m) for mc, _, _ in parts[hh]]
            l = sum(lc * sc for (_, lc, _), sc in zip(parts[hh], scales))
            o = sum(oc * sc for (_, _, oc), sc in zip(parts[hh], scales))
            outs.append(o / l)
        o_ref[0] = _merge_pair(outs[0], outs[1]).astype(o_ref.dtype)

    for n_blocks in range(1, nq + 1):
        pl.when(qi == n_blocks - 1)(functools.partial(tile, n_blocks))


def _fox_prompt(q, k, v, lft, tq):
    b, t, _ = q.shape
    nq = t // tq
    return pl.pallas_call(
        functools.partial(_fox_body, tq=tq, nq=nq),
        grid=(b, N_PAIRS, nq),
        in_specs=[
            pl.BlockSpec((1, tq, PAIR), lambda i, p, j: (i, j, p)),
            pl.BlockSpec((1, t, PAIR), lambda i, p, j: (i, 0, p)),
            pl.BlockSpec((1, t, PAIR), lambda i, p, j: (i, 0, p)),
            pl.BlockSpec((1, N_HEADS, t), lambda i, p, j: (i, 0, 0)),
        ],
        out_specs=pl.BlockSpec((1, tq, PAIR), lambda i, p, j: (i, j, p)),
        out_shape=jax.ShapeDtypeStruct((b, t, W_MIX), BF16),
        scratch_shapes=[pltpu.VMEM((t, PAIR), BF16), pltpu.VMEM((t, PAIR), BF16), pltpu.VMEM((N_HEADS, t), F32)],
        compiler_params=_params(("parallel", "parallel", "arbitrary")),
        name="fox_prompt",
    )(q, k, v, lft)


def _band_body(q_ref, k_ref, v_ref, bias_ref, o_ref, kp_s, vp_s, *, tq):
    qi = pl.program_id(1)
    win = tq + BAND_PAST

    @pl.when(qi == 0)
    def _():
        zeros = jnp.zeros((BAND_PAST, W_MIX), BF16)
        kp_s[:BAND_PAST, :] = zeros
        vp_s[:BAND_PAST, :] = zeros
        kp_s[BAND_PAST:, :] = k_ref[0].astype(BF16)
        vp_s[BAND_PAST:, :] = v_ref[0].astype(BF16)

    off = pl.multiple_of(qi * tq, tq)
    exists = lax.broadcasted_iota(jnp.int32, (1, win), 1) >= BAND_PAST - qi * tq
    for pair in range(N_PAIRS):
        lanes = slice(pair * PAIR, (pair + 1) * PAIR)
        kw = kp_s[pl.ds(off, win), lanes]
        vw = vp_s[pl.ds(off, win), lanes]
        q = q_ref[0, :, lanes]
        outs = []
        for hh in range(2):
            s = _dot_t(_head_of_pair(q, hh), kw) + bias_ref[2 * pair + hh]
            s = jnp.where(exists, s, NEG)
            m = jnp.max(s, axis=-1, keepdims=True)
            p = jnp.exp(s - m)
            l = jnp.sum(p, axis=-1, keepdims=True)
            outs.append(_dot(p.astype(BF16), vw) / l)
        o_ref[0, :, lanes] = _merge_pair(outs[0], outs[1]).astype(o_ref.dtype)


def _band_prompt(q, k, v, bias, tq):
    b, t, _ = q.shape
    win = tq + BAND_PAST
    return pl.pallas_call(
        functools.partial(_band_body, tq=tq),
        grid=(b, t // tq),
        in_specs=[
            pl.BlockSpec((1, tq, W_MIX), lambda i, j: (i, j, 0)),
            pl.BlockSpec((1, t, W_MIX), lambda i, j: (i, 0, 0)),
            pl.BlockSpec((1, t, W_MIX), lambda i, j: (i, 0, 0)),
            _const_spec((N_HEADS, tq, win)),
        ],
        out_specs=pl.BlockSpec((1, tq, W_MIX), lambda i, j: (i, j, 0)),
        out_shape=jax.ShapeDtypeStruct((b, t, W_MIX), BF16),
        scratch_shapes=[pltpu.VMEM((t + BAND_PAST, W_MIX), BF16), pltpu.VMEM((t + BAND_PAST, W_MIX), BF16)],
        compiler_params=_params(("parallel", "arbitrary")),
        name="band_prompt",
    )(q, k, v, bias)


def _toeplitz(w, n, m):
    heads, span = w.shape
    hankel = jnp.tile(w, (1, n + 1))[:, :n * (span + 1)].reshape(heads, n, span + 1)[:, :, :m]
    return hankel[:, ::-1, :]


def _band_bias_prompt(rel_bias, tq):
    win = tq + BAND_PAST
    rel = jnp.arange(tq + win - 1) - (tq - 1) - BAND_PAST
    table = _toeplitz(rel_bias[:, jnp.clip(rel, -MAX_REL, MAX_REL) + MAX_REL].astype(F32), tq, win)
    ii = jnp.arange(tq)[:, None]
    jj = jnp.arange(win)[None, :]
    lo = (ii // CHUNK) * CHUNK
    in_band = (jj >= lo) & (jj < lo + BAND_PAST + CHUNK)
    return jnp.where(in_band[None], table, NEG)


def _fox_sample_body(q_ref, kc_ref, vc_ref, kn_ref, vn_ref, lft_ref, o_ref, *, past, n_new):
    hp = pl.program_id(1)
    c = _cumsum_lanes(lft_ref[0])
    kc = kc_ref[0].astype(BF16)
    vc = vc_ref[0].astype(BF16)
    kn = kn_ref[0].astype(BF16)
    vn = vn_ref[0].astype(BF16)
    q = q_ref[0]
    nq = q.shape[0]
    row = lax.broadcasted_iota(jnp.int32, (nq, LANES), 0)
    col = lax.broadcasted_iota(jnp.int32, (nq, LANES), 1)
    outs = []
    for hh in range(2):
        qh = _head_of_pair(q, hh)
        sel = lax.broadcasted_iota(jnp.int32, (N_HEADS, 1), 0) == 2 * hp + hh
        crow = jnp.sum(jnp.where(sel, c, 0.0), axis=0, keepdims=True)
        sc = _dot_t(qh, kc) - crow[:, :past]
        sn = _dot_t(qh, kn) - crow[:, past:]
        sn = jnp.where((col <= row) & (col < n_new), sn, -jnp.inf)
        m = jnp.maximum(jnp.max(sc, axis=-1, keepdims=True), jnp.max(sn, axis=-1, keepdims=True))
        pc = jnp.exp(sc - m)
        pn = jnp.exp(sn - m)
        l = jnp.sum(pc, axis=-1, keepdims=True) + jnp.sum(pn, axis=-1, keepdims=True)
        outs.append((_dot(pc.astype(BF16), vc) + _dot(pn.astype(BF16), vn)) / l)
    o_ref[0] = _merge_pair(outs[0], outs[1]).astype(o_ref.dtype)


def _fox_sample(q, kc, vc, kn, vn, lft, n_new):
    b, nq, _ = q.shape
    past = kc.shape[1]
    pair_spec = lambda rows: pl.BlockSpec((1, rows, PAIR), lambda i, p: (i, 0, p))
    return pl.pallas_call(
        functools.partial(_fox_sample_body, past=past, n_new=n_new),
        grid=(b, N_PAIRS),
        in_specs=[pair_spec(nq), pair_spec(past), pair_spec(past), pair_spec(LANES), pair_spec(LANES),
                  pl.BlockSpec((1, N_HEADS, past + LANES), lambda i, p: (i, 0, 0))],
        out_specs=pair_spec(nq),
        out_shape=jax.ShapeDtypeStruct((b, nq, W_MIX), BF16),
        compiler_params=_params(("parallel", "parallel")),
        name="fox_sample",
    )(q, kc, vc, kn, vn, lft)


def _band_sample_body(q_ref, kc_ref, vc_ref, kn_ref, vn_ref, bias_ref, o_ref, *, past):
    kc = kc_ref[0].astype(BF16)
    vc = vc_ref[0].astype(BF16)
    kn = kn_ref[0].astype(BF16)
    vn = vn_ref[0].astype(BF16)
    q = q_ref[0]
    outs = []
    for hh in range(2):
        qh = _head_of_pair(q, hh)
        bias = bias_ref[hh]
        sc = _dot_t(qh, kc) + bias[:, :past]
        sn = _dot_t(qh, kn) + bias[:, past:]
        m = jnp.maximum(jnp.max(sc, axis=-1, keepdims=True), jnp.max(sn, axis=-1, keepdims=True))
        pc = jnp.exp(sc - m)
        pn = jnp.exp(sn - m)
        l = jnp.sum(pc, axis=-1, keepdims=True) + jnp.sum(pn, axis=-1, keepdims=True)
        outs.append((_dot(pc.astype(BF16), vc) + _dot(pn.astype(BF16), vn)) / l)
    o_ref[0] = _merge_pair(outs[0], outs[1]).astype(o_ref.dtype)


def _band_sample(q, kc, vc, kn, vn, bias):
    b, nq, _ = q.shape
    past = kc.shape[1]
    pair_spec = lambda rows: pl.BlockSpec((1, rows, PAIR), lambda i, p: (i, 0, p))
    return pl.pallas_call(
        functools.partial(_band_sample_body, past=past),
        grid=(b, N_PAIRS),
        in_specs=[pair_spec(nq), pair_spec(past), pair_spec(past), pair_spec(LANES), pair_spec(LANES),
                  pl.BlockSpec((2, nq, past + LANES), lambda i, p: (p, 0, 0))],
        out_specs=pair_spec(nq),
        out_shape=jax.ShapeDtypeStruct((b, nq, W_MIX), BF16),
        compiler_params=_params(("parallel", "parallel")),
        name="band_sample",
    )(q, kc, vc, kn, vn, bias)


def _band_bias_sample(rel_bias, n_new, past):
    ii = jnp.arange(n_new)[:, None]
    jj = jnp.arange(past + LANES)[None, :]
    table = rel_bias[:, jnp.clip(jj - past - ii, -MAX_REL, MAX_REL) + MAX_REL].astype(F32)
    return jnp.where((jj < past + n_new)[None], table, NEG)


def _merge_body(x_ref, ya_ref, yb_ref, ga_ref, gb_ref, wua_ref, wub_ref, wo_ref, gffn_ref, h_ref, n2_ref):
    merged = (ga_ref[...].astype(F32) * _dot(ya_ref[...], wua_ref[...])
              + gb_ref[...].astype(F32) * _dot(yb_ref[...], wub_ref[...]))
    h = x_ref[...] + _dot(merged.astype(BF16), wo_ref[...])
    h_ref[...] = h
    n2_ref[...] = _rms(h, gffn_ref[...]).astype(BF16)


def _merge(x, ya, yb, ga, gb, w, tb):
    n = x.shape[0]
    row = lambda width: pl.BlockSpec((tb, width), lambda i: (i, 0))
    consts = [w["w_up_a"], w["w_up_b"], w["w_out"], w["g_ffn"]]
    return pl.pallas_call(
        _merge_body,
        grid=(n // tb,),
        in_specs=[row(D_MODEL), row(W_MIX), row(W_MIX), row(D_MODEL), row(D_MODEL)]
        + [_const_spec(a.shape) for a in consts],
        out_specs=[row(D_MODEL), row(D_MODEL)],
        out_shape=[jax.ShapeDtypeStruct((n, D_MODEL), F32), jax.ShapeDtypeStruct((n, D_MODEL), BF16)],
        compiler_params=_params(("parallel",)),
        name="merge",
    )(x, ya, yb, ga, gb, *consts)


def _leave(work, exact):
    hit = work == jnp.max(work, axis=0, keepdims=True)
    if exact:
        idx = lax.broadcasted_iota(jnp.int32, work.shape, 0)
        hit = idx == jnp.min(jnp.where(hit, idx, work.shape[0]), axis=0, keepdims=True)
    return hit


def _top16(s, vals_ref, exact):
    work = s
    for r in range(TOPK):
        vals_ref[r:r + 1, :] = jnp.max(work, axis=0, keepdims=True)
        work = jnp.where(_leave(work, exact), -RANK_SENTINEL * (1.0 + r / 32.0), work)
    return jnp.where(work <= -RANK_SENTINEL, work * (-32.0 / RANK_SENTINEL) - 32.0, float(TOPK))


def _count(mask):
    return jnp.sum(mask.astype(F32), axis=0, keepdims=True)


def _pair_bf16(x):
    bits = lax.bitcast_convert_type(x.astype(BF16).astype(F32), jnp.uint32)
    return bits | (bits >> 16)


def _retrieve_tile(s_s, tt, r1_ref, b_ref, a_ref, c_ref, va_s, vb_s, exact):
    off = jnp.zeros((1, LANES), F32)
    for h in range(PEER_HEADS):
        s0 = s_s[2 * h, tt]
        s1 = s_s[2 * h + 1, tt]
        rank0 = _top16(s0, va_s, exact)
        rank1 = _top16(s1, vb_s, exact)
        va = va_s[...]
        vb = vb_s[...]
        cand = jnp.concatenate([va[0:1] + vb] + [va[k:k + 1] + vb[0:8] for k in range(1, 8)] + [va[8:16] + vb[0:1]],
                               axis=0)
        work = cand
        picked = jnp.zeros(cand.shape, jnp.bool_)
        for _ in range(TOPK):
            hit = _leave(work, exact)
            picked = picked | hit
            work = jnp.where(hit, -jnp.inf, work)
        top = va[0:1] + vb[0:1]
        z = jnp.sum(jnp.where(picked, jnp.exp(cand - top), 0.0), axis=0, keepdims=True)
        pickf = picked.astype(F32)
        counts = [jnp.sum(pickf[0:16], axis=0, keepdims=True)]
        counts += [jnp.sum(pickf[8 + 8 * k:16 + 8 * k], axis=0, keepdims=True) for k in range(1, 8)]
        counts += [pickf[72 + k:73 + k] for k in range(8)]
        c = jnp.zeros(s0.shape, F32)
        for k in range(TOPK):
            c = jnp.where(rank0 == float(k), counts[k], c)
        a = jnp.where(rank0 < float(TOPK), jnp.exp(s0 - va[0:1]) * (0.5 / z), 0.0)
        b = jnp.where(rank1 < float(TOPK), jnp.exp(s1 - vb[0:1]), 0.0)
        r1_ref[tt, h] = rank1.astype(BF16)
        b_ref[tt, h] = b.astype(BF16)
        a_rows = _pair_bf16(a)
        c_rows = _pair_bf16(c)
        for grp in range(N_KEYS // SUBLANES):
            rows = slice(grp * SUBLANES, (grp + 1) * SUBLANES)
            a_ref[tt, h, grp] = a_rows[rows]
            c_ref[tt, h, grp] = c_rows[rows]
        if not exact:
            for n_left in (_count(rank0 < float(TOPK)), _count(rank1 < float(TOPK)), _count(picked)):
                off = off + jnp.abs(n_left - float(TOPK))
    return off


def _retrieve_body(n2_ref, wq_ref, sk_ref, r1_ref, b_ref, a_ref, c_ref, s_s, va_s, vb_s, *, tb):
    q = _dot(n2_ref[...], wq_ref[...]).astype(BF16)
    for h in range(PEER_HEADS):
        qh = q[:, h * PAIR:(h + 1) * PAIR]
        for half in range(2):
            s = _dot_t(sk_ref[h, half], qh)
            for tt in range(tb // LANES):
                s_s[2 * h + half, tt] = s[:, tt * LANES:(tt + 1) * LANES]

    def tile(tt, carry):
        maps = (r1_ref, b_ref, a_ref, c_ref, va_s, vb_s)
        off = _retrieve_tile(s_s, tt, *maps, exact=False)

        @pl.when(jnp.max(off) > 0.0)
        def _():
            _retrieve_tile(s_s, tt, *maps, exact=True)

        return carry

    lax.fori_loop(0, tb // LANES, tile, 0)


def _retrieve(n2, w, tb):
    n = n2.shape[0]
    nt = tb // LANES
    maps = pl.BlockSpec((nt, PEER_HEADS, N_KEYS, LANES), lambda i: (i, 0, 0, 0))
    rows = pl.BlockSpec((nt, PEER_HEADS, N_KEYS // SUBLANES, SUBLANES, LANES), lambda i: (i, 0, 0, 0, 0))
    map_shape = (n // LANES, PEER_HEADS, N_KEYS, LANES)
    row_shape = (n // LANES, PEER_HEADS, N_KEYS // SUBLANES, SUBLANES, LANES)
    return pl.pallas_call(
        functools.partial(_retrieve_body, tb=tb),
        grid=(n // tb,),
        in_specs=[pl.BlockSpec((tb, D_MODEL), lambda i: (i, 0)), _const_spec(w["peer_wq"].shape),
                  _const_spec(w["peer_sk"].shape)],
        out_specs=[maps, maps, rows, rows],
        out_shape=[jax.ShapeDtypeStruct(map_shape, BF16), jax.ShapeDtypeStruct(map_shape, BF16),
                   jax.ShapeDtypeStruct(row_shape, jnp.uint32), jax.ShapeDtypeStruct(row_shape, jnp.uint32)],
        scratch_shapes=[pltpu.VMEM((2 * PEER_HEADS, nt, N_KEYS, LANES), F32), pltpu.VMEM((TOPK, LANES), F32),
                        pltpu.VMEM((TOPK, LANES), F32)],
        compiler_params=_params(("parallel",)),
        name="peer_retrieve",
    )(n2, w["peer_wq"], w["peer_sk"])


def _twice_gelu(x):
    k0 = math.sqrt(2.0 / math.pi)
    return x * (1.0 + jnp.tanh(x * (k0 + (k0 * 0.044715) * (x * x))))


def _row_tile(ref, tt, h, g, ii):
    row = jnp.broadcast_to(ref[tt, h, g, ii:ii + 1, :], (SUBLANES, LANES))
    packed = pltpu.bitcast(row, BF16)
    return jnp.concatenate([packed] * (N_KEYS // packed.shape[0]), axis=0)


def _dense_body(n2_ref, u_ref, vt_ref, r1_ref, b_ref, a_ref, c_ref, o_ref, acc_s, gate_s, r1_s, b_s, *, tb):
    g = pl.program_id(1)

    @pl.when(g == 0)
    def _():
        acc_s[...] = jnp.zeros(acc_s.shape, F32)
        r1_s[...] = r1_ref[...]
        b_s[...] = b_ref[...]

    rows_per_load = 2
    for tt in range(tb // LANES):
        lanes = slice(tt * LANES, (tt + 1) * LANES)
        for ii0 in range(0, SUBLANES, rows_per_load):
            gates = [None] * rows_per_load
            for h in range(PEER_HEADS):
                rank_tile = r1_s[tt, h]
                weight_tile = b_s[tt, h]
                for k in range(rows_per_load):
                    wgt = weight_tile * _row_tile(a_ref, tt, h, g, ii0 + k)
                    term = jnp.where(rank_tile < _row_tile(c_ref, tt, h, g, ii0 + k), wgt, jnp.zeros_like(wgt))
                    gates[k] = term if gates[k] is None else gates[k] + term
            for k in range(rows_per_load):
                gate_s[(ii0 + k) * N_KEYS:(ii0 + k + 1) * N_KEYS, lanes] = gates[k]

    hid = _dot_t(u_ref[...], n2_ref[...])
    weighted = gate_s[...] * _twice_gelu(hid.astype(BF16))
    acc_s[...] += _dot(vt_ref[0], weighted)

    @pl.when(g == pl.num_programs(1) - 1)
    def _():
        o_ref[...] = acc_s[...].T


def _dense(n2, u, vt, r1, b, a, c, tb):
    n = n2.shape[0]
    ec = EXPERT_CHUNK
    ng = N_EXPERTS // ec
    nt = tb // LANES
    once = pl.Buffered(1)
    maps = pl.BlockSpec((nt, PEER_HEADS, N_KEYS, LANES), lambda i, g: (i, 0, 0, 0), pipeline_mode=once)
    rows = pl.BlockSpec((nt, PEER_HEADS, ng, SUBLANES, LANES), lambda i, g: (i, 0, 0, 0, 0), pipeline_mode=once)
    return pl.pallas_call(
        functools.partial(_dense_body, tb=tb),
        grid=(n // tb, ng),
        in_specs=[pl.BlockSpec((tb, D_MODEL), lambda i, g: (i, 0), pipeline_mode=once),
                  pl.BlockSpec((ec, D_MODEL), lambda i, g: (g, 0)),
                  pl.BlockSpec((1, D_MODEL, ec), lambda i, g: (g, 0, 0)),
                  maps, maps, rows, rows],
        out_specs=pl.BlockSpec((tb, D_MODEL), lambda i, g: (i, 0)),
        out_shape=jax.ShapeDtypeStruct((n, D_MODEL), F32),
        scratch_shapes=[pltpu.VMEM((D_MODEL, tb), F32), pltpu.VMEM((ec, tb), BF16),
                        pltpu.VMEM((nt, PEER_HEADS, N_KEYS, LANES), BF16),
                        pltpu.VMEM((nt, PEER_HEADS, N_KEYS, LANES), BF16)],
        compiler_params=_params(("parallel", "arbitrary")),
        name="peer_dense",
    )(n2, u, vt, r1, b, a, c)


def _final_body(h_ref, peer_ref, p_ref, gple_ref, wg_ref, wp_ref, o_ref):
    h = h_ref[...] + peer_ref[...]
    zg = _dot(_rms(h, gple_ref[...]).astype(BF16), wg_ref[...])
    gate = 1.0 / (1.0 + jnp.exp(-zg))
    o_ref[...] = h + gate * _dot(p_ref[...].astype(BF16), wp_ref[...])


def _final(h, peer, p, w, tb):
    n = h.shape[0]
    row = lambda width: pl.BlockSpec((tb, width), lambda i: (i, 0))
    consts = [w["g_ple"], w["w_ple_gate"], w["w_ple_proj"]]
    return pl.pallas_call(
        _final_body,
        grid=(n // tb,),
        in_specs=[row(D_MODEL), row(D_MODEL), row(PLE_DIM)] + [_const_spec(a.shape) for a in consts],
        out_specs=row(D_MODEL),
        out_shape=jax.ShapeDtypeStruct((n, D_MODEL), F32),
        compiler_params=_params(("parallel",)),
        name="final",
    )(h, peer, p, *consts)


def _layer_weights(l, g_mix, w_in, b_f, qn_a, kn_a, qn_b, kn_b, w_up_a, w_up_b, w_out, g_ffn, peer_wq, peer_subkeys,
                   peer_u, peer_v, g_ple, w_ple_gate, w_ple_proj):
    o_f = 3 * W_MIX
    o_b = o_f + N_HEADS
    o_g = o_b + 3 * W_MIX
    wi = w_in[l]
    tile_heads = lambda g: jnp.tile(g[l].astype(F32), N_HEADS)[None, :]
    head_of = jnp.arange(W_MIX) // HEAD_DIM
    sk = peer_subkeys[l].astype(BF16)
    zeros = jnp.zeros_like(sk[:, 0])
    sk_pad = jnp.stack([jnp.concatenate([sk[:, 0], zeros], axis=-1), jnp.concatenate([zeros, sk[:, 1]], axis=-1)],
                       axis=1)
    return {
        "g_mix": g_mix[l][None, :],
        "w_a": wi[:, :o_f].astype(BF16),
        "w_f": jnp.pad(wi[:, o_f:o_b], ((0, 0), (0, LANES - N_HEADS))).astype(BF16),
        "w_b": wi[:, o_b:o_g].astype(BF16),
        "w_g": wi[:, o_g:].astype(BF16),
        "b_f": jnp.pad(b_f[l], (0, LANES - N_HEADS))[None, :],
        "qn_a": tile_heads(qn_a), "kn_a": tile_heads(kn_a), "qn_b": tile_heads(qn_b), "kn_b": tile_heads(kn_b),
        "msum": jnp.where(head_of[:, None] == head_of[None, :], 1.0 / HEAD_DIM, 0.0).astype(BF16),
        "w_up_a": w_up_a[l].astype(BF16), "w_up_b": w_up_b[l].astype(BF16), "w_out": w_out[l].astype(BF16),
        "g_ffn": g_ffn[l][None, :],
        "peer_wq": peer_wq[l].astype(BF16),
        "peer_sk": sk_pad,
        "peer_u": peer_u[l].astype(BF16),
        "peer_vt": peer_v[l].astype(BF16).reshape(-1, EXPERT_CHUNK, D_MODEL).transpose(0, 2, 1),
        "g_ple": g_ple[l][None, :],
        "w_ple_gate": w_ple_gate[l].astype(BF16),
        "w_ple_proj": w_ple_proj[l].astype(BF16),
    }


def _channel(x, ya, yb, ga, gb, p, w, tb, tb_dense):
    h1, n2 = _merge(x, ya, yb, ga, gb, w, tb)
    r1, b, a, c = _retrieve(n2, w, tb_dense)
    peer = _dense(n2, w["peer_u"], w["peer_vt"], r1, b, a, c, tb_dense)
    return _final(h1, peer, p, w, tb)


def _pad_rows(x, rows):
    return jnp.pad(x, ((0, 0), (0, rows - x.shape[1]), (0, 0)))


def kernel(x_prompt, x_sample, cache_a_k, cache_a_v, cache_a_logf, cache_b_k, cache_b_v, p_prompt, p_sample, g_mix, w_in, b_f, qn_a, kn_a, qn_b, kn_b, rel_bias_b, w_up_a, w_up_b, w_out, g_ffn, peer_wq, peer_subkeys, peer_u, peer_v, g_ple, w_ple_gate, w_ple_proj):
    depth = w_in.shape[0]
    bp, tp, _ = x_prompt.shape
    bs, ts, _ = x_sample.shape
    past = cache_a_k.shape[2]
    band_rows = min(BAND_PAST, tp)
    tq = 256
    hp = x_prompt.reshape(bp * tp, D_MODEL)
    hs = x_sample.reshape(bs * ts, D_MODEL)
    outs = [[] for _ in range(10)]
    for l in range(depth):
        w = _layer_weights(l, g_mix, w_in, b_f, qn_a, kn_a, qn_b, kn_b, w_up_a, w_up_b, w_out, g_ffn, peer_wq,
                           peer_subkeys, peer_u, peer_v, g_ple, w_ple_gate, w_ple_proj)
        qa, ka, va, lf, qb, kb, vb, ga, gb = _proj(hp, w, 256)
        as_seq = lambda z: z.reshape(bp, tp, z.shape[-1])
        ya = _fox_prompt(as_seq(qa), as_seq(ka), as_seq(va), as_seq(lf).transpose(0, 2, 1), tq)
        yb = _band_prompt(as_seq(qb), as_seq(kb), as_seq(vb), _band_bias_prompt(rel_bias_b[l], tq), tq)
        hp = _channel(hp, ya.reshape(-1, W_MIX), yb.reshape(-1, W_MIX), ga, gb, p_prompt[l].reshape(-1, PLE_DIM), w,
                      256, 1024)
        heads = lambda z, b_, t_: z.reshape(b_, t_, N_HEADS, HEAD_DIM)
        outs[0].append(heads(ka, bp, tp)); outs[1].append(heads(va, bp, tp)); outs[2].append(as_seq(lf))
        newest = lambda z: heads(as_seq(z)[:, -band_rows:], bp, band_rows)
        outs[3].append(newest(kb)); outs[4].append(newest(vb))
        qa, ka, va, lf, qb, kb, vb, ga, gb = _proj(hs, w, bs * ts)
        as_seq = lambda z: z.reshape(bs, ts, z.shape[-1])
        lft = jnp.concatenate([cache_a_logf[l].astype(F32), as_seq(lf),
                               jnp.zeros((bs, LANES - ts, N_HEADS), F32)], axis=1).transpose(0, 2, 1)
        flat_cache = lambda z: z[l].reshape(bs, z.shape[2], W_MIX)
        ya = _fox_sample(as_seq(qa), flat_cache(cache_a_k), flat_cache(cache_a_v), _pad_rows(as_seq(ka), LANES),
                         _pad_rows(as_seq(va), LANES), lft, ts)
        yb = _band_sample(as_seq(qb), flat_cache(cache_b_k), flat_cache(cache_b_v), _pad_rows(as_seq(kb), LANES),
                          _pad_rows(as_seq(vb), LANES), _band_bias_sample(rel_bias_b[l], ts, cache_b_k.shape[2]))
        hs = _channel(hs, ya.reshape(-1, W_MIX), yb.reshape(-1, W_MIX), ga, gb, p_sample[l].reshape(-1, PLE_DIM), w,
                      bs * ts, bs * ts)
        outs[5].append(heads(ka, bs, ts)); outs[6].append(heads(va, bs, ts)); outs[7].append(as_seq(lf))
        outs[8].append(heads(kb, bs, ts)); outs[9].append(heads(vb, bs, ts))
    return (hp.reshape(bp, tp, D_MODEL), hs.reshape(bs, ts, D_MODEL)) + tuple(jnp.stack(o) for o in outs)
```

```python
import functools
import math

import jax
import jax.numpy as jnp
from jax import lax
from jax.experimental import pallas as pl
from jax.experimental.pallas import tpu as pltpu

F32 = jnp.float32
BF16 = jnp.bfloat16

D_MODEL = 1024
HEAD_DIM = 64
N_HEADS = 8
W_MIX = N_HEADS * HEAD_DIM
PAIR = 2 * HEAD_DIM
N_PAIRS = N_HEADS // 2
CHUNK = 64
BAND_PAST = 8 * CHUNK
MAX_REL = 128
PLE_DIM = 256
PEER_HEADS = 8
N_KEYS = 128
N_EXPERTS = N_KEYS * N_KEYS
TOPK = 16
RMS_EPS = 1e-6
ATT_SCALE = HEAD_DIM ** -0.5
NEG = -1e30
RANK_SENTINEL = 2.0 ** 100
FOX_KEY_CHUNK = 512
LANES = 128
SUBLANES = 8
EXPERT_CHUNK = SUBLANES * N_KEYS
VMEM_LIMIT = 56 * 1024 * 1024

N_CAND = 16 + 7 * 8 + 8


def _params(sem, vmem=VMEM_LIMIT):
    return pltpu.CompilerParams(dimension_semantics=sem, vmem_limit_bytes=vmem)


def _rms(x, g):
    return x * lax.rsqrt(jnp.mean(x * x, axis=-1, keepdims=True) + RMS_EPS) * g


def _dot(a, b):
    return jnp.dot(a, b, preferred_element_type=F32)


def _dot_t(a, b):
    return lax.dot_general(a, b, (((1,), (1,)), ((), ())), preferred_element_type=F32)


def _const_spec(shape):
    nd = len(shape)
    return pl.BlockSpec(shape, lambda *_: (0,) * nd)


def _proj_body(x_ref, g_ref, wa_ref, wf_ref, wb_ref, wg_ref, bf_ref, qna_ref, kna_ref, qnb_ref, knb_ref, msum_ref,
               qa_ref, ka_ref, va_ref, lf_ref, qb_ref, kb_ref, vb_ref, ga_ref, gb_ref):
    n1 = _rms(x_ref[...], g_ref[...]).astype(BF16)

    def head_norm(z, gain):
        ms = _dot((z * z).astype(BF16), msum_ref[...])
        return z * lax.rsqrt(ms + RMS_EPS) * gain

    za = _dot(n1, wa_ref[...])
    qa_ref[...] = (head_norm(za[:, :W_MIX], qna_ref[...]) * ATT_SCALE).astype(BF16)
    ka_ref[...] = head_norm(za[:, W_MIX:2 * W_MIX], kna_ref[...])
    va_ref[...] = za[:, 2 * W_MIX:]
    zb = _dot(n1, wb_ref[...])
    qb_ref[...] = (head_norm(zb[:, :W_MIX], qnb_ref[...]) * ATT_SCALE).astype(BF16)
    kb_ref[...] = head_norm(zb[:, W_MIX:2 * W_MIX], knb_ref[...])
    vb_ref[...] = zb[:, 2 * W_MIX:]
    fl = _dot(n1, wf_ref[...]) + bf_ref[...]
    ls = jnp.minimum(fl, 0.0) - jnp.log1p(jnp.exp(-jnp.abs(fl)))
    lf_ref[...] = ls[:, :N_HEADS]
    zg = _dot(n1, wg_ref[...])
    sg = 1.0 / (1.0 + jnp.exp(-zg))
    ga_ref[...] = sg[:, :D_MODEL].astype(BF16)
    gb_ref[...] = sg[:, D_MODEL:].astype(BF16)


def _proj(x, w, tb):
    n = x.shape[0]
    row = lambda width: pl.BlockSpec((tb, width), lambda i: (i, 0))
    ins = [x, w["g_mix"], w["w_a"], w["w_f"], w["w_b"], w["w_g"], w["b_f"], w["qn_a"], w["kn_a"], w["qn_b"],
           w["kn_b"], w["msum"]]
    in_specs = [row(D_MODEL)] + [_const_spec(a.shape) for a in ins[1:]]
    widths = [(W_MIX, BF16), (W_MIX, F32), (W_MIX, F32), (N_HEADS, F32), (W_MIX, BF16), (W_MIX, F32), (W_MIX, F32),
              (D_MODEL, BF16), (D_MODEL, BF16)]
    return pl.pallas_call(
        _proj_body,
        grid=(n // tb,),
        in_specs=in_specs,
        out_specs=[row(wd) for wd, _ in widths],
        out_shape=[jax.ShapeDtypeStruct((n, wd), dt) for wd, dt in widths],
        compiler_params=_params(("parallel",)),
        name="proj",
    )(*ins)


def _cumsum_lanes(x):
    n = x.shape[-1]
    lane = lax.broadcasted_iota(jnp.int32, x.shape, x.ndim - 1)
    s = 1
    while s < n:
        x = x + jnp.where(lane >= s, pltpu.roll(x, s, axis=x.ndim - 1), 0.0)
        s *= 2
    return x


def _head_of_pair(x, hh):
    lane = lax.broadcasted_iota(jnp.int32, (1, PAIR), 1)
    keep = (lane < HEAD_DIM) if hh == 0 else (lane >= HEAD_DIM)
    return jnp.where(keep, x, jnp.zeros_like(x))


def _merge_pair(o0, o1):
    lane = lax.broadcasted_iota(jnp.int32, (1, PAIR), 1)
    return jnp.where(lane < HEAD_DIM, o0, o1)


def _fox_body(q_ref, k_ref, v_ref, lft_ref, o_ref, kb_s, vb_s, c_s, *, tq, nq):
    hp = pl.program_id(1)
    qi = pl.program_id(2)

    @pl.when(qi == 0)
    def _():
        kb_s[...] = k_ref[0].astype(BF16)
        vb_s[...] = v_ref[0].astype(BF16)
        c_s[...] = _cumsum_lanes(lft_ref[0])

    q = q_ref[0]
    q2 = jnp.concatenate([_head_of_pair(q, 0), _head_of_pair(q, 1)], axis=0)
    row = lax.broadcasted_iota(jnp.int32, (tq, tq), 0)
    col = lax.broadcasted_iota(jnp.int32, (tq, tq), 1)

    def tile(n_blocks):
        keys = n_blocks * tq
        parts = [[], []]
        for start in range(0, keys, FOX_KEY_CHUNK):
            size = min(FOX_KEY_CHUNK, keys - start)
            s2 = _dot_t(q2, kb_s[start:start + size, :])
            probs, stats = [], []
            for hh in range(2):
                s = s2[hh * tq:(hh + 1) * tq] - c_s[pl.ds(2 * hp + hh, 1), start:start + size]
                if start + size == keys:
                    diag = jnp.where(col <= row, s[:, size - tq:], -jnp.inf)
                    s = diag if size == tq else jnp.concatenate([s[:, :size - tq], diag], axis=1)
                m = jnp.max(s, axis=-1, keepdims=True)
                p = jnp.exp(s - m)
                stats.append((m, jnp.sum(p, axis=-1, keepdims=True)))
                probs.append(p.astype(BF16))
            o2 = _dot(jnp.concatenate(probs, axis=0), vb_s[start:start + size, :])
            for hh in range(2):
                parts[hh].append(stats[hh] + (o2[hh * tq:(hh + 1) * tq],))
        outs = []
        for hh in range(2):
            m = functools.reduce(jnp.maximum, [mc for mc, _, _ in parts[hh]])
            scales = [jnp.exp(mc - m) for mc, _, _ in parts[hh]]
            l = sum(lc * sc for (_, lc, _), sc in zip(parts[hh], scales))
            o = sum(oc * sc for (_, _, oc), sc in zip(parts[hh], scales))
            outs.append(o / l)
        o_ref[0] = _merge_pair(outs[0], outs[1]).astype(o_ref.dtype)

    for n_blocks in range(1, nq + 1):
        pl.when(qi == n_blocks - 1)(functools.partial(tile, n_blocks))


def _fox_prompt(q, k, v, lft, tq):
    b, t, _ = q.shape
    nq = t // tq
    return pl.pallas_call(
        functools.partial(_fox_body, tq=tq, nq=nq),
        grid=(b, N_PAIRS, nq),
        in_specs=[
            pl.BlockSpec((1, tq, PAIR), lambda i, p, j: (i, j, p)),
            pl.BlockSpec((1, t, PAIR), lambda i, p, j: (i, 0, p)),
            pl.BlockSpec((1, t, PAIR), lambda i, p, j: (i, 0, p)),
            pl.BlockSpec((1, N_HEADS, t), lambda i, p, j: (i, 0, 0)),
        ],
        out_specs=pl.BlockSpec((1, tq, PAIR), lambda i, p, j: (i, j, p)),
        out_shape=jax.ShapeDtypeStruct((b, t, W_MIX), BF16),
        scratch_shapes=[pltpu.VMEM((t, PAIR), BF16), pltpu.VMEM((t, PAIR), BF16), pltpu.VMEM((N_HEADS, t), F32)],
        compiler_params=_params(("parallel", "parallel", "arbitrary")),
        name="fox_prompt",
    )(q, k, v, lft)


def _band_body(q_ref, k_ref, v_ref, bias_ref, o_ref, kp_s, vp_s, *, tq):
    qi = pl.program_id(1)
    win = tq + BAND_PAST

    @pl.when(qi == 0)
    def _():
        zeros = jnp.zeros((BAND_PAST, W_MIX), BF16)
        kp_s[:BAND_PAST, :] = zeros
        vp_s[:BAND_PAST, :] = zeros
        kp_s[BAND_PAST:, :] = k_ref[0].astype(BF16)
        vp_s[BAND_PAST:, :] = v_ref[0].astype(BF16)

    off = pl.multiple_of(qi * tq, tq)
    exists = lax.broadcasted_iota(jnp.int32, (1, win), 1) >= BAND_PAST - qi * tq
    for pair in range(N_PAIRS):
        lanes = slice(pair * PAIR, (pair + 1) * PAIR)
        kw = kp_s[pl.ds(off, win), lanes]
        vw = vp_s[pl.ds(off, win), lanes]
        q = q_ref[0, :, lanes]
        q2 = jnp.concatenate([_head_of_pair(q, 0), _head_of_pair(q, 1)], axis=0)
        s2 = _dot_t(q2, kw)
        probs, sums = [], []
        for hh in range(2):
            s = s2[hh * tq:(hh + 1) * tq] + bias_ref[2 * pair + hh]
            s = jnp.where(exists, s, NEG)
            m = jnp.max(s, axis=-1, keepdims=True)
            p = jnp.exp(s - m)
            sums.append(jnp.sum(p, axis=-1, keepdims=True))
            probs.append(p.astype(BF16))
        o2 = _dot(jnp.concatenate(probs, axis=0), vw)
        o_ref[0, :, lanes] = _merge_pair(o2[:tq] / sums[0], o2[tq:] / sums[1]).astype(o_ref.dtype)


def _band_prompt(q, k, v, bias, tq):
    b, t, _ = q.shape
    win = tq + BAND_PAST
    return pl.pallas_call(
        functools.partial(_band_body, tq=tq),
        grid=(b, t // tq),
        in_specs=[
            pl.BlockSpec((1, tq, W_MIX), lambda i, j: (i, j, 0)),
            pl.BlockSpec((1, t, W_MIX), lambda i, j: (i, 0, 0)),
            pl.BlockSpec((1, t, W_MIX), lambda i, j: (i, 0, 0)),
            _const_spec((N_HEADS, tq, win)),
        ],
        out_specs=pl.BlockSpec((1, tq, W_MIX), lambda i, j: (i, j, 0)),
        out_shape=jax.ShapeDtypeStruct((b, t, W_MIX), BF16),
        scratch_shapes=[pltpu.VMEM((t + BAND_PAST, W_MIX), BF16), pltpu.VMEM((t + BAND_PAST, W_MIX), BF16)],
        compiler_params=_params(("parallel", "arbitrary")),
        name="band_prompt",
    )(q, k, v, bias)


def _toeplitz(w, n, m):
    heads, span = w.shape
    hankel = jnp.tile(w, (1, n + 1))[:, :n * (span + 1)].reshape(heads, n, span + 1)[:, :, :m]
    return hankel[:, ::-1, :]


def _band_bias_prompt(rel_bias, tq):
    win = tq + BAND_PAST
    rel = jnp.arange(tq + win - 1) - (tq - 1) - BAND_PAST
    table = _toeplitz(rel_bias[:, jnp.clip(rel, -MAX_REL, MAX_REL) + MAX_REL].astype(F32), tq, win)
    ii = jnp.arange(tq)[:, None]
    jj = jnp.arange(win)[None, :]
    lo = (ii // CHUNK) * CHUNK
    in_band = (jj >= lo) & (jj < lo + BAND_PAST + CHUNK)
    return jnp.where(in_band[None], table, NEG)


def _fox_sample_body(q_ref, kc_ref, vc_ref, kn_ref, vn_ref, lft_ref, o_ref, *, past, n_new):
    hp = pl.program_id(1)
    c = _cumsum_lanes(lft_ref[0])
    kc = kc_ref[0].astype(BF16)
    vc = vc_ref[0].astype(BF16)
    kn = kn_ref[0].astype(BF16)
    vn = vn_ref[0].astype(BF16)
    q = q_ref[0]
    nq = q.shape[0]
    row = lax.broadcasted_iota(jnp.int32, (nq, LANES), 0)
    col = lax.broadcasted_iota(jnp.int32, (nq, LANES), 1)
    outs = []
    for hh in range(2):
        qh = _head_of_pair(q, hh)
        sel = lax.broadcasted_iota(jnp.int32, (N_HEADS, 1), 0) == 2 * hp + hh
        crow = jnp.sum(jnp.where(sel, c, 0.0), axis=0, keepdims=True)
        sc = _dot_t(qh, kc) - crow[:, :past]
        sn = _dot_t(qh, kn) - crow[:, past:]
        sn = jnp.where((col <= row) & (col < n_new), sn, -jnp.inf)
        m = jnp.maximum(jnp.max(sc, axis=-1, keepdims=True), jnp.max(sn, axis=-1, keepdims=True))
        pc = jnp.exp(sc - m)
        pn = jnp.exp(sn - m)
        l = jnp.sum(pc, axis=-1, keepdims=True) + jnp.sum(pn, axis=-1, keepdims=True)
        outs.append((_dot(pc.astype(BF16), vc) + _dot(pn.astype(BF16), vn)) / l)
    o_ref[0] = _merge_pair(outs[0], outs[1]).astype(o_ref.dtype)


def _fox_sample(q, kc, vc, kn, vn, lft, n_new):
    b, nq, _ = q.shape
    past = kc.shape[1]
    pair_spec = lambda rows: pl.BlockSpec((1, rows, PAIR), lambda i, p: (i, 0, p))
    return pl.pallas_call(
        functools.partial(_fox_sample_body, past=past, n_new=n_new),
        grid=(b, N_PAIRS),
        in_specs=[pair_spec(nq), pair_spec(past), pair_spec(past), pair_spec(LANES), pair_spec(LANES),
                  pl.BlockSpec((1, N_HEADS, past + LANES), lambda i, p: (i, 0, 0))],
        out_specs=pair_spec(nq),
        out_shape=jax.ShapeDtypeStruct((b, nq, W_MIX), BF16),
        compiler_params=_params(("parallel", "parallel")),
        name="fox_sample",
    )(q, kc, vc, kn, vn, lft)


def _band_sample_body(q_ref, kc_ref, vc_ref, kn_ref, vn_ref, bias_ref, o_ref, *, past):
    kc = kc_ref[0].astype(BF16)
    vc = vc_ref[0].astype(BF16)
    kn = kn_ref[0].astype(BF16)
    vn = vn_ref[0].astype(BF16)
    q = q_ref[0]
    outs = []
    for hh in range(2):
        qh = _head_of_pair(q, hh)
        bias = bias_ref[hh]
        sc = _dot_t(qh, kc) + bias[:, :past]
        sn = _dot_t(qh, kn) + bias[:, past:]
        m = jnp.maximum(jnp.max(sc, axis=-1, keepdims=True), jnp.max(sn, axis=-1, keepdims=True))
        pc = jnp.exp(sc - m)
        pn = jnp.exp(sn - m)
        l = jnp.sum(pc, axis=-1, keepdims=True) + jnp.sum(pn, axis=-1, keepdims=True)
        outs.append((_dot(pc.astype(BF16), vc) + _dot(pn.astype(BF16), vn)) / l)
    o_ref[0] = _merge_pair(outs[0], outs[1]).astype(o_ref.dtype)


def _band_sample(q, kc, vc, kn, vn, bias):
    b, nq, _ = q.shape
    past = kc.shape[1]
    pair_spec = lambda rows: pl.BlockSpec((1, rows, PAIR), lambda i, p: (i, 0, p))
    return pl.pallas_call(
        functools.partial(_band_sample_body, past=past),
        grid=(b, N_PAIRS),
        in_specs=[pair_spec(nq), pair_spec(past), pair_spec(past), pair_spec(LANES), pair_spec(LANES),
                  pl.BlockSpec((2, nq, past + LANES), lambda i, p: (p, 0, 0))],
        out_specs=pair_spec(nq),
        out_shape=jax.ShapeDtypeStruct((b, nq, W_MIX), BF16),
        compiler_params=_params(("parallel", "parallel")),
        name="band_sample",
    )(q, kc, vc, kn, vn, bias)


def _band_bias_sample(rel_bias, n_new, past):
    ii = jnp.arange(n_new)[:, None]
    jj = jnp.arange(past + LANES)[None, :]
    table = rel_bias[:, jnp.clip(jj - past - ii, -MAX_REL, MAX_REL) + MAX_REL].astype(F32)
    return jnp.where((jj < past + n_new)[None], table, NEG)


def _merge_body(x_ref, ya_ref, yb_ref, ga_ref, gb_ref, wua_ref, wub_ref, wo_ref, gffn_ref, h_ref, n2_ref):
    merged = (ga_ref[...].astype(F32) * _dot(ya_ref[...], wua_ref[...])
              + gb_ref[...].astype(F32) * _dot(yb_ref[...], wub_ref[...]))
    h = x_ref[...] + _dot(merged.astype(BF16), wo_ref[...])
    h_ref[...] = h
    n2_ref[...] = _rms(h, gffn_ref[...]).astype(BF16)


def _merge(x, ya, yb, ga, gb, w, tb):
    n = x.shape[0]
    row = lambda width: pl.BlockSpec((tb, width), lambda i: (i, 0))
    consts = [w["w_up_a"], w["w_up_b"], w["w_out"], w["g_ffn"]]
    return pl.pallas_call(
        _merge_body,
        grid=(n // tb,),
        in_specs=[row(D_MODEL), row(W_MIX), row(W_MIX), row(D_MODEL), row(D_MODEL)]
        + [_const_spec(a.shape) for a in consts],
        out_specs=[row(D_MODEL), row(D_MODEL)],
        out_shape=[jax.ShapeDtypeStruct((n, D_MODEL), F32), jax.ShapeDtypeStruct((n, D_MODEL), BF16)],
        compiler_params=_params(("parallel",)),
        name="merge",
    )(x, ya, yb, ga, gb, *consts)


def _leave(work, exact):
    hit = work == jnp.max(work, axis=0, keepdims=True)
    if exact:
        idx = lax.broadcasted_iota(jnp.int32, work.shape, 0)
        hit = idx == jnp.min(jnp.where(hit, idx, work.shape[0]), axis=0, keepdims=True)
    return hit


def _top16(s, vals_ref, exact):
    work = s
    for r in range(TOPK):
        vals_ref[r:r + 1, :] = jnp.max(work, axis=0, keepdims=True)
        work = jnp.where(_leave(work, exact), -RANK_SENTINEL * (1.0 + r / 32.0), work)
    return jnp.where(work <= -RANK_SENTINEL, work * (-32.0 / RANK_SENTINEL) - 32.0, float(TOPK))


def _count(mask):
    return jnp.sum(mask.astype(F32), axis=0, keepdims=True)


def _pair_bf16(x):
    bits = lax.bitcast_convert_type(x.astype(BF16).astype(F32), jnp.uint32)
    return bits | (bits >> 16)


def _retrieve_tile(s_s, tt, r1_ref, b_ref, a_ref, c_ref, va_s, vb_s, exact):
    off = jnp.zeros((1, LANES), F32)
    for h in range(PEER_HEADS):
        s0 = s_s[2 * h, tt]
        s1 = s_s[2 * h + 1, tt]
        rank0 = _top16(s0, va_s, exact)
        rank1 = _top16(s1, vb_s, exact)
        va = va_s[...]
        vb = vb_s[...]
        cand = jnp.concatenate([va[0:1] + vb] + [va[k:k + 1] + vb[0:8] for k in range(1, 8)] + [va[8:16] + vb[0:1]],
                               axis=0)
        work = cand
        picked = jnp.zeros(cand.shape, jnp.bool_)
        for _ in range(TOPK):
            hit = _leave(work, exact)
            picked = picked | hit
            work = jnp.where(hit, -jnp.inf, work)
        top = va[0:1] + vb[0:1]
        z = jnp.sum(jnp.where(picked, jnp.exp(cand - top), 0.0), axis=0, keepdims=True)
        pickf = picked.astype(F32)
        counts = [jnp.sum(pickf[0:16], axis=0, keepdims=True)]
        counts += [jnp.sum(pickf[8 + 8 * k:16 + 8 * k], axis=0, keepdims=True) for k in range(1, 8)]
        counts += [pickf[72 + k:73 + k] for k in range(8)]
        c = jnp.zeros(s0.shape, F32)
        for k in range(TOPK):
            c = jnp.where(rank0 == float(k), counts[k], c)
        a = jnp.where(rank0 < float(TOPK), jnp.exp(s0 - va[0:1]) * (0.5 / z), 0.0)
        b = jnp.where(rank1 < float(TOPK), jnp.exp(s1 - vb[0:1]), 0.0)
        r1_ref[tt, h] = rank1.astype(BF16)
        b_ref[tt, h] = b.astype(BF16)
        a_rows = _pair_bf16(a)
        c_rows = _pair_bf16(c)
        for grp in range(N_KEYS // SUBLANES):
            rows = slice(grp * SUBLANES, (grp + 1) * SUBLANES)
            a_ref[tt, h, grp] = a_rows[rows]
            c_ref[tt, h, grp] = c_rows[rows]
        if not exact:
            for n_left in (_count(rank0 < float(TOPK)), _count(rank1 < float(TOPK)), _count(picked)):
                off = off + jnp.abs(n_left - float(TOPK))
    return off


def _retrieve_body(n2_ref, wq_ref, sk_ref, r1_ref, b_ref, a_ref, c_ref, s_s, va_s, vb_s, *, tb):
    q = _dot(n2_ref[...], wq_ref[...]).astype(BF16)
    for h in range(PEER_HEADS):
        qh = q[:, h * PAIR:(h + 1) * PAIR]
        for half in range(2):
            s = _dot_t(sk_ref[h, half], qh)
            for tt in range(tb // LANES):
                s_s[2 * h + half, tt] = s[:, tt * LANES:(tt + 1) * LANES]

    def tile(tt, carry):
        maps = (r1_ref, b_ref, a_ref, c_ref, va_s, vb_s)
        off = _retrieve_tile(s_s, tt, *maps, exact=False)

        @pl.when(jnp.max(off) > 0.0)
        def _():
            _retrieve_tile(s_s, tt, *maps, exact=True)

        return carry

    lax.fori_loop(0, tb // LANES, tile, 0)


def _retrieve(n2, w, tb):
    n = n2.shape[0]
    nt = tb // LANES
    maps = pl.BlockSpec((nt, PEER_HEADS, N_KEYS, LANES), lambda i: (i, 0, 0, 0))
    rows = pl.BlockSpec((nt, PEER_HEADS, N_KEYS // SUBLANES, SUBLANES, LANES), lambda i: (i, 0, 0, 0, 0))
    map_shape = (n // LANES, PEER_HEADS, N_KEYS, LANES)
    row_shape = (n // LANES, PEER_HEADS, N_KEYS // SUBLANES, SUBLANES, LANES)
    return pl.pallas_call(
        functools.partial(_retrieve_body, tb=tb),
        grid=(n // tb,),
        in_specs=[pl.BlockSpec((tb, D_MODEL), lambda i: (i, 0)), _const_spec(w["peer_wq"].shape),
                  _const_spec(w["peer_sk"].shape)],
        out_specs=[maps, maps, rows, rows],
        out_shape=[jax.ShapeDtypeStruct(map_shape, BF16), jax.ShapeDtypeStruct(map_shape, BF16),
                   jax.ShapeDtypeStruct(row_shape, jnp.uint32), jax.ShapeDtypeStruct(row_shape, jnp.uint32)],
        scratch_shapes=[pltpu.VMEM((2 * PEER_HEADS, nt, N_KEYS, LANES), F32), pltpu.VMEM((TOPK, LANES), F32),
                        pltpu.VMEM((TOPK, LANES), F32)],
        compiler_params=_params(("parallel",)),
        name="peer_retrieve",
    )(n2, w["peer_wq"], w["peer_sk"])


def _twice_gelu(x):
    k0 = math.sqrt(2.0 / math.pi)
    return x * (1.0 + jnp.tanh(x * (k0 + (k0 * 0.044715) * (x * x))))


def _row_tile(ref, tt, h, g, ii):
    row = jnp.broadcast_to(ref[tt, h, g, ii:ii + 1, :], (SUBLANES, LANES))
    packed = pltpu.bitcast(row, BF16)
    return jnp.concatenate([packed] * (N_KEYS // packed.shape[0]), axis=0)


def _dense_body(n2_ref, u_ref, vt_ref, r1_ref, b_ref, a_ref, c_ref, o_ref, acc_s, gate_s, r1_s, b_s, *, tb):
    g = pl.program_id(1)

    @pl.when(g == 0)
    def _():
        acc_s[...] = jnp.zeros(acc_s.shape, F32)
        r1_s[...] = r1_ref[...]
        b_s[...] = b_ref[...]

    rows_per_load = 2
    for tt in range(tb // LANES):
        lanes = slice(tt * LANES, (tt + 1) * LANES)
        for ii0 in range(0, SUBLANES, rows_per_load):
            gates = [None] * rows_per_load
            for h in range(PEER_HEADS):
                rank_tile = r1_s[tt, h]
                weight_tile = b_s[tt, h]
                for k in range(rows_per_load):
                    wgt = weight_tile * _row_tile(a_ref, tt, h, g, ii0 + k)
                    term = jnp.where(rank_tile < _row_tile(c_ref, tt, h, g, ii0 + k), wgt, jnp.zeros_like(wgt))
                    gates[k] = term if gates[k] is None else gates[k] + term
            for k in range(rows_per_load):
                gate_s[(ii0 + k) * N_KEYS:(ii0 + k + 1) * N_KEYS, lanes] = gates[k]

    hid = _dot_t(u_ref[...], n2_ref[...])
    weighted = gate_s[...] * _twice_gelu(hid.astype(BF16))
    acc_s[...] += _dot(vt_ref[0], weighted)

    @pl.when(g == pl.num_programs(1) - 1)
    def _():
        o_ref[...] = acc_s[...].T


def _dense(n2, u, vt, r1, b, a, c, tb):
    n = n2.shape[0]
    ec = EXPERT_CHUNK
    ng = N_EXPERTS // ec
    nt = tb // LANES
    once = pl.Buffered(1)
    maps = pl.BlockSpec((nt, PEER_HEADS, N_KEYS, LANES), lambda i, g: (i, 0, 0, 0), pipeline_mode=once)
    rows = pl.BlockSpec((nt, PEER_HEADS, ng, SUBLANES, LANES), lambda i, g: (i, 0, 0, 0, 0), pipeline_mode=once)
    return pl.pallas_call(
        functools.partial(_dense_body, tb=tb),
        grid=(n // tb, ng),
        in_specs=[pl.BlockSpec((tb, D_MODEL), lambda i, g: (i, 0), pipeline_mode=once),
                  pl.BlockSpec((ec, D_MODEL), lambda i, g: (g, 0)),
                  pl.BlockSpec((1, D_MODEL, ec), lambda i, g: (g, 0, 0)),
                  maps, maps, rows, rows],
        out_specs=pl.BlockSpec((tb, D_MODEL), lambda i, g: (i, 0)),
        out_shape=jax.ShapeDtypeStruct((n, D_MODEL), F32),
        scratch_shapes=[pltpu.VMEM((D_MODEL, tb), F32), pltpu.VMEM((ec, tb), BF16),
                        pltpu.VMEM((nt, PEER_HEADS, N_KEYS, LANES), BF16),
                        pltpu.VMEM((nt, PEER_HEADS, N_KEYS, LANES), BF16)],
        compiler_params=_params(("parallel", "arbitrary")),
        name="peer_dense",
    )(n2, u, vt, r1, b, a, c)


def _final_body(h_ref, peer_ref, p_ref, gple_ref, wg_ref, wp_ref, o_ref):
    h = h_ref[...] + peer_ref[...]
    zg = _dot(_rms(h, gple_ref[...]).astype(BF16), wg_ref[...])
    gate = 1.0 / (1.0 + jnp.exp(-zg))
    o_ref[...] = h + gate * _dot(p_ref[...].astype(BF16), wp_ref[...])


def _final(h, peer, p, w, tb):
    n = h.shape[0]
    row = lambda width: pl.BlockSpec((tb, width), lambda i: (i, 0))
    consts = [w["g_ple"], w["w_ple_gate"], w["w_ple_proj"]]
    return pl.pallas_call(
        _final_body,
        grid=(n // tb,),
        in_specs=[row(D_MODEL), row(D_MODEL), row(PLE_DIM)] + [_const_spec(a.shape) for a in consts],
        out_specs=row(D_MODEL),
        out_shape=jax.ShapeDtypeStruct((n, D_MODEL), F32),
        compiler_params=_params(("parallel",)),
        name="final",
    )(h, peer, p, *consts)


def _layer_weights(l, g_mix, w_in, b_f, qn_a, kn_a, qn_b, kn_b, w_up_a, w_up_b, w_out, g_ffn, peer_wq, peer_subkeys,
                   peer_u, peer_v, g_ple, w_ple_gate, w_ple_proj):
    o_f = 3 * W_MIX
    o_b = o_f + N_HEADS
    o_g = o_b + 3 * W_MIX
    wi = w_in[l]
    tile_heads = lambda g: jnp.tile(g[l].astype(F32), N_HEADS)[None, :]
    head_of = jnp.arange(W_MIX) // HEAD_DIM
    sk = peer_subkeys[l].astype(BF16)
    zeros = jnp.zeros_like(sk[:, 0])
    sk_pad = jnp.stack([jnp.concatenate([sk[:, 0], zeros], axis=-1), jnp.concatenate([zeros, sk[:, 1]], axis=-1)],
                       axis=1)
    return {
        "g_mix": g_mix[l][None, :],
        "w_a": wi[:, :o_f].astype(BF16),
        "w_f": jnp.pad(wi[:, o_f:o_b], ((0, 0), (0, LANES - N_HEADS))).astype(BF16),
        "w_b": wi[:, o_b:o_g].astype(BF16),
        "w_g": wi[:, o_g:].astype(BF16),
        "b_f": jnp.pad(b_f[l], (0, LANES - N_HEADS))[None, :],
        "qn_a": tile_heads(qn_a), "kn_a": tile_heads(kn_a), "qn_b": tile_heads(qn_b), "kn_b": tile_heads(kn_b),
        "msum": jnp.where(head_of[:, None] == head_of[None, :], 1.0 / HEAD_DIM, 0.0).astype(BF16),
        "w_up_a": w_up_a[l].astype(BF16), "w_up_b": w_up_b[l].astype(BF16), "w_out": w_out[l].astype(BF16),
        "g_ffn": g_ffn[l][None, :],
        "peer_wq": peer_wq[l].astype(BF16),
        "peer_sk": sk_pad,
        "peer_u": peer_u[l].astype(BF16),
        "peer_vt": peer_v[l].astype(BF16).reshape(-1, EXPERT_CHUNK, D_MODEL).transpose(0, 2, 1),
        "g_ple": g_ple[l][None, :],
        "w_ple_gate": w_ple_gate[l].astype(BF16),
        "w_ple_proj": w_ple_proj[l].astype(BF16),
    }


def _channel(x, ya, yb, ga, gb, p, w, tb, tb_dense):
    h1, n2 = _merge(x, ya, yb, ga, gb, w, tb)
    r1, b, a, c = _retrieve(n2, w, tb_dense)
    peer = _dense(n2, w["peer_u"], w["peer_vt"], r1, b, a, c, tb_dense)
    return _final(h1, peer, p, w, tb)


def _pad_rows(x, rows):
    return jnp.pad(x, ((0, 0), (0, rows - x.shape[1]), (0, 0)))


def kernel(x_prompt, x_sample, cache_a_k, cache_a_v, cache_a_logf, cache_b_k, cache_b_v, p_prompt, p_sample, g_mix, w_in, b_f, qn_a, kn_a, qn_b, kn_b, rel_bias_b, w_up_a, w_up_b, w_out, g_ffn, peer_wq, peer_subkeys, peer_u, peer_v, g_ple, w_ple_gate, w_ple_proj):
    depth = w_in.shape[0]
    bp, tp, _ = x_prompt.shape
    bs, ts, _ = x_sample.shape
    past = cache_a_k.shape[2]
    band_rows = min(BAND_PAST, tp)
    tq = 256
    hp = x_prompt.reshape(bp * tp, D_MODEL)
    hs = x_sample.reshape(bs * ts, D_MODEL)
    outs = [[] for _ in range(10)]
    for l in range(depth):
        w = _layer_weights(l, g_mix, w_in, b_f, qn_a, kn_a, qn_b, kn_b, w_up_a, w_up_b, w_out, g_ffn, peer_wq,
                           peer_subkeys, peer_u, peer_v, g_ple, w_ple_gate, w_ple_proj)
        qa, ka, va, lf, qb, kb, vb, ga, gb = _proj(hp, w, 256)
        as_seq = lambda z: z.reshape(bp, tp, z.shape[-1])
        ya = _fox_prompt(as_seq(qa), as_seq(ka), as_seq(va), as_seq(lf).transpose(0, 2, 1), tq)
        yb = _band_prompt(as_seq(qb), as_seq(kb), as_seq(vb), _band_bias_prompt(rel_bias_b[l], tq), tq)
        hp = _channel(hp, ya.reshape(-1, W_MIX), yb.reshape(-1, W_MIX), ga, gb, p_prompt[l].reshape(-1, PLE_DIM), w,
                      512, 1024)
        heads = lambda z, b_, t_: z.reshape(b_, t_, N_HEADS, HEAD_DIM)
        outs[0].append(heads(ka, bp, tp)); outs[1].append(heads(va, bp, tp)); outs[2].append(as_seq(lf))
        newest = lambda z: heads(as_seq(z)[:, -band_rows:], bp, band_rows)
        outs[3].append(newest(kb)); outs[4].append(newest(vb))
        qa, ka, va, lf, qb, kb, vb, ga, gb = _proj(hs, w, bs * ts)
        as_seq = lambda z: z.reshape(bs, ts, z.shape[-1])
        lft = jnp.concatenate([cache_a_logf[l].astype(F32), as_seq(lf),
                               jnp.zeros((bs, LANES - ts, N_HEADS), F32)], axis=1).transpose(0, 2, 1)
        flat_cache = lambda z: z[l].reshape(bs, z.shape[2], W_MIX)
        ya = _fox_sample(as_seq(qa), flat_cache(cache_a_k), flat_cache(cache_a_v), _pad_rows(as_seq(ka), LANES),
                         _pad_rows(as_seq(va), LANES), lft, ts)
        yb = _band_sample(as_seq(qb), flat_cache(cache_b_k), flat_cache(cache_b_v), _pad_rows(as_seq(kb), LANES),
                          _pad_rows(as_seq(vb), LANES), _band_bias_sample(rel_bias_b[l], ts, cache_b_k.shape[2]))
        hs = _channel(hs, ya.reshape(-1, W_MIX), yb.reshape(-1, W_MIX), ga, gb, p_sample[l].reshape(-1, PLE_DIM), w,
                      bs * ts, bs * ts)
        outs[5].append(heads(ka, bs, ts)); outs[6].append(heads(va, bs, ts)); outs[7].append(as_seq(lf))
        outs[8].append(heads(kb, bs, ts)); outs[9].append(heads(vb, bs, ts))
    return (hp.reshape(bp, tp, D_MODEL), hs.reshape(bs, ts, D_MODEL)) + tuple(jnp.stack(o) for o in outs)
```

```python
import functools
import math

import jax
import jax.numpy as jnp
from jax import lax
from jax.experimental import pallas as pl
from jax.experimental.pallas import tpu as pltpu

F32 = jnp.float32
BF16 = jnp.bfloat16

D_MODEL = 1024
HEAD_DIM = 64
N_HEADS = 8
W_MIX = N_HEADS * HEAD_DIM
PAIR = 2 * HEAD_DIM
N_PAIRS = N_HEADS // 2
CHUNK = 64
BAND_PAST = 8 * CHUNK
MAX_REL = 128
PLE_DIM = 256
PEER_HEADS = 8
N_KEYS = 128
N_EXPERTS = N_KEYS * N_KEYS
TOPK = 16
RMS_EPS = 1e-6
ATT_SCALE = HEAD_DIM ** -0.5
NEG = -1e30
RANK_SENTINEL = 2.0 ** 100
FOX_KEY_CHUNK = 512
LANES = 128
SUBLANES = 8
EXPERT_CHUNK = SUBLANES * N_KEYS
VMEM_LIMIT = 56 * 1024 * 1024

N_CAND = 16 + 7 * 8 + 8


def _params(sem, vmem=VMEM_LIMIT):
    return pltpu.CompilerParams(dimension_semantics=sem, vmem_limit_bytes=vmem)


def _rms(x, g):
    return x * lax.rsqrt(jnp.mean(x * x, axis=-1, keepdims=True) + RMS_EPS) * g


def _dot(a, b):
    return jnp.dot(a, b, preferred_element_type=F32)


def _dot_t(a, b):
    return lax.dot_general(a, b, (((1,), (1,)), ((), ())), preferred_element_type=F32)


def _const_spec(shape):
    nd = len(shape)
    return pl.BlockSpec(shape, lambda *_: (0,) * nd)


def _proj_body(x_ref, g_ref, wa_ref, wf_ref, wb_ref, wg_ref, bf_ref, qna_ref, kna_ref, qnb_ref, knb_ref, msum_ref,
               qa_ref, ka_ref, va_ref, lf_ref, qb_ref, kb_ref, vb_ref, ga_ref, gb_ref):
    n1 = _rms(x_ref[...], g_ref[...]).astype(BF16)

    def head_norm(z, gain):
        ms = _dot((z * z).astype(BF16), msum_ref[...])
        return z * lax.rsqrt(ms + RMS_EPS) * gain

    za = _dot(n1, wa_ref[...])
    qa_ref[...] = (head_norm(za[:, :W_MIX], qna_ref[...]) * ATT_SCALE).astype(BF16)
    ka_ref[...] = head_norm(za[:, W_MIX:2 * W_MIX], kna_ref[...])
    va_ref[...] = za[:, 2 * W_MIX:]
    zb = _dot(n1, wb_ref[...])
    qb_ref[...] = (head_norm(zb[:, :W_MIX], qnb_ref[...]) * ATT_SCALE).astype(BF16)
    kb_ref[...] = head_norm(zb[:, W_MIX:2 * W_MIX], knb_ref[...])
    vb_ref[...] = zb[:, 2 * W_MIX:]
    fl = _dot(n1, wf_ref[...]) + bf_ref[...]
    ls = jnp.minimum(fl, 0.0) - jnp.log1p(jnp.exp(-jnp.abs(fl)))
    lf_ref[...] = ls[:, :N_HEADS]
    zg = _dot(n1, wg_ref[...])
    sg = 1.0 / (1.0 + jnp.exp(-zg))
    ga_ref[...] = sg[:, :D_MODEL].astype(BF16)
    gb_ref[...] = sg[:, D_MODEL:].astype(BF16)


def _proj(x, w, tb):
    n = x.shape[0]
    row = lambda width: pl.BlockSpec((tb, width), lambda i: (i, 0))
    ins = [x, w["g_mix"], w["w_a"], w["w_f"], w["w_b"], w["w_g"], w["b_f"], w["qn_a"], w["kn_a"], w["qn_b"],
           w["kn_b"], w["msum"]]
    in_specs = [row(D_MODEL)] + [_const_spec(a.shape) for a in ins[1:]]
    widths = [(W_MIX, BF16), (W_MIX, F32), (W_MIX, F32), (N_HEADS, F32), (W_MIX, BF16), (W_MIX, F32), (W_MIX, F32),
              (D_MODEL, BF16), (D_MODEL, BF16)]
    return pl.pallas_call(
        _proj_body,
        grid=(n // tb,),
        in_specs=in_specs,
        out_specs=[row(wd) for wd, _ in widths],
        out_shape=[jax.ShapeDtypeStruct((n, wd), dt) for wd, dt in widths],
        compiler_params=_params(("parallel",)),
        name="proj",
    )(*ins)


def _cumsum_lanes(x):
    n = x.shape[-1]
    lane = lax.broadcasted_iota(jnp.int32, x.shape, x.ndim - 1)
    s = 1
    while s < n:
        x = x + jnp.where(lane >= s, pltpu.roll(x, s, axis=x.ndim - 1), 0.0)
        s *= 2
    return x


def _head_of_pair(x, hh):
    lane = lax.broadcasted_iota(jnp.int32, (1, PAIR), 1)
    keep = (lane < HEAD_DIM) if hh == 0 else (lane >= HEAD_DIM)
    return jnp.where(keep, x, jnp.zeros_like(x))


def _merge_pair(o0, o1):
    lane = lax.broadcasted_iota(jnp.int32, (1, PAIR), 1)
    return jnp.where(lane < HEAD_DIM, o0, o1)


def _fox_body(q_ref, k_ref, v_ref, lft_ref, o_ref, kb_s, vb_s, c_s, *, tq, nq):
    hp = pl.program_id(1)
    qi = pl.program_id(2)

    @pl.when(qi == 0)
    def _():
        kb_s[...] = k_ref[0].astype(BF16)
        vb_s[...] = v_ref[0].astype(BF16)
        c_s[...] = _cumsum_lanes(lft_ref[0])

    q = q_ref[0]
    q2 = jnp.concatenate([_head_of_pair(q, 0), _head_of_pair(q, 1)], axis=0)
    row = lax.broadcasted_iota(jnp.int32, (tq, tq), 0)
    col = lax.broadcasted_iota(jnp.int32, (tq, tq), 1)

    def tile(n_blocks):
        keys = n_blocks * tq
        parts = [[], []]
        for start in range(0, keys, FOX_KEY_CHUNK):
            size = min(FOX_KEY_CHUNK, keys - start)
            s2 = _dot_t(q2, kb_s[start:start + size, :])
            probs, stats = [], []
            for hh in range(2):
                s = s2[hh * tq:(hh + 1) * tq] - c_s[pl.ds(2 * hp + hh, 1), start:start + size]
                if start + size == keys:
                    diag = jnp.where(col <= row, s[:, size - tq:], -jnp.inf)
                    s = diag if size == tq else jnp.concatenate([s[:, :size - tq], diag], axis=1)
                m = jnp.max(s, axis=-1, keepdims=True)
                p = jnp.exp(s - m)
                stats.append((m, jnp.sum(p, axis=-1, keepdims=True)))
                probs.append(p.astype(BF16))
            o2 = _dot(jnp.concatenate(probs, axis=0), vb_s[start:start + size, :])
            for hh in range(2):
                parts[hh].append(stats[hh] + (o2[hh * tq:(hh + 1) * tq],))
        outs = []
        for hh in range(2):
            m = functools.reduce(jnp.maximum, [mc for mc, _, _ in parts[hh]])
            scales = [jnp.exp(mc - m) for mc, _, _ in parts[hh]]
            l = sum(lc * sc for (_, lc, _), sc in zip(parts[hh], scales))
            o = sum(oc * sc for (_, _, oc), sc in zip(parts[hh], scales))
            outs.append(o / l)
        o_ref[0] = _merge_pair(outs[0], outs[1]).astype(o_ref.dtype)

    for n_blocks in range(1, nq + 1):
        pl.when(qi == n_blocks - 1)(functools.partial(tile, n_blocks))


def _fox_prompt(q, k, v, lft, tq):
    b, t, _ = q.shape
    nq = t // tq
    return pl.pallas_call(
        functools.partial(_fox_body, tq=tq, nq=nq),
        grid=(b, N_PAIRS, nq),
        in_specs=[
            pl.BlockSpec((1, tq, PAIR), lambda i, p, j: (i, j, p)),
            pl.BlockSpec((1, t, PAIR), lambda i, p, j: (i, 0, p)),
            pl.BlockSpec((1, t, PAIR), lambda i, p, j: (i, 0, p)),
            pl.BlockSpec((1, N_HEADS, t), lambda i, p, j: (i, 0, 0)),
        ],
        out_specs=pl.BlockSpec((1, tq, PAIR), lambda i, p, j: (i, j, p)),
        out_shape=jax.ShapeDtypeStruct((b, t, W_MIX), BF16),
        scratch_shapes=[pltpu.VMEM((t, PAIR), BF16), pltpu.VMEM((t, PAIR), BF16), pltpu.VMEM((N_HEADS, t), F32)],
        compiler_params=_params(("parallel", "parallel", "arbitrary")),
        name="fox_prompt",
    )(q, k, v, lft)


def _band_body(q_ref, k_ref, v_ref, bias_ref, o_ref, kp_s, vp_s, *, tq):
    qi = pl.program_id(1)
    win = tq + BAND_PAST

    @pl.when(qi == 0)
    def _():
        zeros = jnp.zeros((BAND_PAST, W_MIX), BF16)
        kp_s[:BAND_PAST, :] = zeros
        vp_s[:BAND_PAST, :] = zeros
        kp_s[BAND_PAST:, :] = k_ref[0].astype(BF16)
        vp_s[BAND_PAST:, :] = v_ref[0].astype(BF16)

    off = pl.multiple_of(qi * tq, tq)
    exists = lax.broadcasted_iota(jnp.int32, (1, win), 1) >= BAND_PAST - qi * tq
    for pair in range(N_PAIRS):
        lanes = slice(pair * PAIR, (pair + 1) * PAIR)
        kw = kp_s[pl.ds(off, win), lanes]
        vw = vp_s[pl.ds(off, win), lanes]
        q = q_ref[0, :, lanes]
        q2 = jnp.concatenate([_head_of_pair(q, 0), _head_of_pair(q, 1)], axis=0)
        s2 = _dot_t(q2, kw)
        probs, sums = [], []
        for hh in range(2):
            s = s2[hh * tq:(hh + 1) * tq] + bias_ref[2 * pair + hh]
            s = jnp.where(exists, s, NEG)
            m = jnp.max(s, axis=-1, keepdims=True)
            p = jnp.exp(s - m)
            sums.append(jnp.sum(p, axis=-1, keepdims=True))
            probs.append(p.astype(BF16))
        o2 = _dot(jnp.concatenate(probs, axis=0), vw)
        o_ref[0, :, lanes] = _merge_pair(o2[:tq] / sums[0], o2[tq:] / sums[1]).astype(o_ref.dtype)


def _band_prompt(q, k, v, bias, tq):
    b, t, _ = q.shape
    win = tq + BAND_PAST
    return pl.pallas_call(
        functools.partial(_band_body, tq=tq),
        grid=(b, t // tq),
        in_specs=[
            pl.BlockSpec((1, tq, W_MIX), lambda i, j: (i, j, 0)),
            pl.BlockSpec((1, t, W_MIX), lambda i, j: (i, 0, 0)),
            pl.BlockSpec((1, t, W_MIX), lambda i, j: (i, 0, 0)),
            _const_spec((N_HEADS, tq, win)),
        ],
        out_specs=pl.BlockSpec((1, tq, W_MIX), lambda i, j: (i, j, 0)),
        out_shape=jax.ShapeDtypeStruct((b, t, W_MIX), BF16),
        scratch_shapes=[pltpu.VMEM((t + BAND_PAST, W_MIX), BF16), pltpu.VMEM((t + BAND_PAST, W_MIX), BF16)],
        compiler_params=_params(("parallel", "arbitrary")),
        name="band_prompt",
    )(q, k, v, bias)


def _toeplitz(w, n, m):
    heads, span = w.shape
    hankel = jnp.tile(w, (1, n + 1))[:, :n * (span + 1)].reshape(heads, n, span + 1)[:, :, :m]
    return hankel[:, ::-1, :]


def _band_bias_prompt(rel_bias, tq):
    win = tq + BAND_PAST
    rel = jnp.arange(tq + win - 1) - (tq - 1) - BAND_PAST
    table = _toeplitz(rel_bias[:, jnp.clip(rel, -MAX_REL, MAX_REL) + MAX_REL].astype(F32), tq, win)
    ii = jnp.arange(tq)[:, None]
    jj = jnp.arange(win)[None, :]
    lo = (ii // CHUNK) * CHUNK
    in_band = (jj >= lo) & (jj < lo + BAND_PAST + CHUNK)
    return jnp.where(in_band[None], table, NEG)


def _fox_sample_body(q_ref, kc_ref, vc_ref, kn_ref, vn_ref, lft_ref, o_ref, *, past, n_new):
    hp = pl.program_id(1)
    c = _cumsum_lanes(lft_ref[0])
    kc = kc_ref[0].astype(BF16)
    vc = vc_ref[0].astype(BF16)
    kn = kn_ref[0].astype(BF16)
    vn = vn_ref[0].astype(BF16)
    q = q_ref[0]
    nq = q.shape[0]
    row = lax.broadcasted_iota(jnp.int32, (nq, LANES), 0)
    col = lax.broadcasted_iota(jnp.int32, (nq, LANES), 1)
    outs = []
    for hh in range(2):
        qh = _head_of_pair(q, hh)
        sel = lax.broadcasted_iota(jnp.int32, (N_HEADS, 1), 0) == 2 * hp + hh
        crow = jnp.sum(jnp.where(sel, c, 0.0), axis=0, keepdims=True)
        sc = _dot_t(qh, kc) - crow[:, :past]
        sn = _dot_t(qh, kn) - crow[:, past:]
        sn = jnp.where((col <= row) & (col < n_new), sn, -jnp.inf)
        m = jnp.maximum(jnp.max(sc, axis=-1, keepdims=True), jnp.max(sn, axis=-1, keepdims=True))
        pc = jnp.exp(sc - m)
        pn = jnp.exp(sn - m)
        l = jnp.sum(pc, axis=-1, keepdims=True) + jnp.sum(pn, axis=-1, keepdims=True)
        outs.append((_dot(pc.astype(BF16), vc) + _dot(pn.astype(BF16), vn)) / l)
    o_ref[0] = _merge_pair(outs[0], outs[1]).astype(o_ref.dtype)


def _fox_sample(q, kc, vc, kn, vn, lft, n_new):
    b, nq, _ = q.shape
    past = kc.shape[1]
    pair_spec = lambda rows: pl.BlockSpec((1, rows, PAIR), lambda i, p: (i, 0, p))
    return pl.pallas_call(
        functools.partial(_fox_sample_body, past=past, n_new=n_new),
        grid=(b, N_PAIRS),
        in_specs=[pair_spec(nq), pair_spec(past), pair_spec(past), pair_spec(LANES), pair_spec(LANES),
                  pl.BlockSpec((1, N_HEADS, past + LANES), lambda i, p: (i, 0, 0))],
        out_specs=pair_spec(nq),
        out_shape=jax.ShapeDtypeStruct((b, nq, W_MIX), BF16),
        compiler_params=_params(("parallel", "parallel")),
        name="fox_sample",
    )(q, kc, vc, kn, vn, lft)


def _band_sample_body(q_ref, kc_ref, vc_ref, kn_ref, vn_ref, bias_ref, o_ref, *, past):
    kc = kc_ref[0].astype(BF16)
    vc = vc_ref[0].astype(BF16)
    kn = kn_ref[0].astype(BF16)
    vn = vn_ref[0].astype(BF16)
    q = q_ref[0]
    outs = []
    for hh in range(2):
        qh = _head_of_pair(q, hh)
        bias = bias_ref[hh]
        sc = _dot_t(qh, kc) + bias[:, :past]
        sn = _dot_t(qh, kn) + bias[:, past:]
        m = jnp.maximum(jnp.max(sc, axis=-1, keepdims=True), jnp.max(sn, axis=-1, keepdims=True))
        pc = jnp.exp(sc - m)
        pn = jnp.exp(sn - m)
        l = jnp.sum(pc, axis=-1, keepdims=True) + jnp.sum(pn, axis=-1, keepdims=True)
        outs.append((_dot(pc.astype(BF16), vc) + _dot(pn.astype(BF16), vn)) / l)
    o_ref[0] = _merge_pair(outs[0], outs[1]).astype(o_ref.dtype)


def _band_sample(q, kc, vc, kn, vn, bias):
    b, nq, _ = q.shape
    past = kc.shape[1]
    pair_spec = lambda rows: pl.BlockSpec((1, rows, PAIR), lambda i, p: (i, 0, p))
    return pl.pallas_call(
        functools.partial(_band_sample_body, past=past),
        grid=(b, N_PAIRS),
        in_specs=[pair_spec(nq), pair_spec(past), pair_spec(past), pair_spec(LANES), pair_spec(LANES),
                  pl.BlockSpec((2, nq, past + LANES), lambda i, p: (p, 0, 0))],
        out_specs=pair_spec(nq),
        out_shape=jax.ShapeDtypeStruct((b, nq, W_MIX), BF16),
        compiler_params=_params(("parallel", "parallel")),
        name="band_sample",
    )(q, kc, vc, kn, vn, bias)


def _band_bias_sample(rel_bias, n_new, past):
    ii = jnp.arange(n_new)[:, None]
    jj = jnp.arange(past + LANES)[None, :]
    table = rel_bias[:, jnp.clip(jj - past - ii, -MAX_REL, MAX_REL) + MAX_REL].astype(F32)
    return jnp.where((jj < past + n_new)[None], table, NEG)


def _merge_body(x_ref, ya_ref, yb_ref, ga_ref, gb_ref, wua_ref, wub_ref, wo_ref, gffn_ref, h_ref, n2_ref):
    merged = (ga_ref[...].astype(F32) * _dot(ya_ref[...], wua_ref[...])
              + gb_ref[...].astype(F32) * _dot(yb_ref[...], wub_ref[...]))
    h = x_ref[...] + _dot(merged.astype(BF16), wo_ref[...])
    h_ref[...] = h
    n2_ref[...] = _rms(h, gffn_ref[...]).astype(BF16)


def _merge(x, ya, yb, ga, gb, w, tb):
    n = x.shape[0]
    row = lambda width: pl.BlockSpec((tb, width), lambda i: (i, 0))
    consts = [w["w_up_a"], w["w_up_b"], w["w_out"], w["g_ffn"]]
    return pl.pallas_call(
        _merge_body,
        grid=(n // tb,),
        in_specs=[row(D_MODEL), row(W_MIX), row(W_MIX), row(D_MODEL), row(D_MODEL)]
        + [_const_spec(a.shape) for a in consts],
        out_specs=[row(D_MODEL), row(D_MODEL)],
        out_shape=[jax.ShapeDtypeStruct((n, D_MODEL), F32), jax.ShapeDtypeStruct((n, D_MODEL), BF16)],
        compiler_params=_params(("parallel",)),
        name="merge",
    )(x, ya, yb, ga, gb, *consts)


def _leave(work, exact):
    hit = work == jnp.max(work, axis=0, keepdims=True)
    if exact:
        idx = lax.broadcasted_iota(jnp.int32, work.shape, 0)
        hit = idx == jnp.min(jnp.where(hit, idx, work.shape[0]), axis=0, keepdims=True)
    return hit


def _top16(s, vals_ref, exact):
    work = s
    for r in range(TOPK):
        vals_ref[r:r + 1, :] = jnp.max(work, axis=0, keepdims=True)
        work = jnp.where(_leave(work, exact), -RANK_SENTINEL * (1.0 + r / 32.0), work)
    return jnp.where(work <= -RANK_SENTINEL, work * (-32.0 / RANK_SENTINEL) - 32.0, float(TOPK))


def _count(mask):
    return jnp.sum(mask.astype(F32), axis=0, keepdims=True)


def _pair_bf16(x):
    bits = lax.bitcast_convert_type(x.astype(BF16).astype(F32), jnp.uint32)
    return bits | (bits >> 16)


def _retrieve_tile(s_s, tt, r1_ref, b_ref, a_ref, c_ref, va_s, vb_s, exact):
    off = jnp.zeros((1, LANES), F32)
    for h in range(PEER_HEADS):
        s0 = s_s[2 * h, tt]
        s1 = s_s[2 * h + 1, tt]
        rank0 = _top16(s0, va_s, exact)
        rank1 = _top16(s1, vb_s, exact)
        va = va_s[...]
        vb = vb_s[...]
        cand = jnp.concatenate([va[0:1] + vb] + [va[k:k + 1] + vb[0:8] for k in range(1, 8)] + [va[8:16] + vb[0:1]],
                               axis=0)
        work = cand
        picked = jnp.zeros(cand.shape, jnp.bool_)
        for _ in range(TOPK):
            hit = _leave(work, exact)
            picked = picked | hit
            work = jnp.where(hit, -jnp.inf, work)
        top = va[0:1] + vb[0:1]
        z = jnp.sum(jnp.where(picked, jnp.exp(cand - top), 0.0), axis=0, keepdims=True)
        pickf = picked.astype(F32)
        counts = [jnp.sum(pickf[0:16], axis=0, keepdims=True)]
        counts += [jnp.sum(pickf[8 + 8 * k:16 + 8 * k], axis=0, keepdims=True) for k in range(1, 8)]
        counts += [pickf[72 + k:73 + k] for k in range(8)]
        c = jnp.zeros(s0.shape, F32)
        for k in range(TOPK):
            c = jnp.where(rank0 == float(k), counts[k], c)
        a = jnp.where(rank0 < float(TOPK), jnp.exp(s0 - va[0:1]) * (0.5 / z), 0.0)
        b = jnp.where(rank1 < float(TOPK), jnp.exp(s1 - vb[0:1]), 0.0)
        r1_ref[tt, h] = rank1.astype(BF16)
        b_ref[tt, h] = b.astype(BF16)
        a_rows = _pair_bf16(a)
        c_rows = _pair_bf16(c)
        for grp in range(N_KEYS // SUBLANES):
            rows = slice(grp * SUBLANES, (grp + 1) * SUBLANES)
            a_ref[tt, h, grp] = a_rows[rows]
            c_ref[tt, h, grp] = c_rows[rows]
        if not exact:
            for n_left in (_count(rank0 < float(TOPK)), _count(rank1 < float(TOPK)), _count(picked)):
                off = off + jnp.abs(n_left - float(TOPK))
    return off


def _retrieve_body(n2_ref, wq_ref, sk_ref, r1_ref, b_ref, a_ref, c_ref, s_s, va_s, vb_s, *, tb):
    q = _dot(n2_ref[...], wq_ref[...]).astype(BF16)
    for h in range(PEER_HEADS):
        qh = q[:, h * PAIR:(h + 1) * PAIR]
        for half in range(2):
            s = _dot_t(sk_ref[h, half], qh)
            for tt in range(tb // LANES):
                s_s[2 * h + half, tt] = s[:, tt * LANES:(tt + 1) * LANES]

    def tile(tt, carry):
        maps = (r1_ref, b_ref, a_ref, c_ref, va_s, vb_s)
        off = _retrieve_tile(s_s, tt, *maps, exact=False)

        @pl.when(jnp.max(off) > 0.0)
        def _():
            _retrieve_tile(s_s, tt, *maps, exact=True)

        return carry

    lax.fori_loop(0, tb // LANES, tile, 0)


def _retrieve(n2, w, tb):
    n = n2.shape[0]
    nt = tb // LANES
    maps = pl.BlockSpec((nt, PEER_HEADS, N_KEYS, LANES), lambda i: (i, 0, 0, 0))
    rows = pl.BlockSpec((nt, PEER_HEADS, N_KEYS // SUBLANES, SUBLANES, LANES), lambda i: (i, 0, 0, 0, 0))
    map_shape = (n // LANES, PEER_HEADS, N_KEYS, LANES)
    row_shape = (n // LANES, PEER_HEADS, N_KEYS // SUBLANES, SUBLANES, LANES)
    return pl.pallas_call(
        functools.partial(_retrieve_body, tb=tb),
        grid=(n // tb,),
        in_specs=[pl.BlockSpec((tb, D_MODEL), lambda i: (i, 0)), _const_spec(w["peer_wq"].shape),
                  _const_spec(w["peer_sk"].shape)],
        out_specs=[maps, maps, rows, rows],
        out_shape=[jax.ShapeDtypeStruct(map_shape, BF16), jax.ShapeDtypeStruct(map_shape, BF16),
                   jax.ShapeDtypeStruct(row_shape, jnp.uint32), jax.ShapeDtypeStruct(row_shape, jnp.uint32)],
        scratch_shapes=[pltpu.VMEM((2 * PEER_HEADS, nt, N_KEYS, LANES), F32), pltpu.VMEM((TOPK, LANES), F32),
                        pltpu.VMEM((TOPK, LANES), F32)],
        compiler_params=_params(("parallel",)),
        name="peer_retrieve",
    )(n2, w["peer_wq"], w["peer_sk"])


def _twice_gelu(x):
    k0 = math.sqrt(2.0 / math.pi)
    return x * (1.0 + jnp.tanh(x * (k0 + (k0 * 0.044715) * (x * x))))


def _row_tile(ref, tt, h, g, ii):
    row = jnp.broadcast_to(ref[tt, h, g, ii:ii + 1, :], (SUBLANES, LANES))
    packed = pltpu.bitcast(row, BF16)
    return jnp.concatenate([packed] * (N_KEYS // packed.shape[0]), axis=0)


def _dense_body(n2_ref, u_ref, vt_ref, r1_ref, b_ref, a_ref, c_ref, o_ref, acc_s, gate_s, r1_s, b_s, *, tb):
    g = pl.program_id(1)

    @pl.when(g == 0)
    def _():
        acc_s[...] = jnp.zeros(acc_s.shape, F32)
        r1_s[...] = r1_ref[...]
        b_s[...] = b_ref[...]

    rows_per_load = 2
    for tt in range(tb // LANES):
        lanes = slice(tt * LANES, (tt + 1) * LANES)
        for ii0 in range(0, SUBLANES, rows_per_load):
            gates = [None] * rows_per_load
            for h in range(PEER_HEADS):
                rank_tile = r1_s[tt, h]
                weight_tile = b_s[tt, h]
                for k in range(rows_per_load):
                    wgt = weight_tile * _row_tile(a_ref, tt, h, g, ii0 + k)
                    term = jnp.where(rank_tile < _row_tile(c_ref, tt, h, g, ii0 + k), wgt, jnp.zeros_like(wgt))
                    gates[k] = term if gates[k] is None else gates[k] + term
            for k in range(rows_per_load):
                gate_s[(ii0 + k) * N_KEYS:(ii0 + k + 1) * N_KEYS, lanes] = gates[k]

    hid = _dot_t(u_ref[...], n2_ref[...])
    weighted = gate_s[...] * _twice_gelu(hid.astype(BF16))
    acc_s[...] += _dot(vt_ref[0], weighted)

    @pl.when(g == pl.num_programs(1) - 1)
    def _():
        o_ref[...] = acc_s[...].T


def _dense(n2, u, vt, r1, b, a, c, tb):
    n = n2.shape[0]
    ec = EXPERT_CHUNK
    ng = N_EXPERTS // ec
    nt = tb // LANES
    once = pl.Buffered(1)
    maps = pl.BlockSpec((nt, PEER_HEADS, N_KEYS, LANES), lambda i, g: (i, 0, 0, 0), pipeline_mode=once)
    rows = pl.BlockSpec((nt, PEER_HEADS, ng, SUBLANES, LANES), lambda i, g: (i, 0, 0, 0, 0), pipeline_mode=once)
    return pl.pallas_call(
        functools.partial(_dense_body, tb=tb),
        grid=(n // tb, ng),
        in_specs=[pl.BlockSpec((tb, D_MODEL), lambda i, g: (i, 0), pipeline_mode=once),
                  pl.BlockSpec((ec, D_MODEL), lambda i, g: (g, 0)),
                  pl.BlockSpec((1, D_MODEL, ec), lambda i, g: (g, 0, 0)),
                  maps, maps, rows, rows],
        out_specs=pl.BlockSpec((tb, D_MODEL), lambda i, g: (i, 0)),
        out_shape=jax.ShapeDtypeStruct((n, D_MODEL), F32),
        scratch_shapes=[pltpu.VMEM((D_MODEL, tb), F32), pltpu.VMEM((ec, tb), BF16),
                        pltpu.VMEM((nt, PEER_HEADS, N_KEYS, LANES), BF16),
                        pltpu.VMEM((nt, PEER_HEADS, N_KEYS, LANES), BF16)],
        compiler_params=_params(("parallel", "arbitrary")),
        name="peer_dense",
    )(n2, u, vt, r1, b, a, c)


def _final_body(h_ref, peer_ref, p_ref, gple_ref, wg_ref, wp_ref, o_ref):
    h = h_ref[...] + peer_ref[...]
    zg = _dot(_rms(h, gple_ref[...]).astype(BF16), wg_ref[...])
    gate = 1.0 / (1.0 + jnp.exp(-zg))
    o_ref[...] = h + gate * _dot(p_ref[...].astype(BF16), wp_ref[...])


def _final(h, peer, p, w, tb):
    n = h.shape[0]
    row = lambda width: pl.BlockSpec((tb, width), lambda i: (i, 0))
    consts = [w["g_ple"], w["w_ple_gate"], w["w_ple_proj"]]
    return pl.pallas_call(
        _final_body,
        grid=(n // tb,),
        in_specs=[row(D_MODEL), row(D_MODEL), row(PLE_DIM)] + [_const_spec(a.shape) for a in consts],
        out_specs=row(D_MODEL),
        out_shape=jax.ShapeDtypeStruct((n, D_MODEL), F32),
        compiler_params=_params(("parallel",)),
        name="final",
    )(h, peer, p, *consts)


def _layer_weights(l, g_mix, w_in, b_f, qn_a, kn_a, qn_b, kn_b, w_up_a, w_up_b, w_out, g_ffn, peer_wq, peer_subkeys,
                   peer_u, peer_v, g_ple, w_ple_gate, w_ple_proj):
    o_f = 3 * W_MIX
    o_b = o_f + N_HEADS
    o_g = o_b + 3 * W_MIX
    wi = w_in[l]
    tile_heads = lambda g: jnp.tile(g[l].astype(F32), N_HEADS)[None, :]
    head_of = jnp.arange(W_MIX) // HEAD_DIM
    sk = peer_subkeys[l].astype(BF16)
    zeros = jnp.zeros_like(sk[:, 0])
    sk_pad = jnp.stack([jnp.concatenate([sk[:, 0], zeros], axis=-1), jnp.concatenate([zeros, sk[:, 1]], axis=-1)],
                       axis=1)
    return {
        "g_mix": g_mix[l][None, :],
        "w_a": wi[:, :o_f].astype(BF16),
        "w_f": jnp.pad(wi[:, o_f:o_b], ((0, 0), (0, LANES - N_HEADS))).astype(BF16),
        "w_b": wi[:, o_b:o_g].astype(BF16),
        "w_g": wi[:, o_g:].astype(BF16),
        "b_f": jnp.pad(b_f[l], (0, LANES - N_HEADS))[None, :],
        "qn_a": tile_heads(qn_a), "kn_a": tile_heads(kn_a), "qn_b": tile_heads(qn_b), "kn_b": tile_heads(kn_b),
        "msum": jnp.where(head_of[:, None] == head_of[None, :], 1.0 / HEAD_DIM, 0.0).astype(BF16),
        "w_up_a": w_up_a[l].astype(BF16), "w_up_b": w_up_b[l].astype(BF16), "w_out": w_out[l].astype(BF16),
        "g_ffn": g_ffn[l][None, :],
        "peer_wq": peer_wq[l].astype(BF16),
        "peer_sk": sk_pad,
        "peer_u": peer_u[l].astype(BF16),
        "peer_vt": peer_v[l].astype(BF16).reshape(-1, EXPERT_CHUNK, D_MODEL).transpose(0, 2, 1),
        "g_ple": g_ple[l][None, :],
        "w_ple_gate": w_ple_gate[l].astype(BF16),
        "w_ple_proj": w_ple_proj[l].astype(BF16),
    }


def _channel(x, ya, yb, ga, gb, p, w, tb, tb_dense):
    h1, n2 = _merge(x, ya, yb, ga, gb, w, tb)
    r1, b, a, c = _retrieve(n2, w, tb_dense)
    peer = _dense(n2, w["peer_u"], w["peer_vt"], r1, b, a, c, tb_dense)
    return _final(h1, peer, p, w, tb)


def _pad_rows(x, rows):
    return jnp.pad(x, ((0, 0), (0, rows - x.shape[1]), (0, 0)))


def kernel(x_prompt, x_sample, cache_a_k, cache_a_v, cache_a_logf, cache_b_k, cache_b_v, p_prompt, p_sample, g_mix, w_in, b_f, qn_a, kn_a, qn_b, kn_b, rel_bias_b, w_up_a, w_up_b, w_out, g_ffn, peer_wq, peer_subkeys, peer_u, peer_v, g_ple, w_ple_gate, w_ple_proj):
    depth = w_in.shape[0]
    bp, tp, _ = x_prompt.shape
    bs, ts, _ = x_sample.shape
    past = cache_a_k.shape[2]
    band_rows = min(BAND_PAST, tp)
    tq = 256
    hp = x_prompt.reshape(bp * tp, D_MODEL)
    hs = x_sample.reshape(bs * ts, D_MODEL)
    outs = [[] for _ in range(10)]
    for l in range(depth):
        w = _layer_weights(l, g_mix, w_in, b_f, qn_a, kn_a, qn_b, kn_b, w_up_a, w_up_b, w_out, g_ffn, peer_wq,
                           peer_subkeys, peer_u, peer_v, g_ple, w_ple_gate, w_ple_proj)
        qa, ka, va, lf, qb, kb, vb, ga, gb = _proj(hp, w, 512)
        as_seq = lambda z: z.reshape(bp, tp, z.shape[-1])
        ya = _fox_prompt(as_seq(qa), as_seq(ka), as_seq(va), as_seq(lf).transpose(0, 2, 1), tq)
        yb = _band_prompt(as_seq(qb), as_seq(kb), as_seq(vb), _band_bias_prompt(rel_bias_b[l], tq), tq)
        hp = _channel(hp, ya.reshape(-1, W_MIX), yb.reshape(-1, W_MIX), ga, gb, p_prompt[l].reshape(-1, PLE_DIM), w,
                      1024, 1024)
        heads = lambda z, b_, t_: z.reshape(b_, t_, N_HEADS, HEAD_DIM)
        outs[0].append(heads(ka, bp, tp)); outs[1].append(heads(va, bp, tp)); outs[2].append(as_seq(lf))
        newest = lambda z: heads(as_seq(z)[:, -band_rows:], bp, band_rows)
        outs[3].append(newest(kb)); outs[4].append(newest(vb))
        qa, ka, va, lf, qb, kb, vb, ga, gb = _proj(hs, w, bs * ts)
        as_seq = lambda z: z.reshape(bs, ts, z.shape[-1])
        lft = jnp.concatenate([cache_a_logf[l].astype(F32), as_seq(lf),
                               jnp.zeros((bs, LANES - ts, N_HEADS), F32)], axis=1).transpose(0, 2, 1)
        flat_cache = lambda z: z[l].reshape(bs, z.shape[2], W_MIX)
        ya = _fox_sample(as_seq(qa), flat_cache(cache_a_k), flat_cache(cache_a_v), _pad_rows(as_seq(ka), LANES),
                         _pad_rows(as_seq(va), LANES), lft, ts)
        yb = _band_sample(as_seq(qb), flat_cache(cache_b_k), flat_cache(cache_b_v), _pad_rows(as_seq(kb), LANES),
                          _pad_rows(as_seq(vb), LANES), _band_bias_sample(rel_bias_b[l], ts, cache_b_k.shape[2]))
        hs = _channel(hs, ya.reshape(-1, W_MIX), yb.reshape(-1, W_MIX), ga, gb, p_sample[l].reshape(-1, PLE_DIM), w,
                      bs * ts, bs * ts)
        outs[5].append(heads(ka, bs, ts)); outs[6].append(heads(va, bs, ts)); outs[7].append(as_seq(lf))
        outs[8].append(heads(kb, bs, ts)); outs[9].append(heads(vb, bs, ts))
    return (hp.reshape(bp, tp, D_MODEL), hs.reshape(bs, ts, D_MODEL)) + tuple(jnp.stack(o) for o in outs)
```
